```python
import jax, jax.numpy as jnp
from jax import lax
import numpy as np

D_MODEL = 2048
BATCH = 8
SEQ = 2048
DEPTH = 2

GLA_HEADS = 4
GLA_KEY_W = D_MODEL // 4
GLA_VAL_W = D_MODEL // 2
GLA_DK = GLA_KEY_W // GLA_HEADS
GLA_DV = GLA_VAL_W // GLA_HEADS
GLA_GATE_RANK = 16
GLA_GATE_NORM = 16.0
GLA_CHUNK = 64
MOBA_HEAD_DIM = 128
MOBA_W = D_MODEL // 2
MOBA_HEADS = MOBA_W // MOBA_HEAD_DIM
MOBA_BLOCK = 256
MOBA_TOPK = 3
MOBA_QCHUNK = 16
D_FF = ((8 * D_MODEL // 3 + 255) // 256) * 256
N_BRANCH = 2
EPS = 1e-6
IN_SPLITS = (GLA_KEY_W, GLA_KEY_W, GLA_VAL_W, GLA_VAL_W, GLA_GATE_RANK, MOBA_W, MOBA_W, MOBA_W, N_BRANCH * D_MODEL)
IN_COLS = GLA_KEY_W * 2 + GLA_VAL_W * 2 + GLA_GATE_RANK + MOBA_W * 3 + N_BRANCH * D_MODEL

kernel_name = 'hybrid_gla_moba_adaln_block'


def rms_norm(x, g):
    x32 = x.astype(jnp.float32)
    y = x32 * lax.rsqrt(jnp.mean(x32 * x32, axis=-1, keepdims=True) + EPS)
    return (y * g.astype(jnp.float32)).astype(x.dtype)


def gla_chunk_step(state, inp):
    q, k, v, g = inp
    C = q.shape[2]
    b = jnp.cumsum(g, axis=2)
    causal = jnp.tril(jnp.ones((C, C), dtype=bool))
    diff = b[:, :, :, None, :] - b[:, :, None, :, :]
    decay = jnp.exp(jnp.where(causal[None, None, :, :, None], diff, -jnp.inf))
    attn = jnp.einsum('bhtsd,bhsd->bhts', q[:, :, :, None, :] * decay, k)
    o = (jnp.einsum('bhts,bhsv->bhtv', attn, v)
         + jnp.einsum('bhtd,bhdv->bhtv', q * jnp.exp(b), state))
    b_last = b[:, :, -1:, :]
    state = (jnp.exp(b_last[:, :, 0, :])[..., None] * state
             + jnp.einsum('bhsd,bhsv->bhdv', k * jnp.exp(b_last - b), v))
    return state, o


def gla_mixer(q, k, v, log_a):
    B, S = q.shape[0], q.shape[1]
    nc = S // GLA_CHUNK

    def to_chunks(t):
        t = t.astype(jnp.float32).reshape(B, nc, GLA_CHUNK, GLA_HEADS, t.shape[-1])
        return jnp.transpose(t, (1, 0, 3, 2, 4))

    q = q * (GLA_DK ** -0.5)
    state0 = jnp.zeros((B, GLA_HEADS, GLA_DK, GLA_DV), jnp.float32)
    _, o = lax.scan(gla_chunk_step, state0,
                    (to_chunks(q), to_chunks(k), to_chunks(v), to_chunks(log_a)))
    return jnp.transpose(o, (1, 0, 3, 2, 4)).reshape(B, S, GLA_HEADS, GLA_DV)


def moba_mixer(q, k, v):
    B, S = q.shape[0], q.shape[1]
    H, hd, BS, QC = MOBA_HEADS, MOBA_HEAD_DIM, MOBA_BLOCK, MOBA_QCHUNK
    nb = -(-S // BS)
    sp = nb * BS
    pad = ((0, 0), (0, sp - S), (0, 0), (0, 0))
    q = jnp.transpose(jnp.pad(q, pad), (0, 2, 1, 3)) * (hd ** -0.5)
    k = jnp.transpose(jnp.pad(k, pad), (0, 2, 1, 3))
    v = jnp.transpose(jnp.pad(v, pad), (0, 2, 1, 3))
    kb = k.reshape(B, H, nb, BS, hd)
    vb = v.reshape(B, H, nb, BS, hd)
    k_mean = jnp.mean(kb.astype(jnp.float32), axis=3)
    q_block = jnp.arange(sp) // BS
    past = jnp.arange(nb)[None, :] < q_block[:, None]
    blk_score = jnp.einsum('bhsd,bhnd->bhsn', q.astype(jnp.float32), k_mean)
    blk_score = jnp.where(past[None, None], blk_score, -jnp.inf)
    topk = min(MOBA_TOPK, nb)
    _, sel = lax.top_k(blk_score, topk)
    nq = sp // QC

    def chunks(t):
        return jnp.moveaxis(t.reshape(B, H, nq, QC, t.shape[-1]), 2, 0)

    bi = jnp.arange(B)[:, None, None, None]
    hi = jnp.arange(H)[None, :, None, None]

    def attend(args):
        qc, selc, ci = args
        own = (ci * QC) // BS
        k_sel = kb[bi, hi, selc]
        v_sel = vb[bi, hi, selc]
        k_own = lax.dynamic_index_in_dim(kb, own, axis=2, keepdims=False)
        v_own = lax.dynamic_index_in_dim(vb, own, axis=2, keepdims=False)
        s_sel = jnp.einsum('bhqd,bhqjtd->bhqjt', qc, k_sel).reshape(B, H, QC, topk * BS)
        s_own = jnp.einsum('bhqd,bhtd->bhqt', qc, k_own)
        sel_ok = jnp.repeat(selc < own, BS, axis=-1)
        q_pos = ci * QC + jnp.arange(QC)
        k_pos = own * BS + jnp.arange(BS)
        own_ok = k_pos[None, :] <= q_pos[:, None]
        scores = jnp.concatenate(
            [jnp.where(sel_ok, s_sel.astype(jnp.float32), -jnp.inf),
             jnp.where(own_ok[None, None], s_own.astype(jnp.float32), -jnp.inf)], axis=-1)
        p = jax.nn.softmax(scores, axis=-1).astype(v.dtype)
        p_sel = p[..., :topk * BS].reshape(B, H, QC, topk, BS)
        return (jnp.einsum('bhqjt,bhqjtd->bhqd', p_sel, v_sel)
                + jnp.einsum('bhqt,bhtd->bhqd', p[..., topk * BS:], v_own))

    o = lax.map(attend, (chunks(q), chunks(sel), jnp.arange(nq)))
    o = jnp.moveaxis(o, 0, 2).reshape(B, H, sp, hd)[:, :, :S]
    return jnp.transpose(o, (0, 2, 1, 3))


def setup_inputs(seed: int = 0) -> dict:
    key = jax.random.key(seed)
    ks = jax.random.split(key, 16)

    def w(k, shape, fan_in):
        return jax.random.normal(k, shape, jnp.float32) * (fan_in ** -0.5)

    def gain(k, shape):
        return 1.0 + 0.05 * jax.random.normal(k, shape, jnp.float32)

    return {
        'x': jax.random.normal(ks[0], (BATCH, SEQ, D_MODEL), jnp.float32),
        'c': jax.random.normal(ks[1], (BATCH, D_MODEL), jnp.float32),
        'ada_w': w(ks[2], (DEPTH, D_MODEL, 6 * D_MODEL), D_MODEL),
        'ada_b': 0.02 * jax.random.normal(ks[3], (DEPTH, 6 * D_MODEL), jnp.float32),
        'norm1_g': gain(ks[4], (DEPTH, D_MODEL)),
        'w_in': w(ks[5], (DEPTH, D_MODEL, IN_COLS), D_MODEL),
        'gla_gate_w2': w(ks[6], (DEPTH, GLA_GATE_RANK, GLA_KEY_W), GLA_GATE_RANK),
        'gla_gate_b': 0.02 * jax.random.normal(ks[7], (DEPTH, GLA_KEY_W), jnp.float32),
        'gla_norm_g': gain(ks[8], (DEPTH, GLA_DV)),
        'w_up_gla': w(ks[9], (DEPTH, GLA_VAL_W, D_MODEL), GLA_VAL_W),
        'w_up_moba': w(ks[10], (DEPTH, MOBA_W, D_MODEL), MOBA_W),
        'w_out': w(ks[11], (DEPTH, D_MODEL, D_MODEL), D_MODEL),
        'norm2_g': gain(ks[12], (DEPTH, D_MODEL)),
        'w_ffn_in': w(ks[13], (DEPTH, D_MODEL, 2 * D_FF), D_MODEL),
        'w_ffn_out': w(ks[14], (DEPTH, D_FF, D_MODEL), D_FF),
        'final_g': gain(ks[15], (D_MODEL,)),
    }


def reference(x, c, ada_w, ada_b, norm1_g, w_in, gla_gate_w2, gla_gate_b, gla_norm_g,
              w_up_gla, w_up_moba, w_out, norm2_g, w_ffn_in, w_ffn_out, final_g):
    B, S = x.shape[0], x.shape[1]
    split_pts = np.cumsum(np.array(IN_SPLITS))[:-1].tolist()
    c_act = jax.nn.silu(c)
    for l in range(DEPTH):
        mod = c_act @ ada_w[l] + ada_b[l]
        sh1, sc1, gt1, sh2, sc2, gt2 = jnp.split(mod, 6, axis=-1)
        h = rms_norm(x, norm1_g[l]) * (1.0 + sc1[:, None]) + sh1[:, None]
        proj = h @ w_in[l]
        gq, gk, gv, gr, glr, mq, mk, mv, gates = jnp.split(proj, split_pts, axis=-1)
        log_a = jax.nn.log_sigmoid((glr @ gla_gate_w2[l] + gla_gate_b[l]).astype(jnp.float32)) / GLA_GATE_NORM
        o_gla = gla_mixer(gq.reshape(B, S, GLA_HEADS, GLA_DK), gk.reshape(B, S, GLA_HEADS, GLA_DK),
                          gv.reshape(B, S, GLA_HEADS, GLA_DV), log_a.reshape(B, S, GLA_HEADS, GLA_DK))
        o_gla = rms_norm(o_gla.astype(x.dtype), gla_norm_g[l]).reshape(B, S, GLA_VAL_W) * jax.nn.silu(gr)
        y_gla = o_gla @ w_up_gla[l]
        o_moba = moba_mixer(mq.reshape(B, S, MOBA_HEADS, MOBA_HEAD_DIM), mk.reshape(B, S, MOBA_HEADS, MOBA_HEAD_DIM),
                            mv.reshape(B, S, MOBA_HEADS, MOBA_HEAD_DIM)).reshape(B, S, MOBA_W)
        y_moba = o_moba @ w_up_moba[l]
        g_gla, g_moba = jnp.split(jax.nn.sigmoid(gates), 2, axis=-1)
        x = x + gt1[:, None] * ((g_gla * y_gla + g_moba * y_moba) @ w_out[l])
        h = rms_norm(x, norm2_g[l]) * (1.0 + sc2[:, None]) + sh2[:, None]
        a, u = jnp.split(h @ w_ffn_in[l], 2, axis=-1)
        x = x + gt2[:, None] * ((jax.nn.silu(a) * u) @ w_ffn_out[l])
    return rms_norm(x, final_g)
```

```python
import functools

import jax
import jax.numpy as jnp
from jax import lax
from jax.experimental import pallas as pl
from jax.experimental.pallas import tpu as pltpu

F32 = jnp.float32
BF16 = jnp.bfloat16

GLA_HEADS = 4
GLA_GATE_NORM = 16.0
GLA_CHUNK = 64
MOBA_HEAD_DIM = 128
MOBA_BLOCK = 256
MOBA_TOPK = 3
EPS = 1e-6

V7X_VMEM_BYTES = 64 * 1024 * 1024
VMEM_LIMIT_BYTES = V7X_VMEM_BYTES - 8 * 1024 * 1024
LANES = 128

_NT = (((1,), (1,)), ((), ()))
_TN = (((0,), (0,)), ((), ()))


def _params(semantics):
    return pltpu.CompilerParams(dimension_semantics=semantics, vmem_limit_bytes=VMEM_LIMIT_BYTES)


def _mod_norm(x, g, sc, sh):
    ms = jnp.mean(x * x, axis=-1, keepdims=True)
    y = x * lax.rsqrt(ms + EPS) * g
    return y * (1.0 + sc) + sh


def _adaln_kernel(c_ref, w_ref, b_ref, o_ref):
    c_act = jax.nn.silu(c_ref[...])
    o_ref[...] = (
        jnp.dot(c_act.astype(BF16), w_ref[...].astype(BF16), preferred_element_type=F32) + b_ref[...]
    )


def _adaln(c, ada_w, ada_b, *, tn=1024):
    depth, d, n6 = ada_w.shape
    b = c.shape[0]
    return pl.pallas_call(
        _adaln_kernel,
        grid=(depth, n6 // tn),
        in_specs=[
            pl.BlockSpec((b, d), lambda l, j: (0, 0)),
            pl.BlockSpec((None, d, tn), lambda l, j: (l, 0, j)),
            pl.BlockSpec((None, 1, tn), lambda l, j: (l, 0, j)),
        ],
        out_specs=pl.BlockSpec((None, b, tn), lambda l, j: (l, 0, j)),
        out_shape=jax.ShapeDtypeStruct((depth, b, n6), F32),
        compiler_params=_params(("arbitrary", "arbitrary")),
        name="adaln",
    )(c, ada_w, ada_b.reshape(depth, 1, n6))


def _in_proj_kernel(x_ref, g_ref, sc_ref, sh_ref, w_ref, wl_ref, o_ref, ol_ref, h_ref):
    @pl.when(pl.program_id(1) == 0)
    def _():
        h = _mod_norm(x_ref[...], g_ref[...], sc_ref[...], sh_ref[...]).astype(BF16)
        h_ref[...] = h
        ol_ref[...] = jnp.dot(h, wl_ref[...], preferred_element_type=F32).astype(ol_ref.dtype)

    o_ref[...] = jnp.dot(h_ref[...], w_ref[...], preferred_element_type=F32).astype(o_ref.dtype)


def _in_proj(x2, g, mod, w_main, w_lr, *, seq, tm=1024, tn=1024):
    n, d = x2.shape
    c = w_main.shape[1]
    per_b = seq // tm
    return pl.pallas_call(
        _in_proj_kernel,
        grid=(n // tm, c // tn),
        in_specs=[
            pl.BlockSpec((tm, d), lambda i, j: (i, 0)),
            pl.BlockSpec((1, d), lambda i, j: (0, 0)),
            pl.BlockSpec((None, None, 1, d), lambda i, j: (i // per_b, 1, 0, 0)),
            pl.BlockSpec((None, None, 1, d), lambda i, j: (i // per_b, 0, 0, 0)),
            pl.BlockSpec((d, tn), lambda i, j: (0, j)),
            pl.BlockSpec((d, LANES), lambda i, j: (0, 0)),
        ],
        out_specs=[
            pl.BlockSpec((tm, tn), lambda i, j: (i, j)),
            pl.BlockSpec((tm, LANES), lambda i, j: (i, 0)),
        ],
        out_shape=[
            jax.ShapeDtypeStruct((n, c), BF16),
            jax.ShapeDtypeStruct((n, LANES), BF16),
        ],
        scratch_shapes=[pltpu.VMEM((tm, d), BF16)],
        compiler_params=_params(("arbitrary", "arbitrary")),
        name="in_proj",
    )(x2, g, mod, mod, w_main, w_lr)


def _split3(x):
    hi = x.astype(BF16)
    r1 = x - hi.astype(F32)
    mid = r1.astype(BF16)
    lo = (r1 - mid.astype(F32)).astype(BF16)
    return hi, mid, lo


def _gla_kernel(q_ref, k_ref, v_ref, gr_ref, lr_ref, w2_ref, gb_ref, ng_ref, o_ref, st_ref, *, dk, dv):
    seq = q_ref.shape[0]
    cs = GLA_CHUNK
    st_ref[...] = jnp.zeros_like(st_ref)
    row = lax.broadcasted_iota(jnp.int32, (cs, cs), 0)
    col = lax.broadcasted_iota(jnp.int32, (cs, cs), 1)
    causal = col <= row
    tril = causal.astype(BF16)
    scale = dk ** -0.5

    def chunk(ci, carry):
        r = pl.ds(pl.multiple_of(ci * cs, cs), cs)
        q = q_ref[r, :].astype(F32) * scale
        k = k_ref[r, :].astype(F32)
        v = v_ref[r, :]
        xg = jnp.dot(lr_ref[r, :], w2_ref[...], preferred_element_type=F32) + gb_ref[...]
        g = jax.nn.log_sigmoid(xg) / GLA_GATE_NORM
        parts = jnp.concatenate(_split3(g), axis=1)
        bs = jnp.dot(tril, parts, preferred_element_type=F32)
        b = bs[:, :dk] + bs[:, dk:2 * dk] + bs[:, 2 * dk:]
        b_last = b[cs - 1:cs, :]
        qd = (q * jnp.exp(b)).astype(BF16)
        kd = (k * jnp.exp(-b)).astype(BF16)
        ke = (k * jnp.exp(b_last - b)).astype(BF16)
        attn = lax.dot_general(qd, kd, _NT, preferred_element_type=F32)
        attn = jnp.where(causal, attn, 0.0).astype(BF16)
        st = st_ref[...]
        o = jnp.dot(attn, v, preferred_element_type=F32)
        o = o + lax.dot_general(qd, st.astype(BF16), _NT, preferred_element_type=F32)
        vt = v.astype(F32).T.astype(BF16)
        st_ref[...] = st * jnp.exp(b_last) + jnp.dot(vt, ke, preferred_element_type=F32)
        ms = jnp.mean(o * o, axis=-1, keepdims=True)
        y = o * lax.rsqrt(ms + EPS) * ng_ref[...]
        o_ref[r, :] = (y * jax.nn.silu(gr_ref[r, :].astype(F32))).astype(o_ref.dtype)
        return carry

    lax.fori_loop(0, seq // cs, chunk, 0)


def _gla(proj, lr, w2p, gate_b, norm_g, *, batch, seq, key_w, val_w):
    n = proj.shape[0]
    dk = key_w // GLA_HEADS
    dv = val_w // GLA_HEADS
    kb = key_w // dk
    vb = 2 * key_w // dv
    rb = (2 * key_w + val_w) // dv
    kern = functools.partial(_gla_kernel, dk=dk, dv=dv)
    return pl.pallas_call(
        kern,
        grid=(batch, GLA_HEADS),
        in_specs=[
            pl.BlockSpec((seq, dk), lambda b, h: (b, h)),
            pl.BlockSpec((seq, dk), lambda b, h: (b, kb + h)),
            pl.BlockSpec((seq, dv), lambda b, h: (b, vb + h)),
            pl.BlockSpec((seq, dv), lambda b, h: (b, rb + h)),
            pl.BlockSpec((seq, LANES), lambda b, h: (b, 0)),
            pl.BlockSpec((LANES, dk), lambda b, h: (0, h)),
            pl.BlockSpec((1, dk), lambda b, h: (0, h)),
            pl.BlockSpec((1, dv), lambda b, h: (0, 0)),
        ],
        out_specs=pl.BlockSpec((seq, dv), lambda b, h: (b, h)),
        out_shape=jax.ShapeDtypeStruct((n, val_w), BF16),
        scratch_shapes=[pltpu.VMEM((dv, dk), F32)],
        compiler_params=_params(("arbitrary", "arbitrary")),
        name="gla",
    )(proj, proj, proj, proj, lr, w2p, gate_b, norm_g)


def _moba_kernel(q_ref, k_ref, v_ref, o_ref, km_ref):
    bs = MOBA_BLOCK
    hd = MOBA_HEAD_DIM
    seq = k_ref.shape[0]
    nb = seq // bs
    i = pl.program_id(2)

    @pl.when(i == 0)
    def _():
        km_ref[...] = jnp.zeros_like(km_ref)
        kf = k_ref[...].astype(F32)
        km_ref[0:nb, :] = jnp.mean(kf.reshape(nb, bs, hd), axis=1)

    q = q_ref[...]
    sc = lax.dot_general(q, km_ref[...].astype(BF16), _NT, preferred_element_type=F32)
    lane = lax.broadcasted_iota(jnp.int32, sc.shape, 1)
    past = lane < i
    sc = jnp.where(past, sc, -jnp.inf)
    beaten = jnp.zeros(sc.shape, F32)
    for jp in range(nb):
        c = sc[:, jp:jp + 1]
        ahead = (c > sc) | ((c == sc) & (lane > jp))
        beaten = beaten + ahead.astype(F32)
    sel = (past & (beaten < MOBA_TOPK)).astype(F32)

    s = lax.dot_general(q, k_ref[...], _NT, preferred_element_type=F32) * (hd ** -0.5)
    row = lax.broadcasted_iota(jnp.int32, (bs, bs), 0)
    col = lax.broadcasted_iota(jnp.int32, (bs, bs), 1)
    tri = col <= row
    iv = jnp.full((bs, 1), i, jnp.int32)
    blocks = []
    for j in range(nb):
        ok = ((iv > j) & (sel[:, j:j + 1] > 0.5)) | ((iv == j) & tri)
        blocks.append(jnp.where(ok, s[:, j * bs:(j + 1) * bs], -jnp.inf))
    m = blocks[0].max(axis=-1, keepdims=True)
    for j in range(1, nb):
        m = jnp.maximum(m, blocks[j].max(axis=-1, keepdims=True))
    l = jnp.zeros((bs, 1), F32)
    o = jnp.zeros((bs, hd), F32)
    for j in range(nb):
        p = jnp.exp(blocks[j] - m)
        l = l + p.sum(axis=-1, keepdims=True)
        o = o + jnp.dot(p.astype(BF16), v_ref[j * bs:(j + 1) * bs, :], preferred_element_type=F32)
    o_ref[...] = (o / l).astype(o_ref.dtype)


def _moba(proj, *, batch, seq, col0, width):
    n = proj.shape[0]
    hd = MOBA_HEAD_DIM
    heads = width // hd
    nqb = seq // MOBA_BLOCK
    qb = col0 // hd
    kb = qb + heads
    vb = kb + heads
    return pl.pallas_call(
        _moba_kernel,
        grid=(batch, heads, nqb),
        in_specs=[
            pl.BlockSpec((MOBA_BLOCK, hd), lambda b, h, i: (b * nqb + i, qb + h)),
            pl.BlockSpec((seq, hd), lambda b, h, i: (b, kb + h)),
            pl.BlockSpec((seq, hd), lambda b, h, i: (b, vb + h)),
        ],
        out_specs=pl.BlockSpec((MOBA_BLOCK, hd), lambda b, h, i: (b * nqb + i, h)),
        out_shape=jax.ShapeDtypeStruct((n, width), BF16),
        scratch_shapes=[pltpu.VMEM((LANES, hd), F32)],
        compiler_params=_params(("arbitrary", "arbitrary", "arbitrary")),
        name="moba",
    )(proj, proj, proj)


def _merge_kernel(og_ref, om_ref, wg_ref, wm_ref, g1_ref, g2_ref, wo_ref, x_ref, gt_ref, o_ref):
    j = pl.program_id(1)
    yg = jnp.dot(og_ref[...], wg_ref[...], preferred_element_type=F32)
    ym = jnp.dot(om_ref[...], wm_ref[...], preferred_element_type=F32)
    z = jax.nn.sigmoid(g1_ref[...].astype(F32)) * yg + jax.nn.sigmoid(g2_ref[...].astype(F32)) * ym
    part = jnp.dot(z.astype(BF16), wo_ref[...], preferred_element_type=F32)

    @pl.when(j == 0)
    def _():
        o_ref[...] = part

    @pl.when(j > 0)
    def _():
        o_ref[...] += part

    @pl.when(j == pl.num_programs(1) - 1)
    def _():
        o_ref[...] = x_ref[...] + gt_ref[...] * o_ref[...]


def _merge(o_gla, o_moba, w_up_gla, w_up_moba, proj, gate_col0, w_out, x2, mod, *, seq, tm=512, tn=512):
    n, d = x2.shape
    per_b = seq // tm
    g1b = gate_col0 // tn
    g2b = (gate_col0 + d) // tn
    kg = o_gla.shape[1]
    km = o_moba.shape[1]
    return pl.pallas_call(
        _merge_kernel,
        grid=(n // tm, d // tn),
        in_specs=[
            pl.BlockSpec((tm, kg), lambda i, j: (i, 0)),
            pl.BlockSpec((tm, km), lambda i, j: (i, 0)),
            pl.BlockSpec((kg, tn), lambda i, j: (0, j)),
            pl.BlockSpec((km, tn), lambda i, j: (0, j)),
            pl.BlockSpec((tm, tn), lambda i, j: (i, g1b + j)),
            pl.BlockSpec((tm, tn), lambda i, j: (i, g2b + j)),
            pl.BlockSpec((tn, d), lambda i, j: (j, 0)),
            pl.BlockSpec((tm, d), lambda i, j: (i, 0)),
            pl.BlockSpec((None, None, 1, d), lambda i, j: (i // per_b, 2, 0, 0)),
        ],
        out_specs=pl.BlockSpec((tm, d), lambda i, j: (i, 0)),
        out_shape=jax.ShapeDtypeStruct((n, d), F32),
        compiler_params=_params(("arbitrary", "arbitrary")),
        name="merge",
    )(o_gla, o_moba, w_up_gla, w_up_moba, proj, proj, w_out, x2, mod)


def _ffn_kernel(x_ref, g_ref, sc_ref, sh_ref, gt_ref, wa_ref, wu_ref, wo_ref, fg_ref, o_ref, h_ref, *, final_norm):
    j = pl.program_id(1)

    @pl.when(j == 0)
    def _():
        h_ref[...] = _mod_norm(x_ref[...], g_ref[...], sc_ref[...], sh_ref[...]).astype(BF16)

    h = h_ref[...]
    a = jnp.dot(h, wa_ref[...], preferred_element_type=F32)
    u = jnp.dot(h, wu_ref[...], preferred_element_type=F32)
    part = jnp.dot((jax.nn.silu(a) * u).astype(BF16), wo_ref[...], preferred_element_type=F32)

    @pl.when(j == 0)
    def _():
        o_ref[...] = part

    @pl.when(j > 0)
    def _():
        o_ref[...] += part

    @pl.when(j == pl.num_programs(1) - 1)
    def _():
        y = x_ref[...] + gt_ref[...] * o_ref[...]
        if final_norm:
            ms = jnp.mean(y * y, axis=-1, keepdims=True)
            y = y * lax.rsqrt(ms + EPS) * fg_ref[...]
        o_ref[...] = y


def _ffn(x2, g, mod, w_in, w_out, final_g, *, seq, final_norm, tm=512, tf=512):
    n, d = x2.shape
    dff = w_out.shape[0]
    per_b = seq // tm
    nf = dff // tf
    kern = functools.partial(_ffn_kernel, final_norm=final_norm)
    return pl.pallas_call(
        kern,
        grid=(n // tm, nf),
        in_specs=[
            pl.BlockSpec((tm, d), lambda i, j: (i, 0)),
            pl.BlockSpec((1, d), lambda i, j: (0, 0)),
            pl.BlockSpec((None, None, 1, d), lambda i, j: (i // per_b, 4, 0, 0)),
            pl.BlockSpec((None, None, 1, d), lambda i, j: (i // per_b, 3, 0, 0)),
            pl.BlockSpec((None, None, 1, d), lambda i, j: (i // per_b, 5, 0, 0)),
            pl.BlockSpec((d, tf), lambda i, j: (0, j)),
            pl.BlockSpec((d, tf), lambda i, j: (0, nf + j)),
            pl.BlockSpec((tf, d), lambda i, j: (j, 0)),
            pl.BlockSpec((1, d), lambda i, j: (0, 0)),
        ],
        out_specs=pl.BlockSpec((tm, d), lambda i, j: (i, 0)),
        out_shape=jax.ShapeDtypeStruct((n, d), F32),
        scratch_shapes=[pltpu.VMEM((tm, d), BF16)],
        compiler_params=_params(("arbitrary", "arbitrary")),
        name="ffn",
    )(x2, g, mod, mod, mod, w_in, w_in, w_out, final_g)


def kernel(x, c, ada_w, ada_b, norm1_g, w_in, gla_gate_w2, gla_gate_b, gla_norm_g, w_up_gla, w_up_moba,
           w_out, norm2_g, w_ffn_in, w_ffn_out, final_g):
    batch, seq, d = x.shape
    depth = ada_w.shape[0]
    rank, key_w = gla_gate_w2.shape[1:]
    val_w = w_up_gla.shape[1]
    moba_w = w_up_moba.shape[1]
    n = batch * seq
    lr0 = 2 * key_w + 2 * val_w
    moba0 = lr0
    gate0 = moba0 + 3 * moba_w

    mod = _adaln(c, ada_w, ada_b).reshape(depth, batch, 6, 1, d)
    x2 = x.reshape(n, d)
    for l in range(depth):
        w_main = jnp.concatenate([w_in[l, :, :lr0], w_in[l, :, lr0 + rank:]], axis=1).astype(BF16)
        w_lr = jnp.pad(w_in[l, :, lr0:lr0 + rank], ((0, 0), (0, LANES - rank))).astype(BF16)
        w2p = jnp.pad(gla_gate_w2[l], ((0, LANES - rank), (0, 0))).astype(BF16)
        proj, lr = _in_proj(x2, norm1_g[l][None], mod[l], w_main, w_lr, seq=seq)
        o_gla = _gla(proj, lr, w2p, gla_gate_b[l][None], gla_norm_g[l][None],
                     batch=batch, seq=seq, key_w=key_w, val_w=val_w)
        o_moba = _moba(proj, batch=batch, seq=seq, col0=moba0, width=moba_w)
        x2 = _merge(o_gla, o_moba, w_up_gla[l].astype(BF16), w_up_moba[l].astype(BF16), proj, gate0,
                    w_out[l].astype(BF16), x2, mod[l], seq=seq)
        x2 = _ffn(x2, norm2_g[l][None], mod[l], w_ffn_in[l].astype(BF16), w_ffn_out[l].astype(BF16),
                  final_g[None], seq=seq, final_norm=(l == depth - 1))
    return x2.reshape(batch, seq, d)
```

```python
import functools

import jax
import jax.numpy as jnp
from jax import lax
from jax.experimental import pallas as pl
from jax.experimental.pallas import tpu as pltpu

F32 = jnp.float32
BF16 = jnp.bfloat16

GLA_HEADS = 4
GLA_GATE_NORM = 16.0
GLA_CHUNK = 64
MOBA_HEAD_DIM = 128
MOBA_BLOCK = 256
MOBA_TOPK = 3
EPS = 1e-6

V7X_VMEM_BYTES = 64 * 1024 * 1024
VMEM_LIMIT_BYTES = V7X_VMEM_BYTES - 8 * 1024 * 1024
LANES = 128
BF16_SUBLANES = 16

_NT = (((1,), (1,)), ((), ()))


def _params(semantics):
    return pltpu.CompilerParams(dimension_semantics=semantics, vmem_limit_bytes=VMEM_LIMIT_BYTES)


def _mod_norm(x, g, sc, sh):
    ms = jnp.mean(x * x, axis=-1, keepdims=True)
    y = x * lax.rsqrt(ms + EPS) * g
    return y * (1.0 + sc) + sh


def _adaln_kernel(c_ref, w_ref, b_ref, o_ref):
    c_act = jax.nn.silu(c_ref[...])
    o_ref[...] = (
        jnp.dot(c_act.astype(BF16), w_ref[...].astype(BF16), preferred_element_type=F32) + b_ref[...]
    )


def _adaln(c, ada_w, ada_b, *, tn=1024):
    depth, d, n6 = ada_w.shape
    b = c.shape[0]
    return pl.pallas_call(
        _adaln_kernel,
        grid=(depth, n6 // tn),
        in_specs=[
            pl.BlockSpec((b, d), lambda l, j: (0, 0)),
            pl.BlockSpec((None, d, tn), lambda l, j: (l, 0, j)),
            pl.BlockSpec((None, 1, tn), lambda l, j: (l, 0, j)),
        ],
        out_specs=pl.BlockSpec((None, b, tn), lambda l, j: (l, 0, j)),
        out_shape=jax.ShapeDtypeStruct((depth, b, n6), F32),
        compiler_params=_params(("arbitrary", "arbitrary")),
        name="adaln",
    )(c, ada_w, ada_b.reshape(depth, 1, n6))


def _in_proj_kernel(x_ref, g_ref, sc_ref, sh_ref, w_ref, wl_ref, o_ref, ol_ref, h_ref):
    @pl.when(pl.program_id(1) == 0)
    def _():
        h = _mod_norm(x_ref[...], g_ref[...], sc_ref[...], sh_ref[...]).astype(BF16)
        h_ref[...] = h
        ol_ref[...] = jnp.dot(h, wl_ref[...], preferred_element_type=F32).astype(ol_ref.dtype)

    o_ref[...] = jnp.dot(h_ref[...], w_ref[...], preferred_element_type=F32).astype(o_ref.dtype)


def _in_proj(x2, g, mod, w_main, w_lr, *, seq, tm=1024, tn=1024):
    n, d = x2.shape
    c = w_main.shape[1]
    per_b = seq // tm
    return pl.pallas_call(
        _in_proj_kernel,
        grid=(n // tm, c // tn),
        in_specs=[
            pl.BlockSpec((tm, d), lambda i, j: (i, 0)),
            pl.BlockSpec((1, d), lambda i, j: (0, 0)),
            pl.BlockSpec((None, None, 1, d), lambda i, j: (i // per_b, 1, 0, 0)),
            pl.BlockSpec((None, None, 1, d), lambda i, j: (i // per_b, 0, 0, 0)),
            pl.BlockSpec((d, tn), lambda i, j: (0, j)),
            pl.BlockSpec((d, LANES), lambda i, j: (0, 0)),
        ],
        out_specs=[
            pl.BlockSpec((tm, tn), lambda i, j: (i, j)),
            pl.BlockSpec((tm, LANES), lambda i, j: (i, 0)),
        ],
        out_shape=[
            jax.ShapeDtypeStruct((n, c), BF16),
            jax.ShapeDtypeStruct((n, LANES), BF16),
        ],
        scratch_shapes=[pltpu.VMEM((tm, d), BF16)],
        compiler_params=_params(("arbitrary", "arbitrary")),
        name="in_proj",
    )(x2, g, mod, mod, w_main, w_lr)


def _split3(x):
    hi = x.astype(BF16)
    r1 = x - hi.astype(F32)
    mid = r1.astype(BF16)
    lo = (r1 - mid.astype(F32)).astype(BF16)
    return hi, mid, lo


def _gla_kernel(q_ref, k_ref, v_ref, gr_ref, lr_ref, w2_ref, gb_ref, ng_ref, o_ref, g_ref, st_ref,
                *, heads, dk, dv, gate_rows):
    seq = q_ref.shape[0]
    cs = GLA_CHUNK
    kw = heads * dk

    def gate(pi, carry):
        r = pl.ds(pl.multiple_of(pi * gate_rows, gate_rows), gate_rows)
        xg = jnp.dot(lr_ref[r, :], w2_ref[...], preferred_element_type=F32) + gb_ref[...]
        g_ref[r, :] = jax.nn.log_sigmoid(xg) / GLA_GATE_NORM
        return carry

    lax.fori_loop(0, seq // gate_rows, gate, 0)
    st_ref[...] = jnp.zeros_like(st_ref)
    row = lax.broadcasted_iota(jnp.int32, (cs, cs), 0)
    col = lax.broadcasted_iota(jnp.int32, (cs, cs), 1)
    causal = col <= row
    tril = causal.astype(BF16)
    scale = dk ** -0.5

    def chunk(ci, carry):
        r = pl.ds(pl.multiple_of(ci * cs, cs), cs)
        parts = jnp.concatenate(_split3(g_ref[r, :]), axis=1)
        bs = jnp.dot(tril, parts, preferred_element_type=F32)
        b = bs[:, :kw] + bs[:, kw:2 * kw] + bs[:, 2 * kw:]
        b_last = b[cs - 1:cs, :]
        q = q_ref[r, :].astype(F32) * scale
        k = k_ref[r, :].astype(F32)
        qd = (q * jnp.exp(b)).astype(BF16)
        kd = (k * jnp.exp(-b)).astype(BF16)
        ke = (k * jnp.exp(b_last - b)).astype(BF16)
        e_last = jnp.exp(b_last)
        for h in range(heads):
            ks = slice(h * dk, (h + 1) * dk)
            vs = slice(h * dv, (h + 1) * dv)
            attn = lax.dot_general(qd[:, ks], kd[:, ks], _NT, preferred_element_type=F32)
            attn = jnp.where(causal, attn, 0.0).astype(BF16)
            v = v_ref[r, vs]
            st = st_ref[h]
            o = jnp.dot(attn, v, preferred_element_type=F32)
            o = o + lax.dot_general(qd[:, ks], st.astype(BF16), _NT, preferred_element_type=F32)
            vt = v.astype(F32).T.astype(BF16)
            st_ref[h] = st * e_last[:, ks] + jnp.dot(vt, ke[:, ks], preferred_element_type=F32)
            ms = jnp.mean(o * o, axis=-1, keepdims=True)
            y = o * lax.rsqrt(ms + EPS) * ng_ref[...]
            o_ref[r, vs] = (y * jax.nn.silu(gr_ref[r, vs].astype(F32))).astype(o_ref.dtype)
        return carry

    lax.fori_loop(0, seq // cs, chunk, 0, unroll=2)


def _gla(proj, lr, w2p, gate_b, norm_g, *, batch, seq, key_w, val_w, gate_rows=512):
    n = proj.shape[0]
    dk = key_w // GLA_HEADS
    dv = val_w // GLA_HEADS
    vb = 2 * key_w // val_w
    rb = vb + 1
    kern = functools.partial(_gla_kernel, heads=GLA_HEADS, dk=dk, dv=dv, gate_rows=gate_rows)
    return pl.pallas_call(
        kern,
        grid=(batch,),
        in_specs=[
            pl.BlockSpec((seq, key_w), lambda b: (b, 0)),
            pl.BlockSpec((seq, key_w), lambda b: (b, 1)),
            pl.BlockSpec((seq, val_w), lambda b: (b, vb)),
            pl.BlockSpec((seq, val_w), lambda b: (b, rb)),
            pl.BlockSpec((seq, LANES), lambda b: (b, 0)),
            pl.BlockSpec((LANES, key_w), lambda b: (0, 0)),
            pl.BlockSpec((1, key_w), lambda b: (0, 0)),
            pl.BlockSpec((1, dv), lambda b: (0, 0)),
        ],
        out_specs=pl.BlockSpec((seq, val_w), lambda b: (b, 0)),
        out_shape=jax.ShapeDtypeStruct((n, val_w), BF16),
        scratch_shapes=[pltpu.VMEM((seq, key_w), F32), pltpu.VMEM((GLA_HEADS, dv, dk), F32)],
        compiler_params=_params(("arbitrary",)),
        name="gla",
    )(proj, proj, proj, proj, lr, w2p, gate_b, norm_g)


def _moba_kernel(q_ref, k_ref, v_ref, o_ref, vt_ref):
    bs = MOBA_BLOCK
    hd = MOBA_HEAD_DIM
    seq = k_ref.shape[0]
    nb = seq // bs
    scale = hd ** -0.5

    km = jnp.mean(k_ref[...].astype(F32).reshape(nb, bs, hd), axis=1)
    km = jnp.concatenate([km, jnp.zeros((BF16_SUBLANES - nb, hd), F32)], axis=0).astype(BF16)
    sc = lax.dot_general(km, q_ref[...], _NT, preferred_element_type=F32)[0:nb, :]
    blk = lax.broadcasted_iota(jnp.int32, sc.shape, 0)
    qpos = lax.broadcasted_iota(jnp.int32, sc.shape, 1)
    past = (blk + 1) * bs <= qpos
    sc = jnp.where(past, sc, -jnp.inf)
    beaten = jnp.zeros(sc.shape, F32)
    for jp in range(nb):
        c = sc[jp:jp + 1, :]
        ahead = (c > sc) | ((c == sc) & (blk > jp))
        beaten = beaten + ahead.astype(F32)
    sel_bias = jnp.where(past & (beaten < MOBA_TOPK), 0.0, -jnp.inf)

    vt_ref[...] = v_ref[...].astype(F32).T.astype(BF16)
    krow = lax.broadcasted_iota(jnp.int32, (bs, bs), 0)
    qcol = lax.broadcasted_iota(jnp.int32, (bs, bs), 1)
    own_bias = jnp.where(krow <= qcol, 0.0, -jnp.inf)

    for i in range(nb):
        qs = slice(i * bs, (i + 1) * bs)
        nk = (i + 1) * bs
        s = lax.dot_general(k_ref[0:nk, :], q_ref[qs, :], _NT, preferred_element_type=F32)
        pieces = [s[j * bs:(j + 1) * bs, :] + sel_bias[j:j + 1, qs] for j in range(i)]
        pieces.append(s[i * bs:nk, :] + own_bias)
        m = pieces[0].max(axis=0, keepdims=True)
        for pc in pieces[1:]:
            m = jnp.maximum(m, pc.max(axis=0, keepdims=True))
        ps = [jnp.exp((pc - m) * scale) for pc in pieces]
        l = ps[0].sum(axis=0, keepdims=True)
        for p in ps[1:]:
            l = l + p.sum(axis=0, keepdims=True)
        pt = jnp.concatenate([p.astype(BF16) for p in ps], axis=0) if i else ps[0].astype(BF16)
        ot = jnp.dot(vt_ref[:, 0:nk], pt, preferred_element_type=F32)
        o_ref[qs, :] = (ot / l).T.astype(o_ref.dtype)


def _moba(proj, *, batch, seq, col0, width):
    n = proj.shape[0]
    hd = MOBA_HEAD_DIM
    heads = width // hd
    qb = col0 // hd
    kb = qb + heads
    vb = kb + heads
    return pl.pallas_call(
        _moba_kernel,
        grid=(batch, heads),
        in_specs=[
            pl.BlockSpec((seq, hd), lambda b, h: (b, qb + h)),
            pl.BlockSpec((seq, hd), lambda b, h: (b, kb + h)),
            pl.BlockSpec((seq, hd), lambda b, h: (b, vb + h)),
        ],
        out_specs=pl.BlockSpec((seq, hd), lambda b, h: (b, h)),
        out_shape=jax.ShapeDtypeStruct((n, width), BF16),
        scratch_shapes=[pltpu.VMEM((hd, seq), BF16)],
        compiler_params=_params(("arbitrary", "arbitrary")),
        name="moba",
    )(proj, proj, proj)


def _merge_kernel(og_ref, om_ref, wg_ref, wm_ref, g1_ref, g2_ref, wo_ref, x_ref, gt_ref, o_ref):
    j = pl.program_id(1)
    yg = jnp.dot(og_ref[...], wg_ref[...], preferred_element_type=F32)
    ym = jnp.dot(om_ref[...], wm_ref[...], preferred_element_type=F32)
    z = jax.nn.sigmoid(g1_ref[...].astype(F32)) * yg + jax.nn.sigmoid(g2_ref[...].astype(F32)) * ym
    part = jnp.dot(z.astype(BF16), wo_ref[...], preferred_element_type=F32)

    @pl.when(j == 0)
    def _():
        o_ref[...] = part

    @pl.when(j > 0)
    def _():
        o_ref[...] += part

    @pl.when(j == pl.num_programs(1) - 1)
    def _():
        o_ref[...] = x_ref[...] + gt_ref[...] * o_ref[...]


def _merge(o_gla, o_moba, w_up_gla, w_up_moba, proj, gate_col0, w_out, x2, mod, *, seq, tm=512, tn=512):
    n, d = x2.shape
    per_b = seq // tm
    g1b = gate_col0 // tn
    g2b = (gate_col0 + d) // tn
    kg = o_gla.shape[1]
    km = o_moba.shape[1]
    return pl.pallas_call(
        _merge_kernel,
        grid=(n // tm, d // tn),
        in_specs=[
            pl.BlockSpec((tm, kg), lambda i, j: (i, 0)),
            pl.BlockSpec((tm, km), lambda i, j: (i, 0)),
            pl.BlockSpec((kg, tn), lambda i, j: (0, j)),
            pl.BlockSpec((km, tn), lambda i, j: (0, j)),
            pl.BlockSpec((tm, tn), lambda i, j: (i, g1b + j)),
            pl.BlockSpec((tm, tn), lambda i, j: (i, g2b + j)),
            pl.BlockSpec((tn, d), lambda i, j: (j, 0)),
            pl.BlockSpec((tm, d), lambda i, j: (i, 0)),
            pl.BlockSpec((None, None, 1, d), lambda i, j: (i // per_b, 2, 0, 0)),
        ],
        out_specs=pl.BlockSpec((tm, d), lambda i, j: (i, 0)),
        out_shape=jax.ShapeDtypeStruct((n, d), F32),
        compiler_params=_params(("arbitrary", "arbitrary")),
        name="merge",
    )(o_gla, o_moba, w_up_gla, w_up_moba, proj, proj, w_out, x2, mod)


def _ffn_kernel(x_ref, g_ref, sc_ref, sh_ref, gt_ref, wa_ref, wu_ref, wo_ref, fg_ref, o_ref, h_ref, *, final_norm):
    j = pl.program_id(1)

    @pl.when(j == 0)
    def _():
        h_ref[...] = _mod_norm(x_ref[...], g_ref[...], sc_ref[...], sh_ref[...]).astype(BF16)

    h = h_ref[...]
    a = jnp.dot(h, wa_ref[...], preferred_element_type=F32)
    u = jnp.dot(h, wu_ref[...], preferred_element_type=F32)
    part = jnp.dot((jax.nn.silu(a) * u).astype(BF16), wo_ref[...], preferred_element_type=F32)

    @pl.when(j == 0)
    def _():
        o_ref[...] = part

    @pl.when(j > 0)
    def _():
        o_ref[...] += part

    @pl.when(j == pl.num_programs(1) - 1)
    def _():
        y = x_ref[...] + gt_ref[...] * o_ref[...]
        if final_norm:
            ms = jnp.mean(y * y, axis=-1, keepdims=True)
            y = y * lax.rsqrt(ms + EPS) * fg_ref[...]
        o_ref[...] = y


def _ffn(x2, g, mod, w_in, w_out, final_g, *, seq, final_norm, tm=512, tf=512):
    n, d = x2.shape
    dff = w_out.shape[0]
    per_b = seq // tm
    nf = dff // tf
    kern = functools.partial(_ffn_kernel, final_norm=final_norm)
    return pl.pallas_call(
        kern,
        grid=(n // tm, nf),
        in_specs=[
            pl.BlockSpec((tm, d), lambda i, j: (i, 0)),
            pl.BlockSpec((1, d), lambda i, j: (0, 0)),
            pl.BlockSpec((None, None, 1, d), lambda i, j: (i // per_b, 4, 0, 0)),
            pl.BlockSpec((None, None, 1, d), lambda i, j: (i // per_b, 3, 0, 0)),
            pl.BlockSpec((None, None, 1, d), lambda i, j: (i // per_b, 5, 0, 0)),
            pl.BlockSpec((d, tf), lambda i, j: (0, j)),
            pl.BlockSpec((d, tf), lambda i, j: (0, nf + j)),
            pl.BlockSpec((tf, d), lambda i, j: (j, 0)),
            pl.BlockSpec((1, d), lambda i, j: (0, 0)),
        ],
        out_specs=pl.BlockSpec((tm, d), lambda i, j: (i, 0)),
        out_shape=jax.ShapeDtypeStruct((n, d), F32),
        scratch_shapes=[pltpu.VMEM((tm, d), BF16)],
        compiler_params=_params(("arbitrary", "arbitrary")),
        name="ffn",
    )(x2, g, mod, mod, mod, w_in, w_in, w_out, final_g)


def kernel(x, c, ada_w, ada_b, norm1_g, w_in, gla_gate_w2, gla_gate_b, gla_norm_g, w_up_gla, w_up_moba,
           w_out, norm2_g, w_ffn_in, w_ffn_out, final_g):
    batch, seq, d = x.shape
    depth = ada_w.shape[0]
    rank, key_w = gla_gate_w2.shape[1:]
    val_w = w_up_gla.shape[1]
    moba_w = w_up_moba.shape[1]
    n = batch * seq
    lr0 = 2 * key_w + 2 * val_w
    moba0 = lr0
    gate0 = moba0 + 3 * moba_w

    mod = _adaln(c, ada_w, ada_b).reshape(depth, batch, 6, 1, d)
    x2 = x.reshape(n, d)
    for l in range(depth):
        w_main = jnp.concatenate([w_in[l, :, :lr0], w_in[l, :, lr0 + rank:]], axis=1).astype(BF16)
        w_lr = jnp.pad(w_in[l, :, lr0:lr0 + rank], ((0, 0), (0, LANES - rank))).astype(BF16)
        w2p = jnp.pad(gla_gate_w2[l], ((0, LANES - rank), (0, 0))).astype(BF16)
        proj, lr = _in_proj(x2, norm1_g[l][None], mod[l], w_main, w_lr, seq=seq)
        o_gla = _gla(proj, lr, w2p, gla_gate_b[l][None], gla_norm_g[l][None],
                     batch=batch, seq=seq, key_w=key_w, val_w=val_w)
        o_moba = _moba(proj, batch=batch, seq=seq, col0=moba0, width=moba_w)
        x2 = _merge(o_gla, o_moba, w_up_gla[l].astype(BF16), w_up_moba[l].astype(BF16), proj, gate0,
                    w_out[l].astype(BF16), x2, mod[l], seq=seq)
        x2 = _ffn(x2, norm2_g[l][None], mod[l], w_ffn_in[l].astype(BF16), w_ffn_out[l].astype(BF16),
                  final_g[None], seq=seq, final_norm=(l == depth - 1))
    return x2.reshape(batch, seq, d)
```

```python
import functools

import jax
import jax.numpy as jnp
from jax import lax
from jax.experimental import pallas as pl
from jax.experimental.pallas import tpu as pltpu

F32 = jnp.float32
BF16 = jnp.bfloat16

GLA_HEADS = 4
GLA_GATE_NORM = 16.0
GLA_CHUNK = 64
MOBA_HEAD_DIM = 128
MOBA_BLOCK = 256
MOBA_TOPK = 3
EPS = 1e-6
LOG2E = 1.4426950408889634

V7X_VMEM_BYTES = 64 * 1024 * 1024
VMEM_LIMIT_BYTES = V7X_VMEM_BYTES - 8 * 1024 * 1024
LANES = 128
BF16_SUBLANES = 16

_NT = (((1,), (1,)), ((), ()))


def _params(semantics):
    return pltpu.CompilerParams(dimension_semantics=semantics, vmem_limit_bytes=VMEM_LIMIT_BYTES)


def _mod_norm(x, g, sc, sh):
    ms = jnp.mean(x * x, axis=-1, keepdims=True)
    y = x * lax.rsqrt(ms + EPS) * g
    return y * (1.0 + sc) + sh


def _adaln_kernel(c_ref, w_ref, b_ref, o_ref):
    c_act = jax.nn.silu(c_ref[...])
    o_ref[...] = (
        jnp.dot(c_act.astype(BF16), w_ref[...].astype(BF16), preferred_element_type=F32) + b_ref[...]
    )


def _adaln(c, ada_w, ada_b, *, tn=1024):
    depth, d, n6 = ada_w.shape
    b = c.shape[0]
    return pl.pallas_call(
        _adaln_kernel,
        grid=(depth, n6 // tn),
        in_specs=[
            pl.BlockSpec((b, d), lambda l, j: (0, 0)),
            pl.BlockSpec((None, d, tn), lambda l, j: (l, 0, j)),
            pl.BlockSpec((None, 1, tn), lambda l, j: (l, 0, j)),
        ],
        out_specs=pl.BlockSpec((None, b, tn), lambda l, j: (l, 0, j)),
        out_shape=jax.ShapeDtypeStruct((depth, b, n6), F32),
        compiler_params=_params(("arbitrary", "arbitrary")),
        name="adaln",
    )(c, ada_w, ada_b.reshape(depth, 1, n6))


def _in_proj_kernel(x_ref, g_ref, sc_ref, sh_ref, w_ref, wl_ref, o_ref, ol_ref, h_ref):
    @pl.when(pl.program_id(1) == 0)
    def _():
        h = _mod_norm(x_ref[...], g_ref[...], sc_ref[...], sh_ref[...]).astype(BF16)
        h_ref[...] = h
        ol_ref[...] = jnp.dot(h, wl_ref[...], preferred_element_type=F32).astype(ol_ref.dtype)

    o_ref[...] = jnp.dot(h_ref[...], w_ref[...], preferred_element_type=F32).astype(o_ref.dtype)


def _in_proj(x2, g, mod, w_main, w_lr, *, seq, tm=1024, tn=1024):
    n, d = x2.shape
    c = w_main.shape[1]
    per_b = seq // tm
    return pl.pallas_call(
        _in_proj_kernel,
        grid=(n // tm, c // tn),
        in_specs=[
            pl.BlockSpec((tm, d), lambda i, j: (i, 0)),
            pl.BlockSpec((1, d), lambda i, j: (0, 0)),
            pl.BlockSpec((None, None, 1, d), lambda i, j: (i // per_b, 1, 0, 0)),
            pl.BlockSpec((None, None, 1, d), lambda i, j: (i // per_b, 0, 0, 0)),
            pl.BlockSpec((d, tn), lambda i, j: (0, j)),
            pl.BlockSpec((d, LANES), lambda i, j: (0, 0)),
        ],
        out_specs=[
            pl.BlockSpec((tm, tn), lambda i, j: (i, j)),
            pl.BlockSpec((tm, LANES), lambda i, j: (i, 0)),
        ],
        out_shape=[
            jax.ShapeDtypeStruct((n, c), BF16),
            jax.ShapeDtypeStruct((n, LANES), BF16),
        ],
        scratch_shapes=[pltpu.VMEM((tm, d), BF16)],
        compiler_params=_params(("arbitrary", "arbitrary")),
        name="in_proj",
    )(x2, g, mod, mod, w_main, w_lr)


def _split3(x):
    hi = x.astype(BF16)
    r1 = x - hi.astype(F32)
    mid = r1.astype(BF16)
    lo = (r1 - mid.astype(F32)).astype(BF16)
    return hi, mid, lo


def _gla_kernel(q_ref, k_ref, v_ref, gr_ref, lr_ref, w2_ref, gb_ref, ng_ref, o_ref, g_ref, st_ref,
                *, heads, dk, dv, gate_rows):
    seq = q_ref.shape[0]
    cs = GLA_CHUNK
    kw = heads * dk

    def gate(pi, carry):
        r = pl.ds(pl.multiple_of(pi * gate_rows, gate_rows), gate_rows)
        xg = jnp.dot(lr_ref[r, :], w2_ref[...], preferred_element_type=F32) + gb_ref[...]
        g_ref[r, :] = jax.nn.log_sigmoid(xg) / GLA_GATE_NORM
        return carry

    lax.fori_loop(0, seq // gate_rows, gate, 0)
    st_ref[...] = jnp.zeros_like(st_ref)
    row = lax.broadcasted_iota(jnp.int32, (cs, cs), 0)
    col = lax.broadcasted_iota(jnp.int32, (cs, cs), 1)
    causal = col <= row
    tril = causal.astype(BF16)
    scale = dk ** -0.5

    def chunk(ci, carry):
        r = pl.ds(pl.multiple_of(ci * cs, cs), cs)
        parts = jnp.concatenate(_split3(g_ref[r, :]), axis=1)
        bs = jnp.dot(tril, parts, preferred_element_type=F32)
        b = bs[:, :kw] + bs[:, kw:2 * kw] + bs[:, 2 * kw:]
        b_last = b[cs - 1:cs, :]
        q = q_ref[r, :].astype(F32) * scale
        k = k_ref[r, :].astype(F32)
        qd = (q * jnp.exp(b)).astype(BF16)
        kd = (k * jnp.exp(-b)).astype(BF16)
        ke = (k * jnp.exp(b_last - b)).astype(BF16)
        e_last = jnp.exp(b_last)
        for h in range(heads):
            ks = slice(h * dk, (h + 1) * dk)
            vs = slice(h * dv, (h + 1) * dv)
            attn = lax.dot_general(qd[:, ks], kd[:, ks], _NT, preferred_element_type=F32)
            attn = jnp.where(causal, attn, 0.0).astype(BF16)
            v = v_ref[r, vs]
            st = st_ref[h]
            o = jnp.dot(attn, v, preferred_element_type=F32)
            o = o + lax.dot_general(qd[:, ks], st.astype(BF16), _NT, preferred_element_type=F32)
            vt = v.astype(F32).T.astype(BF16)
            st_ref[h] = st * e_last[:, ks] + jnp.dot(vt, ke[:, ks], preferred_element_type=F32)
            ms = jnp.mean(o * o, axis=-1, keepdims=True)
            y = o * lax.rsqrt(ms + EPS) * ng_ref[...]
            o_ref[r, vs] = (y * jax.nn.silu(gr_ref[r, vs].astype(F32))).astype(o_ref.dtype)
        return carry

    lax.fori_loop(0, seq // cs, chunk, 0, unroll=2)


def _gla(proj, lr, w2p, gate_b, norm_g, *, batch, seq, key_w, val_w, gate_rows=512):
    n = proj.shape[0]
    dk = key_w // GLA_HEADS
    dv = val_w // GLA_HEADS
    vb = 2 * key_w // val_w
    rb = vb + 1
    kern = functools.partial(_gla_kernel, heads=GLA_HEADS, dk=dk, dv=dv, gate_rows=gate_rows)
    return pl.pallas_call(
        kern,
        grid=(batch,),
        in_specs=[
            pl.BlockSpec((seq, key_w), lambda b: (b, 0)),
            pl.BlockSpec((seq, key_w), lambda b: (b, 1)),
            pl.BlockSpec((seq, val_w), lambda b: (b, vb)),
            pl.BlockSpec((seq, val_w), lambda b: (b, rb)),
            pl.BlockSpec((seq, LANES), lambda b: (b, 0)),
            pl.BlockSpec((LANES, key_w), lambda b: (0, 0)),
            pl.BlockSpec((1, key_w), lambda b: (0, 0)),
            pl.BlockSpec((1, dv), lambda b: (0, 0)),
        ],
        out_specs=pl.BlockSpec((seq, val_w), lambda b: (b, 0)),
        out_shape=jax.ShapeDtypeStruct((n, val_w), BF16),
        scratch_shapes=[pltpu.VMEM((seq, key_w), F32), pltpu.VMEM((GLA_HEADS, dv, dk), F32)],
        compiler_params=_params(("arbitrary",)),
        name="gla",
    )(proj, proj, proj, proj, lr, w2p, gate_b, norm_g)


def _moba_kernel(q_ref, k_ref, v_ref, o_ref, vt_ref, s_ref, pt_ref):
    bs = MOBA_BLOCK
    hd = MOBA_HEAD_DIM
    seq = k_ref.shape[0]
    nb = seq // bs
    scale = hd ** -0.5

    km = jnp.mean(k_ref[...].astype(F32).reshape(nb, bs, hd), axis=1)
    km = jnp.concatenate([km, jnp.zeros((BF16_SUBLANES - nb, hd), F32)], axis=0).astype(BF16)
    sc = lax.dot_general(km, q_ref[...], _NT, preferred_element_type=F32)[0:nb, :]
    blk = lax.broadcasted_iota(jnp.int32, sc.shape, 0)
    qpos = lax.broadcasted_iota(jnp.int32, sc.shape, 1)
    past = (blk + 1) * bs <= qpos
    sc = jnp.where(past, sc, -jnp.inf)
    beaten = jnp.zeros(sc.shape, F32)
    for jp in range(nb):
        c = sc[jp:jp + 1, :]
        ahead = (c > sc) | ((c == sc) & (blk > jp))
        beaten = beaten + ahead.astype(F32)
    sel_bias = jnp.where(past & (beaten < MOBA_TOPK), 0.0, -jnp.inf)

    vt_ref[...] = v_ref[...].astype(F32).T.astype(BF16)
    krow = lax.broadcasted_iota(jnp.int32, (bs, bs), 0)
    qcol = lax.broadcasted_iota(jnp.int32, (bs, bs), 1)
    own_bias = jnp.where(krow <= qcol, 0.0, -jnp.inf)

    def scores(i):
        nk = (i + 1) * bs
        s_ref[i % 2, 0:nk, :] = lax.dot_general(
            k_ref[0:nk, :], q_ref[i * bs:nk, :], _NT, preferred_element_type=F32)

    scores(0)
    for i in range(nb):
        if i + 1 < nb:
            scores(i + 1)
        slot = i % 2
        qs = slice(i * bs, (i + 1) * bs)
        nk = (i + 1) * bs
        biases = [sel_bias[j:j + 1, qs] for j in range(i)]
        own = s_ref[slot, i * bs:nk, :] + own_bias
        m = own.max(axis=0, keepdims=True)
        for j in range(i):
            m = jnp.maximum(m, s_ref[slot, j * bs:(j + 1) * bs, :].max(axis=0, keepdims=True) + biases[j])
        p = jnp.exp2((own - m) * (scale * LOG2E))
        l = p.sum(axis=0, keepdims=True)
        pt_ref[slot, i * bs:nk, :] = p.astype(BF16)
        for j in range(i):
            p = jnp.exp2((s_ref[slot, j * bs:(j + 1) * bs, :] - (m - biases[j])) * (scale * LOG2E))
            l = l + p.sum(axis=0, keepdims=True)
            pt_ref[slot, j * bs:(j + 1) * bs, :] = p.astype(BF16)
        ot = jnp.dot(vt_ref[:, 0:nk], pt_ref[slot, 0:nk, :], preferred_element_type=F32)
        o_ref[qs, :] = (ot / l).T.astype(o_ref.dtype)


def _moba(proj, *, batch, seq, col0, width):
    n = proj.shape[0]
    hd = MOBA_HEAD_DIM
    heads = width // hd
    qb = col0 // hd
    kb = qb + heads
    vb = kb + heads
    return pl.pallas_call(
        _moba_kernel,
        grid=(batch, heads),
        in_specs=[
            pl.BlockSpec((seq, hd), lambda b, h: (b, qb + h)),
            pl.BlockSpec((seq, hd), lambda b, h: (b, kb + h)),
            pl.BlockSpec((seq, hd), lambda b, h: (b, vb + h)),
        ],
        out_specs=pl.BlockSpec((seq, hd), lambda b, h: (b, h)),
        out_shape=jax.ShapeDtypeStruct((n, width), BF16),
        scratch_shapes=[
            pltpu.VMEM((hd, seq), BF16),
            pltpu.VMEM((2, seq, MOBA_BLOCK), F32),
            pltpu.VMEM((2, seq, MOBA_BLOCK), BF16),
        ],
        compiler_params=_params(("arbitrary", "arbitrary")),
        name="moba",
    )(proj, proj, proj)


def _merge_kernel(og_ref, om_ref, wg_ref, wm_ref, g1_ref, g2_ref, wo_ref, x_ref, gt_ref, o_ref):
    yg = jnp.dot(og_ref[...], wg_ref[...], preferred_element_type=F32)
    ym = jnp.dot(om_ref[...], wm_ref[...], preferred_element_type=F32)
    z = jax.nn.sigmoid(g1_ref[...].astype(F32)) * yg + jax.nn.sigmoid(g2_ref[...].astype(F32)) * ym
    y = jnp.dot(z.astype(BF16), wo_ref[...], preferred_element_type=F32)
    o_ref[...] = x_ref[...] + gt_ref[...] * y


def _merge(o_gla, o_moba, w_up_gla, w_up_moba, proj, gate_col0, w_out, x2, mod, *, seq, tm=512):
    n, d = x2.shape
    per_b = seq // tm
    g1b = gate_col0 // d
    kg = o_gla.shape[1]
    km = o_moba.shape[1]
    resident = dict(pipeline_mode=pl.Buffered(1))
    return pl.pallas_call(
        _merge_kernel,
        grid=(n // tm,),
        in_specs=[
            pl.BlockSpec((tm, kg), lambda i: (i, 0)),
            pl.BlockSpec((tm, km), lambda i: (i, 0)),
            pl.BlockSpec((kg, d), lambda i: (0, 0), **resident),
            pl.BlockSpec((km, d), lambda i: (0, 0), **resident),
            pl.BlockSpec((tm, d), lambda i: (i, g1b)),
            pl.BlockSpec((tm, d), lambda i: (i, g1b + 1)),
            pl.BlockSpec((d, d), lambda i: (0, 0), **resident),
            pl.BlockSpec((tm, d), lambda i: (i, 0)),
            pl.BlockSpec((None, None, 1, d), lambda i: (i // per_b, 2, 0, 0)),
        ],
        out_specs=pl.BlockSpec((tm, d), lambda i: (i, 0)),
        out_shape=jax.ShapeDtypeStruct((n, d), F32),
        compiler_params=_params(("arbitrary",)),
        name="merge",
    )(o_gla, o_moba, w_up_gla, w_up_moba, proj, proj, w_out, x2, mod)


def _ffn_kernel(x_ref, g_ref, sc_ref, sh_ref, gt_ref, wa_ref, wu_ref, wo_ref, fg_ref, o_ref, h_ref, *, final_norm):
    j = pl.program_id(1)

    @pl.when(j == 0)
    def _():
        h_ref[...] = _mod_norm(x_ref[...], g_ref[...], sc_ref[...], sh_ref[...]).astype(BF16)
        o_ref[...] = jnp.zeros_like(o_ref)

    h = h_ref[...]
    a = jnp.dot(h, wa_ref[...], preferred_element_type=F32)
    u = jnp.dot(h, wu_ref[...], preferred_element_type=F32)
    o_ref[...] += jnp.dot((jax.nn.silu(a) * u).astype(BF16), wo_ref[...], preferred_element_type=F32)

    @pl.when(j == pl.num_programs(1) - 1)
    def _():
        y = x_ref[...] + gt_ref[...] * o_ref[...]
        if final_norm:
            ms = jnp.mean(y * y, axis=-1, keepdims=True)
            y = y * lax.rsqrt(ms + EPS) * fg_ref[...]
        o_ref[...] = y


def _ffn(x2, g, mod, w_in, w_out, final_g, *, seq, final_norm, tm=512, tf=512):
    n, d = x2.shape
    dff = w_out.shape[0]
    per_b = seq // tm
    nf = dff // tf
    kern = functools.partial(_ffn_kernel, final_norm=final_norm)
    return pl.pallas_call(
        kern,
        grid=(n // tm, nf),
        in_specs=[
            pl.BlockSpec((tm, d), lambda i, j: (i, 0)),
            pl.BlockSpec((1, d), lambda i, j: (0, 0)),
            pl.BlockSpec((None, None, 1, d), lambda i, j: (i // per_b, 4, 0, 0)),
            pl.BlockSpec((None, None, 1, d), lambda i, j: (i // per_b, 3, 0, 0)),
            pl.BlockSpec((None, None, 1, d), lambda i, j: (i // per_b, 5, 0, 0)),
            pl.BlockSpec((d, tf), lambda i, j: (0, j)),
            pl.BlockSpec((d, tf), lambda i, j: (0, nf + j)),
            pl.BlockSpec((tf, d), lambda i, j: (j, 0)),
            pl.BlockSpec((1, d), lambda i, j: (0, 0)),
        ],
        out_specs=pl.BlockSpec((tm, d), lambda i, j: (i, 0)),
        out_shape=jax.ShapeDtypeStruct((n, d), F32),
        scratch_shapes=[pltpu.VMEM((tm, d), BF16)],
        compiler_params=_params(("arbitrary", "arbitrary")),
        name="ffn",
    )(x2, g, mod, mod, mod, w_in, w_in, w_out, final_g)


def kernel(x, c, ada_w, ada_b, norm1_g, w_in, gla_gate_w2, gla_gate_b, gla_norm_g, w_up_gla, w_up_moba,
           w_out, norm2_g, w_ffn_in, w_ffn_out, final_g):
    batch, seq, d = x.shape
    depth = ada_w.shape[0]
    rank, key_w = gla_gate_w2.shape[1:]
    val_w = w_up_gla.shape[1]
    moba_w = w_up_moba.shape[1]
    n = batch * seq
    lr0 = 2 * key_w + 2 * val_w
    moba0 = lr0
    gate0 = moba0 + 3 * moba_w

    mod = _adaln(c, ada_w, ada_b).reshape(depth, batch, 6, 1, d)
    x2 = x.reshape(n, d)
    for l in range(depth):
        w_main = jnp.concatenate([w_in[l, :, :lr0], w_in[l, :, lr0 + rank:]], axis=1).astype(BF16)
        w_lr = jnp.pad(w_in[l, :, lr0:lr0 + rank], ((0, 0), (0, LANES - rank))).astype(BF16)
        w2p = jnp.pad(gla_gate_w2[l], ((0, LANES - rank), (0, 0))).astype(BF16)
        proj, lr = _in_proj(x2, norm1_g[l][None], mod[l], w_main, w_lr, seq=seq)
        o_gla = _gla(proj, lr, w2p, gla_gate_b[l][None], gla_norm_g[l][None],
                     batch=batch, seq=seq, key_w=key_w, val_w=val_w)
        o_moba = _moba(proj, batch=batch, seq=seq, col0=moba0, width=moba_w)
        x2 = _merge(o_gla, o_moba, w_up_gla[l].astype(BF16), w_up_moba[l].astype(BF16), proj, gate0,
                    w_out[l].astype(BF16), x2, mod[l], seq=seq)
        x2 = _ffn(x2, norm2_g[l][None], mod[l], w_ffn_in[l].astype(BF16), w_ffn_out[l].astype(BF16),
                  final_g[None], seq=seq, final_norm=(l == depth - 1))
    return x2.reshape(batch, seq, d)
```

```python
import functools

import jax
import jax.numpy as jnp
from jax import lax
from jax.experimental import pallas as pl
from jax.experimental.pallas import tpu as pltpu

F32 = jnp.float32
BF16 = jnp.bfloat16

GLA_HEADS = 4
GLA_GATE_NORM = 16.0
GLA_CHUNK = 64
MOBA_HEAD_DIM = 128
MOBA_BLOCK = 256
MOBA_TOPK = 3
EPS = 1e-6
LOG2E = 1.4426950408889634

V7X_VMEM_BYTES = 64 * 1024 * 1024
VMEM_LIMIT_BYTES = V7X_VMEM_BYTES - 8 * 1024 * 1024
LANES = 128
BF16_SUBLANES = 16

_NT = (((1,), (1,)), ((), ()))


def _params(semantics):
    return pltpu.CompilerParams(dimension_semantics=semantics, vmem_limit_bytes=VMEM_LIMIT_BYTES)


def _mod_norm(x, g, sc, sh):
    ms = jnp.mean(x * x, axis=-1, keepdims=True)
    y = x * lax.rsqrt(ms + EPS) * g
    return y * (1.0 + sc) + sh


def _adaln_kernel(c_ref, w_ref, b_ref, o_ref):
    c_act = jax.nn.silu(c_ref[...])
    o_ref[...] = (
        jnp.dot(c_act.astype(BF16), w_ref[...].astype(BF16), preferred_element_type=F32) + b_ref[...]
    )


def _adaln(c, ada_w, ada_b, *, tn=1024):
    depth, d, n6 = ada_w.shape
    b = c.shape[0]
    return pl.pallas_call(
        _adaln_kernel,
        grid=(depth, n6 // tn),
        in_specs=[
            pl.BlockSpec((b, d), lambda l, j: (0, 0)),
            pl.BlockSpec((None, d, tn), lambda l, j: (l, 0, j)),
            pl.BlockSpec((None, 1, tn), lambda l, j: (l, 0, j)),
        ],
        out_specs=pl.BlockSpec((None, b, tn), lambda l, j: (l, 0, j)),
        out_shape=jax.ShapeDtypeStruct((depth, b, n6), F32),
        compiler_params=_params(("arbitrary", "arbitrary")),
        name="adaln",
    )(c, ada_w, ada_b.reshape(depth, 1, n6))


def _repack_kernel(a_ref, b_ref, o_ref, ol_ref, *, lr_tile, rank):
    t = pl.program_id(1)

    @pl.when(t < lr_tile)
    def _():
        o_ref[...] = a_ref[...].astype(BF16)

    @pl.when(t >= lr_tile)
    def _():
        o_ref[...] = jnp.concatenate([a_ref[:, rank:], b_ref[:, :rank]], axis=1).astype(BF16)

    @pl.when(t == lr_tile)
    def _():
        pad = jnp.zeros((a_ref.shape[0], LANES - rank), F32)
        ol_ref[...] = jnp.concatenate([a_ref[:, :rank], pad], axis=1).astype(BF16)


def _repack_w_in(w_in, lr0, rank, *, tw=1024):
    depth, d, c = w_in.shape
    assert lr0 % tw == 0 and (c - rank) % tw == 0 and rank < LANES
    kern = functools.partial(_repack_kernel, lr_tile=lr0 // tw, rank=rank)
    return pl.pallas_call(
        kern,
        grid=(depth, (c - rank) // tw),
        in_specs=[
            pl.BlockSpec((None, d, tw), lambda l, t: (l, 0, t)),
            pl.BlockSpec((None, d, LANES), lambda l, t: (l, 0, (t + 1) * (tw // LANES))),
        ],
        out_specs=[
            pl.BlockSpec((None, d, tw), lambda l, t: (l, 0, t)),
            pl.BlockSpec((None, d, LANES), lambda l, t: (l, 0, 0)),
        ],
        out_shape=[
            jax.ShapeDtypeStruct((depth, d, c - rank), BF16),
            jax.ShapeDtypeStruct((depth, d, LANES), BF16),
        ],
        compiler_params=_params(("arbitrary", "arbitrary")),
        name="repack_w_in",
    )(w_in, w_in)


def _in_proj_kernel(x_ref, g_ref, sc_ref, sh_ref, w_ref, wl_ref, o_ref, ol_ref, h_ref):
    @pl.when(pl.program_id(1) == 0)
    def _():
        h = _mod_norm(x_ref[...], g_ref[...], sc_ref[...], sh_ref[...]).astype(BF16)
        h_ref[...] = h
        ol_ref[...] = jnp.dot(h, wl_ref[...], preferred_element_type=F32).astype(ol_ref.dtype)

    o_ref[...] = jnp.dot(h_ref[...], w_ref[...], preferred_element_type=F32).astype(o_ref.dtype)


def _in_proj(x2, g, mod, w_main, w_lr, layer, *, seq, tm=1024, tn=2048):
    n, d = x2.shape
    c = w_main.shape[2]
    per_b = seq // tm
    return pl.pallas_call(
        _in_proj_kernel,
        grid=(n // tm, c // tn),
        in_specs=[
            pl.BlockSpec((tm, d), lambda i, j: (i, 0)),
            pl.BlockSpec((1, d), lambda i, j: (0, 0)),
            pl.BlockSpec((None, None, 1, d), lambda i, j: (i // per_b, 1, 0, 0)),
            pl.BlockSpec((None, None, 1, d), lambda i, j: (i // per_b, 0, 0, 0)),
            pl.BlockSpec((None, d, tn), lambda i, j: (layer, 0, j)),
            pl.BlockSpec((None, d, LANES), lambda i, j: (layer, 0, 0)),
        ],
        out_specs=[
            pl.BlockSpec((tm, tn), lambda i, j: (i, j)),
            pl.BlockSpec((tm, LANES), lambda i, j: (i, 0)),
        ],
        out_shape=[
            jax.ShapeDtypeStruct((n, c), BF16),
            jax.ShapeDtypeStruct((n, LANES), BF16),
        ],
        scratch_shapes=[pltpu.VMEM((tm, d), BF16)],
        compiler_params=_params(("arbitrary", "arbitrary")),
        name="in_proj",
    )(x2, g, mod, mod, w_main, w_lr)


def _split3(x):
    hi = x.astype(BF16)
    r1 = x - hi.astype(F32)
    mid = r1.astype(BF16)
    lo = (r1 - mid.astype(F32)).astype(BF16)
    return hi, mid, lo


def _gla_kernel(q_ref, k_ref, v_ref, gr_ref, lr_ref, w2_ref, gb_ref, ng_ref, o_ref,
                qm_ref, km_ref, qe_ref, ke_ref, el_ref, st_ref, *, heads, dk, dv, gate_rows):
    seq = q_ref.shape[0]
    cs = GLA_CHUNK
    kw = heads * dk
    per_trip = gate_rows // cs
    row = lax.broadcasted_iota(jnp.int32, (cs, cs), 0)
    col = lax.broadcasted_iota(jnp.int32, (cs, cs), 1)
    causal = col <= row
    tril = causal.astype(BF16)
    scale = dk ** -0.5

    def decay(pi, carry):
        base = pl.multiple_of(pi * gate_rows, gate_rows)
        r = pl.ds(base, gate_rows)
        xg = jnp.dot(lr_ref[r, :], w2_ref[...], preferred_element_type=F32) + gb_ref[...]
        g = (jnp.minimum(xg, 0.0) - jnp.log1p(jnp.exp(-jnp.abs(xg)))) / GLA_GATE_NORM
        parts = jnp.concatenate(_split3(g), axis=1)
        q = q_ref[r, :].astype(F32) * scale
        k = k_ref[r, :].astype(F32)
        for c in range(per_trip):
            rows = slice(c * cs, (c + 1) * cs)
            out = pl.ds(base + c * cs, cs)
            bs = jnp.dot(tril, parts[rows, :], preferred_element_type=F32)
            b = bs[:, :kw] + bs[:, kw:2 * kw] + bs[:, 2 * kw:]
            b_last = b[cs - 1:cs, :]
            b_mid = b[cs // 2 - 1:cs // 2, :]
            qm_ref[out, :] = (q[rows, :] * jnp.exp(b - b_mid)).astype(BF16)
            km_ref[out, :] = (k[rows, :] * jnp.exp(b_mid - b)).astype(BF16)
            qe_ref[out, :] = (q[rows, :] * jnp.exp(b)).astype(BF16)
            ke_ref[out, :] = (k[rows, :] * jnp.exp(b_last - b)).astype(BF16)
            el_ref[pl.ds(pi * per_trip + c, 1), :] = jnp.exp(b_last)
        return carry

    lax.fori_loop(0, seq // gate_rows, decay, 0)
    st_ref[...] = jnp.zeros_like(st_ref)

    def chunk(ci, carry):
        r = pl.ds(pl.multiple_of(ci * cs, cs), cs)
        e_last = el_ref[pl.ds(ci, 1), :]
        hs = range(heads)
        ks = [slice(h * dk, (h + 1) * dk) for h in hs]
        vs = [slice(h * dv, (h + 1) * dv) for h in hs]
        attn = [lax.dot_general(qm_ref[r, ks[h]], km_ref[r, ks[h]], _NT, preferred_element_type=F32) for h in hs]
        v = [v_ref[r, vs[h]] for h in hs]
        st = [st_ref[h] for h in hs]
        o_inter = [lax.dot_general(qe_ref[r, ks[h]], st[h].astype(BF16), _NT, preferred_element_type=F32)
                   for h in hs]
        kv = [jnp.dot(v[h].astype(F32).T.astype(BF16), ke_ref[r, ks[h]], preferred_element_type=F32) for h in hs]
        attn = [jnp.where(causal, attn[h], 0.0).astype(BF16) for h in hs]
        o = [jnp.dot(attn[h], v[h], preferred_element_type=F32) + o_inter[h] for h in hs]
        for h in hs:
            st_ref[h] = st[h] * e_last[:, ks[h]] + kv[h]
        for h in hs:
            ms = jnp.mean(o[h] * o[h], axis=-1, keepdims=True)
            y = o[h] * lax.rsqrt(ms + EPS) * ng_ref[...]
            o_ref[r, vs[h]] = (y * jax.nn.silu(gr_ref[r, vs[h]].astype(F32))).astype(o_ref.dtype)
        return carry

    lax.fori_loop(0, seq // cs, chunk, 0, unroll=2)


def _gla(proj, lr, w2p, gate_b, norm_g, *, batch, seq, key_w, val_w, gate_rows=512):
    n = proj.shape[0]
    dk = key_w // GLA_HEADS
    dv = val_w // GLA_HEADS
    vb = 2 * key_w // val_w
    rb = vb + 1
    kern = functools.partial(_gla_kernel, heads=GLA_HEADS, dk=dk, dv=dv, gate_rows=gate_rows)
    return pl.pallas_call(
        kern,
        grid=(batch,),
        in_specs=[
            pl.BlockSpec((seq, key_w), lambda b: (b, 0)),
            pl.BlockSpec((seq, key_w), lambda b: (b, 1)),
            pl.BlockSpec((seq, val_w), lambda b: (b, vb)),
            pl.BlockSpec((seq, val_w), lambda b: (b, rb)),
            pl.BlockSpec((seq, LANES), lambda b: (b, 0)),
            pl.BlockSpec((LANES, key_w), lambda b: (0, 0)),
            pl.BlockSpec((1, key_w), lambda b: (0, 0)),
            pl.BlockSpec((1, dv), lambda b: (0, 0)),
        ],
        out_specs=pl.BlockSpec((seq, val_w), lambda b: (b, 0)),
        out_shape=jax.ShapeDtypeStruct((n, val_w), BF16),
        scratch_shapes=[
            pltpu.VMEM((seq, key_w), BF16),
            pltpu.VMEM((seq, key_w), BF16),
            pltpu.VMEM((seq, key_w), BF16),
            pltpu.VMEM((seq, key_w), BF16),
            pltpu.VMEM((seq // GLA_CHUNK, key_w), F32),
            pltpu.VMEM((GLA_HEADS, dv, dk), F32),
        ],
        compiler_params=_params(("arbitrary",)),
        name="gla",
    )(proj, proj, proj, proj, lr, w2p, gate_b, norm_g)


def _moba_kernel(q_ref, k_ref, v_ref, o_ref, vt_ref, s_ref, pt_ref):
    bs = MOBA_BLOCK
    hd = MOBA_HEAD_DIM
    seq = k_ref.shape[0]
    nb = seq // bs
    scale = hd ** -0.5

    km = jnp.mean(k_ref[...].astype(F32).reshape(nb, bs, hd), axis=1)
    km = jnp.concatenate([km, jnp.zeros((BF16_SUBLANES - nb, hd), F32)], axis=0).astype(BF16)
    sc = lax.dot_general(km, q_ref[...], _NT, preferred_element_type=F32)[0:nb, :]
    blk = lax.broadcasted_iota(jnp.int32, sc.shape, 0)
    qpos = lax.broadcasted_iota(jnp.int32, sc.shape, 1)
    past = (blk + 1) * bs <= qpos
    sc = jnp.where(past, sc, -jnp.inf)
    beaten = jnp.zeros(sc.shape, F32)
    for jp in range(nb):
        c = sc[jp:jp + 1, :]
        ahead = (c > sc) | ((c == sc) & (blk > jp))
        beaten = beaten + ahead.astype(F32)
    sel_bias = jnp.where(past & (beaten < MOBA_TOPK), 0.0, -jnp.inf)

    vt_ref[0:hd, :] = v_ref[...].astype(F32).T.astype(BF16)
    ones_row = lax.broadcasted_iota(jnp.int32, (BF16_SUBLANES, seq), 0) == 0
    vt_ref[hd:hd + BF16_SUBLANES, :] = ones_row.astype(F32).astype(BF16)
    krow = lax.broadcasted_iota(jnp.int32, (bs, bs), 0)
    qcol = lax.broadcasted_iota(jnp.int32, (bs, bs), 1)
    own_bias = jnp.where(krow <= qcol, 0.0, -jnp.inf)

    def scores(i):
        nk = (i + 1) * bs
        s_ref[i % 2, 0:nk, :] = lax.dot_general(
            k_ref[0:nk, :], q_ref[i * bs:nk, :], _NT, preferred_element_type=F32)

    def softmax(i):
        slot = i % 2
        qs = slice(i * bs, (i + 1) * bs)
        nk = (i + 1) * bs
        biases = [sel_bias[j:j + 1, qs] for j in range(i)]
        m = (s_ref[slot, i * bs:nk, :] + own_bias).max(axis=0, keepdims=True)
        for j in range(i):
            m = jnp.maximum(m, s_ref[slot, j * bs:(j + 1) * bs, :].max(axis=0, keepdims=True) + biases[j])
        own = s_ref[slot, i * bs:nk, :] + own_bias
        pt_ref[slot, i * bs:nk, :] = jnp.exp2((own - m) * (scale * LOG2E)).astype(BF16)
        for j in range(i):
            p = jnp.exp2((s_ref[slot, j * bs:(j + 1) * bs, :] - (m - biases[j])) * (scale * LOG2E))
            pt_ref[slot, j * bs:(j + 1) * bs, :] = p.astype(BF16)

    def attend(i):
        nk = (i + 1) * bs
        ot = jnp.dot(vt_ref[:, 0:nk], pt_ref[i % 2, 0:nk, :], preferred_element_type=F32)
        o_ref[i * bs:nk, :] = (ot[0:hd, :] / ot[hd:hd + 1, :]).T.astype(o_ref.dtype)

    scores(0)
    for i in range(nb):
        if i + 1 < nb:
            scores(i + 1)
        if i > 0:
            attend(i - 1)
        softmax(i)
    attend(nb - 1)


def _moba(proj, *, batch, seq, col0, width):
    n = proj.shape[0]
    hd = MOBA_HEAD_DIM
    heads = width // hd
    qb = col0 // hd
    kb = qb + heads
    vb = kb + heads
    return pl.pallas_call(
        _moba_kernel,
        grid=(batch, heads),
        in_specs=[
            pl.BlockSpec((seq, hd), lambda b, h: (b, qb + h)),
            pl.BlockSpec((seq, hd), lambda b, h: (b, kb + h)),
            pl.BlockSpec((seq, hd), lambda b, h: (b, vb + h)),
        ],
        out_specs=pl.BlockSpec((seq, hd), lambda b, h: (b, h)),
        out_shape=jax.ShapeDtypeStruct((n, width), BF16),
        scratch_shapes=[
            pltpu.VMEM((hd + BF16_SUBLANES, seq), BF16),
            pltpu.VMEM((2, seq, MOBA_BLOCK), F32),
            pltpu.VMEM((2, seq, MOBA_BLOCK), BF16),
        ],
        compiler_params=_params(("arbitrary", "arbitrary")),
        name="moba",
    )(proj, proj, proj)


def _merge_kernel(og_ref, om_ref, wg_ref, wm_ref, g1_ref, g2_ref, wo_ref, x_ref, gt_ref, o_ref):
    yg = jnp.dot(og_ref[...], wg_ref[...], preferred_element_type=F32)
    ym = jnp.dot(om_ref[...], wm_ref[...], preferred_element_type=F32)
    z = jax.nn.sigmoid(g1_ref[...].astype(F32)) * yg + jax.nn.sigmoid(g2_ref[...].astype(F32)) * ym
    y = jnp.dot(z.astype(BF16), wo_ref[...], preferred_element_type=F32)
    o_ref[...] = x_ref[...] + gt_ref[...] * y


def _merge(o_gla, o_moba, w_up_gla, w_up_moba, proj, gate_col0, w_out, x2, mod, *, seq, tm=512):
    n, d = x2.shape
    per_b = seq // tm
    g1b = gate_col0 // d
    kg = o_gla.shape[1]
    km = o_moba.shape[1]
    resident = dict(pipeline_mode=pl.Buffered(1))
    return pl.pallas_call(
        _merge_kernel,
        grid=(n // tm,),
        in_specs=[
            pl.BlockSpec((tm, kg), lambda i: (i, 0)),
            pl.BlockSpec((tm, km), lambda i: (i, 0)),
            pl.BlockSpec((kg, d), lambda i: (0, 0), **resident),
            pl.BlockSpec((km, d), lambda i: (0, 0), **resident),
            pl.BlockSpec((tm, d), lambda i: (i, g1b)),
            pl.BlockSpec((tm, d), lambda i: (i, g1b + 1)),
            pl.BlockSpec((d, d), lambda i: (0, 0), **resident),
            pl.BlockSpec((tm, d), lambda i: (i, 0)),
            pl.BlockSpec((None, None, 1, d), lambda i: (i // per_b, 2, 0, 0)),
        ],
        out_specs=pl.BlockSpec((tm, d), lambda i: (i, 0)),
        out_shape=jax.ShapeDtypeStruct((n, d), F32),
        compiler_params=_params(("arbitrary",)),
        name="merge",
    )(o_gla, o_moba, w_up_gla, w_up_moba, proj, proj, w_out, x2, mod)


def _ffn_kernel(x_ref, g_ref, sc_ref, sh_ref, gt_ref, wa_ref, wu_ref, wo_ref, fg_ref, o_ref, h_ref, *, final_norm):
    j = pl.program_id(1)

    @pl.when(j == 0)
    def _():
        h_ref[...] = _mod_norm(x_ref[...], g_ref[...], sc_ref[...], sh_ref[...]).astype(BF16)
        o_ref[...] = jnp.zeros_like(o_ref)

    h = h_ref[...]
    a = jnp.dot(h, wa_ref[...], preferred_element_type=F32)
    u = jnp.dot(h, wu_ref[...], preferred_element_type=F32)
    o_ref[...] += jnp.dot((jax.nn.silu(a) * u).astype(BF16), wo_ref[...], preferred_element_type=F32)

    @pl.when(j == pl.num_programs(1) - 1)
    def _():
        y = x_ref[...] + gt_ref[...] * o_ref[...]
        if final_norm:
            ms = jnp.mean(y * y, axis=-1, keepdims=True)
            y = y * lax.rsqrt(ms + EPS) * fg_ref[...]
        o_ref[...] = y


def _ffn(x2, g, mod, w_in, w_out, final_g, *, seq, final_norm, tm=512, tf=512):
    n, d = x2.shape
    dff = w_out.shape[0]
    per_b = seq // tm
    nf = dff // tf
    kern = functools.partial(_ffn_kernel, final_norm=final_norm)
    return pl.pallas_call(
        kern,
        grid=(n // tm, nf),
        in_specs=[
            pl.BlockSpec((tm, d), lambda i, j: (i, 0)),
            pl.BlockSpec((1, d), lambda i, j: (0, 0)),
            pl.BlockSpec((None, None, 1, d), lambda i, j: (i // per_b, 4, 0, 0)),
            pl.BlockSpec((None, None, 1, d), lambda i, j: (i // per_b, 3, 0, 0)),
            pl.BlockSpec((None, None, 1, d), lambda i, j: (i // per_b, 5, 0, 0)),
            pl.BlockSpec((d, tf), lambda i, j: (0, j)),
            pl.BlockSpec((d, tf), lambda i, j: (0, nf + j)),
            pl.BlockSpec((tf, d), lambda i, j: (j, 0)),
            pl.BlockSpec((1, d), lambda i, j: (0, 0)),
        ],
        out_specs=pl.BlockSpec((tm, d), lambda i, j: (i, 0)),
        out_shape=jax.ShapeDtypeStruct((n, d), F32),
        scratch_shapes=[pltpu.VMEM((tm, d), BF16)],
        compiler_params=_params(("arbitrary", "arbitrary")),
        name="ffn",
    )(x2, g, mod, mod, mod, w_in, w_in, w_out, final_g)


def kernel(x, c, ada_w, ada_b, norm1_g, w_in, gla_gate_w2, gla_gate_b, gla_norm_g, w_up_gla, w_up_moba,
           w_out, norm2_g, w_ffn_in, w_ffn_out, final_g):
    batch, seq, d = x.shape
    depth = ada_w.shape[0]
    rank, key_w = gla_gate_w2.shape[1:]
    val_w = w_up_gla.shape[1]
    moba_w = w_up_moba.shape[1]
    n = batch * seq
    lr0 = 2 * key_w + 2 * val_w
    moba0 = lr0
    gate0 = moba0 + 3 * moba_w

    mod = _adaln(c, ada_w, ada_b).reshape(depth, batch, 6, 1, d)
    w_main, w_lr = _repack_w_in(w_in, lr0, rank)
    x2 = x.reshape(n, d)
    for l in range(depth):
        w2p = jnp.pad(gla_gate_w2[l], ((0, LANES - rank), (0, 0))).astype(BF16)
        proj, lr = _in_proj(x2, norm1_g[l][None], mod[l], w_main, w_lr, l, seq=seq)
        o_gla = _gla(proj, lr, w2p, gla_gate_b[l][None], gla_norm_g[l][None],
                     batch=batch, seq=seq, key_w=key_w, val_w=val_w)
        o_moba = _moba(proj, batch=batch, seq=seq, col0=moba0, width=moba_w)
        x2 = _merge(o_gla, o_moba, w_up_gla[l].astype(BF16), w_up_moba[l].astype(BF16), proj, gate0,
                    w_out[l].astype(BF16), x2, mod[l], seq=seq)
        x2 = _ffn(x2, norm2_g[l][None], mod[l], w_ffn_in[l].astype(BF16), w_ffn_out[l].astype(BF16),
                  final_g[None], seq=seq, final_norm=(l == depth - 1))
    return x2.reshape(batch, seq, d)
```

```python
import functools

import jax
import jax.numpy as jnp
from jax import lax
from jax.experimental import pallas as pl
from jax.experimental.pallas import tpu as pltpu

F32 = jnp.float32
BF16 = jnp.bfloat16

GLA_HEADS = 4
GLA_GATE_NORM = 16.0
GLA_CHUNK = 64
MOBA_HEAD_DIM = 128
MOBA_BLOCK = 256
MOBA_TOPK = 3
EPS = 1e-6
LOG2E = 1.4426950408889634

V7X_VMEM_BYTES = 64 * 1024 * 1024
VMEM_LIMIT_BYTES = V7X_VMEM_BYTES - 8 * 1024 * 1024
LANES = 128
BF16_SUBLANES = 16

_NT = (((1,), (1,)), ((), ()))


def _params(semantics):
    return pltpu.CompilerParams(dimension_semantics=semantics, vmem_limit_bytes=VMEM_LIMIT_BYTES)


def _mod_norm(x, g, sc, sh):
    ms = jnp.mean(x * x, axis=-1, keepdims=True)
    y = x * lax.rsqrt(ms + EPS) * g
    return y * (1.0 + sc) + sh


def _adaln_kernel(c_ref, w_ref, b_ref, o_ref):
    c_act = jax.nn.silu(c_ref[...])
    o_ref[...] = (
        jnp.dot(c_act.astype(BF16), w_ref[...].astype(BF16), preferred_element_type=F32) + b_ref[...]
    )


def _adaln(c, ada_w, ada_b, *, tn=1024):
    depth, d, n6 = ada_w.shape
    b = c.shape[0]
    return pl.pallas_call(
        _adaln_kernel,
        grid=(depth, n6 // tn),
        in_specs=[
            pl.BlockSpec((b, d), lambda l, j: (0, 0)),
            pl.BlockSpec((None, d, tn), lambda l, j: (l, 0, j)),
            pl.BlockSpec((None, 1, tn), lambda l, j: (l, 0, j)),
        ],
        out_specs=pl.BlockSpec((None, b, tn), lambda l, j: (l, 0, j)),
        out_shape=jax.ShapeDtypeStruct((depth, b, n6), F32),
        compiler_params=_params(("arbitrary", "arbitrary")),
        name="adaln",
    )(c, ada_w, ada_b.reshape(depth, 1, n6))


def _repack_kernel(a_ref, b_ref, o_ref, ol_ref, *, lr_tile, rank):
    t = pl.program_id(1)

    @pl.when(t < lr_tile)
    def _():
        o_ref[...] = a_ref[...].astype(BF16)

    @pl.when(t >= lr_tile)
    def _():
        o_ref[...] = jnp.concatenate([a_ref[rank:, :], b_ref[...]], axis=0).astype(BF16)

    @pl.when(t == lr_tile)
    def _():
        pad = jnp.zeros((ol_ref.shape[0] - rank, ol_ref.shape[1]), F32)
        ol_ref[...] = jnp.concatenate([a_ref[0:rank, :], pad], axis=0).astype(BF16)


def _repack_w_in(w_in_t, lr0, rank, *, tw=1024):
    depth, c, d = w_in_t.shape
    assert lr0 % tw == 0 and (c - rank) % tw == 0 and tw % rank == 0 and rank % BF16_SUBLANES == 0
    kern = functools.partial(_repack_kernel, lr_tile=lr0 // tw, rank=rank)
    return pl.pallas_call(
        kern,
        grid=(depth, (c - rank) // tw),
        in_specs=[
            pl.BlockSpec((None, tw, d), lambda l, t: (l, t, 0)),
            pl.BlockSpec((None, rank, d), lambda l, t: (l, (t + 1) * (tw // rank), 0)),
        ],
        out_specs=[
            pl.BlockSpec((None, tw, d), lambda l, t: (l, t, 0)),
            pl.BlockSpec((None, LANES, d), lambda l, t: (l, 0, 0)),
        ],
        out_shape=[
            jax.ShapeDtypeStruct((depth, c - rank, d), BF16),
            jax.ShapeDtypeStruct((depth, LANES, d), BF16),
        ],
        compiler_params=_params(("arbitrary", "arbitrary")),
        name="repack_w_in",
    )(w_in_t, w_in_t)


def _in_proj_kernel(x_ref, g_ref, sc_ref, sh_ref, w_ref, wl_ref, o_ref, ol_ref, h_ref):
    j = pl.program_id(1)

    def project(h):
        o_ref[...] = lax.dot_general(h, w_ref[...], _NT, preferred_element_type=F32).astype(o_ref.dtype)

    @pl.when(j == 0)
    def _():
        h = _mod_norm(x_ref[...], g_ref[...], sc_ref[...], sh_ref[...]).astype(BF16)
        h_ref[...] = h
        ol_ref[...] = lax.dot_general(h, wl_ref[...], _NT, preferred_element_type=F32).astype(ol_ref.dtype)
        project(h)

    @pl.when(j > 0)
    def _():
        project(h_ref[...])


def _in_proj(x2, g, mod, w_main, w_lr, layer, *, seq, tm=1024, tn=2048):
    n, d = x2.shape
    c = w_main.shape[1]
    per_b = seq // tm
    return pl.pallas_call(
        _in_proj_kernel,
        grid=(n // tm, c // tn),
        in_specs=[
            pl.BlockSpec((tm, d), lambda i, j: (i, 0)),
            pl.BlockSpec((1, d), lambda i, j: (0, 0)),
            pl.BlockSpec((None, None, 1, d), lambda i, j: (i // per_b, 1, 0, 0)),
            pl.BlockSpec((None, None, 1, d), lambda i, j: (i // per_b, 0, 0, 0)),
            pl.BlockSpec((None, tn, d), lambda i, j: (layer, j, 0)),
            pl.BlockSpec((None, LANES, d), lambda i, j: (layer, 0, 0)),
        ],
        out_specs=[
            pl.BlockSpec((tm, tn), lambda i, j: (i, j)),
            pl.BlockSpec((tm, LANES), lambda i, j: (i, 0)),
        ],
        out_shape=[
            jax.ShapeDtypeStruct((n, c), BF16),
            jax.ShapeDtypeStruct((n, LANES), BF16),
        ],
        scratch_shapes=[pltpu.VMEM((tm, d), BF16)],
        compiler_params=_params(("arbitrary", "arbitrary")),
        name="in_proj",
    )(x2, g, mod, mod, w_main, w_lr)


def _split3(x):
    hi = x.astype(BF16)
    r1 = x - hi.astype(F32)
    mid = r1.astype(BF16)
    lo = (r1 - mid.astype(F32)).astype(BF16)
    return hi, mid, lo


def _gla_kernel(q_ref, k_ref, v_ref, gr_ref, lr_ref, w2_ref, gb_ref, ng_ref, o_ref,
                qm_ref, km_ref, qe_ref, ke_ref, el_ref, st_ref, *, heads, dk, dv, gate_rows):
    seq = q_ref.shape[0]
    cs = GLA_CHUNK
    kw = heads * dk
    per_trip = gate_rows // cs
    row = lax.broadcasted_iota(jnp.int32, (cs, cs), 0)
    col = lax.broadcasted_iota(jnp.int32, (cs, cs), 1)
    causal = col <= row
    tril = causal.astype(BF16)
    scale = dk ** -0.5

    def decay(pi, carry):
        base = pl.multiple_of(pi * gate_rows, gate_rows)
        r = pl.ds(base, gate_rows)
        xg = jnp.dot(lr_ref[r, :], w2_ref[...], preferred_element_type=F32) + gb_ref[...]
        g = (jnp.minimum(xg, 0.0) - jnp.log1p(jnp.exp(-jnp.abs(xg)))) / GLA_GATE_NORM
        parts = jnp.concatenate(_split3(g), axis=1)
        q = q_ref[r, :].astype(F32) * scale
        k = k_ref[r, :].astype(F32)
        for c in range(per_trip):
            rows = slice(c * cs, (c + 1) * cs)
            out = pl.ds(base + c * cs, cs)
            bs = jnp.dot(tril, parts[rows, :], preferred_element_type=F32)
            b = bs[:, :kw] + bs[:, kw:2 * kw] + bs[:, 2 * kw:]
            b_last = b[cs - 1:cs, :]
            b_mid = b[cs // 2 - 1:cs // 2, :]
            qm_ref[out, :] = (q[rows, :] * jnp.exp(b - b_mid)).astype(BF16)
            km_ref[out, :] = (k[rows, :] * jnp.exp(b_mid - b)).astype(BF16)
            qe_ref[out, :] = (q[rows, :] * jnp.exp(b)).astype(BF16)
            ke_ref[out, :] = (k[rows, :] * jnp.exp(b_last - b)).astype(BF16)
            el_ref[pl.ds(pi * per_trip + c, 1), :] = jnp.exp(b_last)
        return carry

    lax.fori_loop(0, seq // gate_rows, decay, 0)
    st_ref[...] = jnp.zeros_like(st_ref)

    def chunk(ci, carry):
        r = pl.ds(pl.multiple_of(ci * cs, cs), cs)
        e_last = el_ref[pl.ds(ci, 1), :]
        hs = range(heads)
        ks = [slice(h * dk, (h + 1) * dk) for h in hs]
        vs = [slice(h * dv, (h + 1) * dv) for h in hs]
        attn = [lax.dot_general(qm_ref[r, ks[h]], km_ref[r, ks[h]], _NT, preferred_element_type=F32) for h in hs]
        v = [v_ref[r, vs[h]] for h in hs]
        st = [st_ref[h] for h in hs]
        o_inter = [lax.dot_general(qe_ref[r, ks[h]], st[h].astype(BF16), _NT, preferred_element_type=F32)
                   for h in hs]
        kv = [jnp.dot(v[h].astype(F32).T.astype(BF16), ke_ref[r, ks[h]], preferred_element_type=F32) for h in hs]
        attn = [jnp.where(causal, attn[h], 0.0).astype(BF16) for h in hs]
        o = [jnp.dot(attn[h], v[h], preferred_element_type=F32) + o_inter[h] for h in hs]
        for h in hs:
            st_ref[h] = st[h] * e_last[:, ks[h]] + kv[h]
        for h in hs:
            ms = jnp.mean(o[h] * o[h], axis=-1, keepdims=True)
            y = o[h] * lax.rsqrt(ms + EPS) * ng_ref[...]
            o_ref[r, vs[h]] = (y * jax.nn.silu(gr_ref[r, vs[h]].astype(F32))).astype(o_ref.dtype)
        return carry

    lax.fori_loop(0, seq // cs, chunk, 0, unroll=2)


def _gla(proj, lr, w2p, gate_b, norm_g, *, batch, seq, key_w, val_w, gate_rows=512):
    n = proj.shape[0]
    dk = key_w // GLA_HEADS
    dv = val_w // GLA_HEADS
    vb = 2 * key_w // val_w
    rb = vb + 1
    kern = functools.partial(_gla_kernel, heads=GLA_HEADS, dk=dk, dv=dv, gate_rows=gate_rows)
    return pl.pallas_call(
        kern,
        grid=(batch,),
        in_specs=[
            pl.BlockSpec((seq, key_w), lambda b: (b, 0)),
            pl.BlockSpec((seq, key_w), lambda b: (b, 1)),
            pl.BlockSpec((seq, val_w), lambda b: (b, vb)),
            pl.BlockSpec((seq, val_w), lambda b: (b, rb)),
            pl.BlockSpec((seq, LANES), lambda b: (b, 0)),
            pl.BlockSpec((LANES, key_w), lambda b: (0, 0)),
            pl.BlockSpec((1, key_w), lambda b: (0, 0)),
            pl.BlockSpec((1, dv), lambda b: (0, 0)),
        ],
        out_specs=pl.BlockSpec((seq, val_w), lambda b: (b, 0)),
        out_shape=jax.ShapeDtypeStruct((n, val_w), BF16),
        scratch_shapes=[
            pltpu.VMEM((seq, key_w), BF16),
            pltpu.VMEM((seq, key_w), BF16),
            pltpu.VMEM((seq, key_w), BF16),
            pltpu.VMEM((seq, key_w), BF16),
            pltpu.VMEM((seq // GLA_CHUNK, key_w), F32),
            pltpu.VMEM((GLA_HEADS, dv, dk), F32),
        ],
        compiler_params=_params(("arbitrary",)),
        name="gla",
    )(proj, proj, proj, proj, lr, w2p, gate_b, norm_g)


def _moba_kernel(q_ref, k_ref, v_ref, o_ref, vt_ref, s_ref, pt_ref):
    bs = MOBA_BLOCK
    hd = MOBA_HEAD_DIM
    seq = k_ref.shape[0]
    nb = seq // bs
    scale = hd ** -0.5

    km = jnp.mean(k_ref[...].astype(F32).reshape(nb, bs, hd), axis=1)
    km = jnp.concatenate([km, jnp.zeros((BF16_SUBLANES - nb, hd), F32)], axis=0).astype(BF16)
    sc = lax.dot_general(km, q_ref[...], _NT, preferred_element_type=F32)[0:nb, :]
    blk = lax.broadcasted_iota(jnp.int32, sc.shape, 0)
    qpos = lax.broadcasted_iota(jnp.int32, sc.shape, 1)
    past = (blk + 1) * bs <= qpos
    sc = jnp.where(past, sc, -jnp.inf)
    beaten = jnp.zeros(sc.shape, F32)
    for jp in range(nb):
        c = sc[jp:jp + 1, :]
        ahead = (c > sc) | ((c == sc) & (blk > jp))
        beaten = beaten + ahead.astype(F32)
    sel_bias = jnp.where(past & (beaten < MOBA_TOPK), 0.0, -jnp.inf)

    vt_ref[0:hd, :] = v_ref[...].astype(F32).T.astype(BF16)
    ones_row = lax.broadcasted_iota(jnp.int32, (BF16_SUBLANES, seq), 0) == 0
    vt_ref[hd:hd + BF16_SUBLANES, :] = ones_row.astype(F32).astype(BF16)
    krow = lax.broadcasted_iota(jnp.int32, (bs, bs), 0)
    qcol = lax.broadcasted_iota(jnp.int32, (bs, bs), 1)
    own_bias = jnp.where(krow <= qcol, 0.0, -jnp.inf)

    def scores(i):
        nk = (i + 1) * bs
        s_ref[i % 2, 0:nk, :] = lax.dot_general(
            k_ref[0:nk, :], q_ref[i * bs:nk, :], _NT, preferred_element_type=F32)

    def softmax(i):
        slot = i % 2
        qs = slice(i * bs, (i + 1) * bs)
        nk = (i + 1) * bs
        biases = [sel_bias[j:j + 1, qs] for j in range(i)]
        m = (s_ref[slot, i * bs:nk, :] + own_bias).max(axis=0, keepdims=True)
        for j in range(i):
            m = jnp.maximum(m, s_ref[slot, j * bs:(j + 1) * bs, :].max(axis=0, keepdims=True) + biases[j])
        own = s_ref[slot, i * bs:nk, :] + own_bias
        pt_ref[slot, i * bs:nk, :] = jnp.exp2((own - m) * (scale * LOG2E)).astype(BF16)
        for j in range(i):
            p = jnp.exp2((s_ref[slot, j * bs:(j + 1) * bs, :] - (m - biases[j])) * (scale * LOG2E))
            pt_ref[slot, j * bs:(j + 1) * bs, :] = p.astype(BF16)

    def attend(i):
        nk = (i + 1) * bs
        ot = jnp.dot(vt_ref[:, 0:nk], pt_ref[i % 2, 0:nk, :], preferred_element_type=F32)
        o_ref[i * bs:nk, :] = (ot[0:hd, :] / ot[hd:hd + 1, :]).T.astype(o_ref.dtype)

    scores(0)
    for i in range(nb):
        if i + 1 < nb:
            scores(i + 1)
        if i > 0:
            attend(i - 1)
        softmax(i)
    attend(nb - 1)


def _moba(proj, *, batch, seq, col0, width):
    n = proj.shape[0]
    hd = MOBA_HEAD_DIM
    heads = width // hd
    qb = col0 // hd
    kb = qb + heads
    vb = kb + heads
    return pl.pallas_call(
        _moba_kernel,
        grid=(batch, heads),
        in_specs=[
            pl.BlockSpec((seq, hd), lambda b, h: (b, qb + h)),
            pl.BlockSpec((seq, hd), lambda b, h: (b, kb + h)),
            pl.BlockSpec((seq, hd), lambda b, h: (b, vb + h)),
        ],
        out_specs=pl.BlockSpec((seq, hd), lambda b, h: (b, h)),
        out_shape=jax.ShapeDtypeStruct((n, width), BF16),
        scratch_shapes=[
            pltpu.VMEM((hd + BF16_SUBLANES, seq), BF16),
            pltpu.VMEM((2, seq, MOBA_BLOCK), F32),
            pltpu.VMEM((2, seq, MOBA_BLOCK), BF16),
        ],
        compiler_params=_params(("arbitrary", "arbitrary")),
        name="moba",
    )(proj, proj, proj)


def _merge_kernel(og_ref, om_ref, wg_ref, wm_ref, g1_ref, g2_ref, wo_ref, x_ref, gt_ref, o_ref):
    yg = jnp.dot(og_ref[...], wg_ref[...], preferred_element_type=F32)
    ym = jnp.dot(om_ref[...], wm_ref[...], preferred_element_type=F32)
    z = jax.nn.sigmoid(g1_ref[...].astype(F32)) * yg + jax.nn.sigmoid(g2_ref[...].astype(F32)) * ym
    y = jnp.dot(z.astype(BF16), wo_ref[...], preferred_element_type=F32)
    o_ref[...] = x_ref[...] + gt_ref[...] * y


def _merge(o_gla, o_moba, w_up_gla, w_up_moba, proj, gate_col0, w_out, x2, mod, *, seq, tm=512):
    n, d = x2.shape
    per_b = seq // tm
    g1b = gate_col0 // d
    kg = o_gla.shape[1]
    km = o_moba.shape[1]
    resident = dict(pipeline_mode=pl.Buffered(1))
    return pl.pallas_call(
        _merge_kernel,
        grid=(n // tm,),
        in_specs=[
            pl.BlockSpec((tm, kg), lambda i: (i, 0)),
            pl.BlockSpec((tm, km), lambda i: (i, 0)),
            pl.BlockSpec((kg, d), lambda i: (0, 0), **resident),
            pl.BlockSpec((km, d), lambda i: (0, 0), **resident),
            pl.BlockSpec((tm, d), lambda i: (i, g1b)),
            pl.BlockSpec((tm, d), lambda i: (i, g1b + 1)),
            pl.BlockSpec((d, d), lambda i: (0, 0), **resident),
            pl.BlockSpec((tm, d), lambda i: (i, 0)),
            pl.BlockSpec((None, None, 1, d), lambda i: (i // per_b, 2, 0, 0)),
        ],
        out_specs=pl.BlockSpec((tm, d), lambda i: (i, 0)),
        out_shape=jax.ShapeDtypeStruct((n, d), F32),
        compiler_params=_params(("arbitrary",)),
        name="merge",
    )(o_gla, o_moba, w_up_gla, w_up_moba, proj, proj, w_out, x2, mod)


def _ffn_kernel(x_ref, g_ref, sc_ref, sh_ref, gt_ref, wa_ref, wu_ref, wo_ref, fg_ref, o_ref, h_ref, *, final_norm):
    j = pl.program_id(1)

    def part(h):
        a = jnp.dot(h, wa_ref[...], preferred_element_type=F32)
        u = jnp.dot(h, wu_ref[...], preferred_element_type=F32)
        return jnp.dot((jax.nn.silu(a) * u).astype(BF16), wo_ref[...], preferred_element_type=F32)

    @pl.when(j == 0)
    def _():
        h = _mod_norm(x_ref[...], g_ref[...], sc_ref[...], sh_ref[...]).astype(BF16)
        h_ref[...] = h
        o_ref[...] = part(h)

    @pl.when(j > 0)
    def _():
        o_ref[...] += part(h_ref[...])

    @pl.when(j == pl.num_programs(1) - 1)
    def _():
        y = x_ref[...] + gt_ref[...] * o_ref[...]
        if final_norm:
            ms = jnp.mean(y * y, axis=-1, keepdims=True)
            y = y * lax.rsqrt(ms + EPS) * fg_ref[...]
        o_ref[...] = y


def _ffn(x2, g, mod, w_in, w_out, final_g, *, seq, final_norm, tm=512, tf=512):
    n, d = x2.shape
    dff = w_out.shape[0]
    per_b = seq // tm
    nf = dff // tf
    kern = functools.partial(_ffn_kernel, final_norm=final_norm)
    return pl.pallas_call(
        kern,
        grid=(n // tm, nf),
        in_specs=[
            pl.BlockSpec((tm, d), lambda i, j: (i, 0)),
            pl.BlockSpec((1, d), lambda i, j: (0, 0)),
            pl.BlockSpec((None, None, 1, d), lambda i, j: (i // per_b, 4, 0, 0)),
            pl.BlockSpec((None, None, 1, d), lambda i, j: (i // per_b, 3, 0, 0)),
            pl.BlockSpec((None, None, 1, d), lambda i, j: (i // per_b, 5, 0, 0)),
            pl.BlockSpec((d, tf), lambda i, j: (0, j)),
            pl.BlockSpec((d, tf), lambda i, j: (0, nf + j)),
            pl.BlockSpec((tf, d), lambda i, j: (j, 0)),
            pl.BlockSpec((1, d), lambda i, j: (0, 0)),
        ],
        out_specs=pl.BlockSpec((tm, d), lambda i, j: (i, 0)),
        out_shape=jax.ShapeDtypeStruct((n, d), F32),
        scratch_shapes=[pltpu.VMEM((tm, d), BF16)],
        compiler_params=_params(("arbitrary", "arbitrary")),
        name="ffn",
    )(x2, g, mod, mod, mod, w_in, w_in, w_out, final_g)


def kernel(x, c, ada_w, ada_b, norm1_g, w_in, gla_gate_w2, gla_gate_b, gla_norm_g, w_up_gla, w_up_moba,
           w_out, norm2_g, w_ffn_in, w_ffn_out, final_g):
    batch, seq, d = x.shape
    depth = ada_w.shape[0]
    rank, key_w = gla_gate_w2.shape[1:]
    val_w = w_up_gla.shape[1]
    moba_w = w_up_moba.shape[1]
    n = batch * seq
    lr0 = 2 * key_w + 2 * val_w
    moba0 = lr0
    gate0 = moba0 + 3 * moba_w

    mod = _adaln(c, ada_w, ada_b).reshape(depth, batch, 6, 1, d)
    w_main, w_lr = _repack_w_in(jnp.swapaxes(w_in, 1, 2), lr0, rank)
    x2 = x.reshape(n, d)
    for l in range(depth):
        w2p = jnp.pad(gla_gate_w2[l], ((0, LANES - rank), (0, 0))).astype(BF16)
        proj, lr = _in_proj(x2, norm1_g[l][None], mod[l], w_main, w_lr, l, seq=seq)
        o_gla = _gla(proj, lr, w2p, gla_gate_b[l][None], gla_norm_g[l][None],
                     batch=batch, seq=seq, key_w=key_w, val_w=val_w)
        o_moba = _moba(proj, batch=batch, seq=seq, col0=moba0, width=moba_w)
        x2 = _merge(o_gla, o_moba, w_up_gla[l].astype(BF16), w_up_moba[l].astype(BF16), proj, gate0,
                    w_out[l].astype(BF16), x2, mod[l], seq=seq)
        x2 = _ffn(x2, norm2_g[l][None], mod[l], w_ffn_in[l].astype(BF16), w_ffn_out[l].astype(BF16),
                  final_g[None], seq=seq, final_norm=(l == depth - 1))
    return x2.reshape(batch, seq, d)
```

```python
import functools

import jax
import jax.numpy as jnp
from jax import lax
from jax.experimental import pallas as pl
from jax.experimental.pallas import tpu as pltpu

F32 = jnp.float32
BF16 = jnp.bfloat16

GLA_HEADS = 4
GLA_GATE_NORM = 16.0
GLA_CHUNK = 64
MOBA_HEAD_DIM = 128
MOBA_BLOCK = 256
MOBA_TOPK = 3
EPS = 1e-6
LOG2E = 1.4426950408889634

V7X_VMEM_BYTES = 64 * 1024 * 1024
VMEM_LIMIT_BYTES = V7X_VMEM_BYTES - 8 * 1024 * 1024
LANES = 128
BF16_SUBLANES = 16

_NT = (((1,), (1,)), ((), ()))


def _params(semantics):
    return pltpu.CompilerParams(dimension_semantics=semantics, vmem_limit_bytes=VMEM_LIMIT_BYTES)


def _mod_norm(x, g, sc, sh):
    ms = jnp.mean(x * x, axis=-1, keepdims=True)
    y = x * lax.rsqrt(ms + EPS) * g
    return y * (1.0 + sc) + sh


def _adaln_kernel(c_ref, w_ref, b_ref, o_ref):
    c_act = jax.nn.silu(c_ref[...])
    o_ref[...] = (
        jnp.dot(c_act.astype(BF16), w_ref[...].astype(BF16), preferred_element_type=F32) + b_ref[...]
    )


def _adaln(c, ada_w, ada_b, *, tn=1024):
    depth, d, n6 = ada_w.shape
    b = c.shape[0]
    return pl.pallas_call(
        _adaln_kernel,
        grid=(depth, n6 // tn),
        in_specs=[
            pl.BlockSpec((b, d), lambda l, j: (0, 0)),
            pl.BlockSpec((None, d, tn), lambda l, j: (l, 0, j)),
            pl.BlockSpec((None, 1, tn), lambda l, j: (l, 0, j)),
        ],
        out_specs=pl.BlockSpec((None, b, tn), lambda l, j: (l, 0, j)),
        out_shape=jax.ShapeDtypeStruct((depth, b, n6), F32),
        compiler_params=_params(("arbitrary", "arbitrary")),
        name="adaln",
    )(c, ada_w, ada_b.reshape(depth, 1, n6))


def _repack_kernel(a_ref, b_ref, o_ref, ol_ref, *, lr_tile, rank):
    t = pl.program_id(1)

    @pl.when(t < lr_tile)
    def _():
        o_ref[...] = a_ref[...].astype(BF16)

    @pl.when(t >= lr_tile)
    def _():
        o_ref[...] = jnp.concatenate([a_ref[rank:, :], b_ref[...]], axis=0).astype(BF16)

    @pl.when(t == lr_tile)
    def _():
        pad = jnp.zeros((ol_ref.shape[0] - rank, ol_ref.shape[1]), F32)
        ol_ref[...] = jnp.concatenate([a_ref[0:rank, :], pad], axis=0).astype(BF16)


def _repack_w_in(w_in_t, lr0, rank, *, tw=1024):
    depth, c, d = w_in_t.shape
    assert lr0 % tw == 0 and (c - rank) % tw == 0 and tw % rank == 0 and rank % BF16_SUBLANES == 0
    kern = functools.partial(_repack_kernel, lr_tile=lr0 // tw, rank=rank)
    return pl.pallas_call(
        kern,
        grid=(depth, (c - rank) // tw),
        in_specs=[
            pl.BlockSpec((None, tw, d), lambda l, t: (l, t, 0)),
            pl.BlockSpec((None, rank, d), lambda l, t: (l, (t + 1) * (tw // rank), 0)),
        ],
        out_specs=[
            pl.BlockSpec((None, tw, d), lambda l, t: (l, t, 0)),
            pl.BlockSpec((None, LANES, d), lambda l, t: (l, 0, 0)),
        ],
        out_shape=[
            jax.ShapeDtypeStruct((depth, c - rank, d), BF16),
            jax.ShapeDtypeStruct((depth, LANES, d), BF16),
        ],
        compiler_params=_params(("arbitrary", "arbitrary")),
        name="repack_w_in",
    )(w_in_t, w_in_t)


def _in_proj_kernel(x_ref, g_ref, sc_ref, sh_ref, w_ref, wl_ref, o_ref, ol_ref, h_ref):
    j = pl.program_id(1)

    def project(h):
        o_ref[...] = lax.dot_general(h, w_ref[...], _NT, preferred_element_type=F32).astype(o_ref.dtype)

    @pl.when(j == 0)
    def _():
        h = _mod_norm(x_ref[...], g_ref[...], sc_ref[...], sh_ref[...]).astype(BF16)
        h_ref[...] = h
        ol_ref[...] = lax.dot_general(h, wl_ref[...], _NT, preferred_element_type=F32).astype(ol_ref.dtype)
        project(h)

    @pl.when(j > 0)
    def _():
        project(h_ref[...])


def _in_proj(x2, g, mod, w_main, w_lr, layer, *, seq, tm=1024, tn=2048):
    n, d = x2.shape
    c = w_main.shape[1]
    per_b = seq // tm
    return pl.pallas_call(
        _in_proj_kernel,
        grid=(n // tm, c // tn),
        in_specs=[
            pl.BlockSpec((tm, d), lambda i, j: (i, 0)),
            pl.BlockSpec((1, d), lambda i, j: (0, 0)),
            pl.BlockSpec((None, None, 1, d), lambda i, j: (i // per_b, 1, 0, 0)),
            pl.BlockSpec((None, None, 1, d), lambda i, j: (i // per_b, 0, 0, 0)),
            pl.BlockSpec((None, tn, d), lambda i, j: (layer, j, 0)),
            pl.BlockSpec((None, LANES, d), lambda i, j: (layer, 0, 0)),
        ],
        out_specs=[
            pl.BlockSpec((tm, tn), lambda i, j: (i, j)),
            pl.BlockSpec((tm, LANES), lambda i, j: (i, 0)),
        ],
        out_shape=[
            jax.ShapeDtypeStruct((n, c), BF16),
            jax.ShapeDtypeStruct((n, LANES), BF16),
        ],
        scratch_shapes=[pltpu.VMEM((tm, d), BF16)],
        compiler_params=_params(("arbitrary", "arbitrary")),
        name="in_proj",
    )(x2, g, mod, mod, w_main, w_lr)


def _split3(x):
    hi = x.astype(BF16)
    r1 = x - hi.astype(F32)
    mid = r1.astype(BF16)
    lo = (r1 - mid.astype(F32)).astype(BF16)
    return hi, mid, lo


def _gla_kernel(q_ref, k_ref, v_ref, gr_ref, lr_ref, w2_ref, gb_ref, ng_ref, o_ref,
                qm_ref, km_ref, qe_ref, ke_ref, el_ref, st_ref, *, heads, dk, dv, gate_rows):
    seq = q_ref.shape[0]
    cs = GLA_CHUNK
    kw = heads * dk
    per_trip = gate_rows // cs
    row = lax.broadcasted_iota(jnp.int32, (cs, cs), 0)
    col = lax.broadcasted_iota(jnp.int32, (cs, cs), 1)
    causal = col <= row
    tril = causal.astype(BF16)
    scale = dk ** -0.5

    def decay(pi, carry):
        base = pl.multiple_of(pi * gate_rows, gate_rows)
        r = pl.ds(base, gate_rows)
        xg = jnp.dot(lr_ref[r, :], w2_ref[...], preferred_element_type=F32) + gb_ref[...]
        g = (jnp.minimum(xg, 0.0) - jnp.log1p(jnp.exp(-jnp.abs(xg)))) / GLA_GATE_NORM
        parts = jnp.concatenate(_split3(g), axis=1)
        q = q_ref[r, :].astype(F32) * scale
        k = k_ref[r, :].astype(F32)
        for c in range(per_trip):
            rows = slice(c * cs, (c + 1) * cs)
            out = pl.ds(base + c * cs, cs)
            bs = jnp.dot(tril, parts[rows, :], preferred_element_type=F32)
            b = bs[:, :kw] + bs[:, kw:2 * kw] + bs[:, 2 * kw:]
            b_last = b[cs - 1:cs, :]
            b_mid = b[cs // 2 - 1:cs // 2, :]
            qm_ref[out, :] = (q[rows, :] * jnp.exp(b - b_mid)).astype(BF16)
            km_ref[out, :] = (k[rows, :] * jnp.exp(b_mid - b)).astype(BF16)
            qe_ref[out, :] = (q[rows, :] * jnp.exp(b)).astype(BF16)
            ke_ref[out, :] = (k[rows, :] * jnp.exp(b_last - b)).astype(BF16)
            el_ref[pl.ds(pi * per_trip + c, 1), :] = jnp.exp(b_last)
        return carry

    lax.fori_loop(0, seq // gate_rows, decay, 0)
    st_ref[...] = jnp.zeros_like(st_ref)

    def chunk(ci, carry):
        r = pl.ds(pl.multiple_of(ci * cs, cs), cs)
        e_last = el_ref[pl.ds(ci, 1), :]
        hs = range(heads)
        ks = [slice(h * dk, (h + 1) * dk) for h in hs]
        vs = [slice(h * dv, (h + 1) * dv) for h in hs]
        attn = [lax.dot_general(qm_ref[r, ks[h]], km_ref[r, ks[h]], _NT, preferred_element_type=F32) for h in hs]
        v = [v_ref[r, vs[h]] for h in hs]
        st = [st_ref[h] for h in hs]
        o_inter = [lax.dot_general(qe_ref[r, ks[h]], st[h].astype(BF16), _NT, preferred_element_type=F32)
                   for h in hs]
        kv = [jnp.dot(v[h].astype(F32).T.astype(BF16), ke_ref[r, ks[h]], preferred_element_type=F32) for h in hs]
        attn = [jnp.where(causal, attn[h], 0.0).astype(BF16) for h in hs]
        o = [jnp.dot(attn[h], v[h], preferred_element_type=F32) + o_inter[h] for h in hs]
        for h in hs:
            st_ref[h] = st[h] * e_last[:, ks[h]] + kv[h]
        for h in hs:
            ms = jnp.mean(o[h] * o[h], axis=-1, keepdims=True)
            y = o[h] * lax.rsqrt(ms + EPS) * ng_ref[...]
            o_ref[r, vs[h]] = (y * jax.nn.silu(gr_ref[r, vs[h]].astype(F32))).astype(o_ref.dtype)
        return carry

    lax.fori_loop(0, seq // cs, chunk, 0, unroll=2)


def _gla(proj, lr, w2p, gate_b, norm_g, *, batch, seq, key_w, val_w, gate_rows=512):
    n = proj.shape[0]
    dk = key_w // GLA_HEADS
    dv = val_w // GLA_HEADS
    vb = 2 * key_w // val_w
    rb = vb + 1
    kern = functools.partial(_gla_kernel, heads=GLA_HEADS, dk=dk, dv=dv, gate_rows=gate_rows)
    return pl.pallas_call(
        kern,
        grid=(batch,),
        in_specs=[
            pl.BlockSpec((seq, key_w), lambda b: (b, 0)),
            pl.BlockSpec((seq, key_w), lambda b: (b, 1)),
            pl.BlockSpec((seq, val_w), lambda b: (b, vb)),
            pl.BlockSpec((seq, val_w), lambda b: (b, rb)),
            pl.BlockSpec((seq, LANES), lambda b: (b, 0)),
            pl.BlockSpec((LANES, key_w), lambda b: (0, 0)),
            pl.BlockSpec((1, key_w), lambda b: (0, 0)),
            pl.BlockSpec((1, dv), lambda b: (0, 0)),
        ],
        out_specs=pl.BlockSpec((seq, val_w), lambda b: (b, 0)),
        out_shape=jax.ShapeDtypeStruct((n, val_w), BF16),
        scratch_shapes=[
            pltpu.VMEM((seq, key_w), BF16),
            pltpu.VMEM((seq, key_w), BF16),
            pltpu.VMEM((seq, key_w), BF16),
            pltpu.VMEM((seq, key_w), BF16),
            pltpu.VMEM((seq // GLA_CHUNK, key_w), F32),
            pltpu.VMEM((GLA_HEADS, dv, dk), F32),
        ],
        compiler_params=_params(("arbitrary",)),
        name="gla",
    )(proj, proj, proj, proj, lr, w2p, gate_b, norm_g)


def _moba_kernel(q_ref, k_ref, v_ref, o_ref, vt_ref, s_ref, pt_ref):
    bs = MOBA_BLOCK
    hd = MOBA_HEAD_DIM
    seq = k_ref.shape[0]
    nb = seq // bs
    scale = hd ** -0.5

    km = jnp.mean(k_ref[...].astype(F32).reshape(nb, bs, hd), axis=1)
    km = jnp.concatenate([km, jnp.zeros((BF16_SUBLANES - nb, hd), F32)], axis=0).astype(BF16)
    sc = lax.dot_general(km, q_ref[...], _NT, preferred_element_type=F32)[0:nb, :]
    blk = lax.broadcasted_iota(jnp.int32, sc.shape, 0)
    qpos = lax.broadcasted_iota(jnp.int32, sc.shape, 1)
    past = (blk + 1) * bs <= qpos
    sc = jnp.where(past, sc, -jnp.inf)
    beaten = jnp.zeros(sc.shape, F32)
    for jp in range(nb):
        c = sc[jp:jp + 1, :]
        ahead = (c > sc) | ((c == sc) & (blk > jp))
        beaten = beaten + ahead.astype(F32)
    sel_bias = jnp.where(past & (beaten < MOBA_TOPK), 0.0, -jnp.inf)

    vt_ref[0:hd, :] = v_ref[...].astype(F32).T.astype(BF16)
    ones_row = lax.broadcasted_iota(jnp.int32, (BF16_SUBLANES, seq), 0) == 0
    vt_ref[hd:hd + BF16_SUBLANES, :] = ones_row.astype(F32).astype(BF16)
    krow = lax.broadcasted_iota(jnp.int32, (bs, bs), 0)
    qcol = lax.broadcasted_iota(jnp.int32, (bs, bs), 1)
    own_bias = jnp.where(krow <= qcol, 0.0, -jnp.inf)

    def scores(i):
        nk = (i + 1) * bs
        s_ref[i % 2, 0:nk, :] = lax.dot_general(
            k_ref[0:nk, :], q_ref[i * bs:nk, :], _NT, preferred_element_type=F32)

    def softmax(i):
        slot = i % 2
        qs = slice(i * bs, (i + 1) * bs)
        nk = (i + 1) * bs
        biases = [sel_bias[j:j + 1, qs] for j in range(i)]
        m = (s_ref[slot, i * bs:nk, :] + own_bias).max(axis=0, keepdims=True)
        for j in range(i):
            m = jnp.maximum(m, s_ref[slot, j * bs:(j + 1) * bs, :].max(axis=0, keepdims=True) + biases[j])
        own = s_ref[slot, i * bs:nk, :] + own_bias
        pt_ref[slot, i * bs:nk, :] = jnp.exp2((own - m) * (scale * LOG2E)).astype(BF16)
        for j in range(i):
            p = jnp.exp2((s_ref[slot, j * bs:(j + 1) * bs, :] - (m - biases[j])) * (scale * LOG2E))
            pt_ref[slot, j * bs:(j + 1) * bs, :] = p.astype(BF16)

    def attend(i):
        nk = (i + 1) * bs
        ot = jnp.dot(vt_ref[:, 0:nk], pt_ref[i % 2, 0:nk, :], preferred_element_type=F32)
        o_ref[i * bs:nk, :] = (ot[0:hd, :] / ot[hd:hd + 1, :]).T.astype(o_ref.dtype)

    scores(0)
    for i in range(nb):
        if i + 1 < nb:
            scores(i + 1)
        if i > 0:
            attend(i - 1)
        softmax(i)
    attend(nb - 1)


def _moba(proj, *, batch, seq, col0, width):
    n = proj.shape[0]
    hd = MOBA_HEAD_DIM
    heads = width // hd
    qb = col0 // hd
    kb = qb + heads
    vb = kb + heads
    return pl.pallas_call(
        _moba_kernel,
        grid=(batch, heads),
        in_specs=[
            pl.BlockSpec((seq, hd), lambda b, h: (b, qb + h)),
            pl.BlockSpec((seq, hd), lambda b, h: (b, kb + h)),
            pl.BlockSpec((seq, hd), lambda b, h: (b, vb + h)),
        ],
        out_specs=pl.BlockSpec((seq, hd), lambda b, h: (b, h)),
        out_shape=jax.ShapeDtypeStruct((n, width), BF16),
        scratch_shapes=[
            pltpu.VMEM((hd + BF16_SUBLANES, seq), BF16),
            pltpu.VMEM((2, seq, MOBA_BLOCK), F32),
            pltpu.VMEM((2, seq, MOBA_BLOCK), BF16),
        ],
        compiler_params=_params(("arbitrary", "arbitrary")),
        name="moba",
    )(proj, proj, proj)


def _merge_kernel(og_ref, om_ref, wg_ref, wm_ref, g1_ref, g2_ref, wo_ref, x_ref, gt_ref, o_ref):
    yg = jnp.dot(og_ref[...], wg_ref[...], preferred_element_type=F32)
    ym = jnp.dot(om_ref[...], wm_ref[...], preferred_element_type=F32)
    z = jax.nn.sigmoid(g1_ref[...].astype(F32)) * yg + jax.nn.sigmoid(g2_ref[...].astype(F32)) * ym
    y = jnp.dot(z.astype(BF16), wo_ref[...], preferred_element_type=F32)
    o_ref[...] = x_ref[...] + gt_ref[...] * y


def _merge(o_gla, o_moba, w_up_gla, w_up_moba, proj, gate_col0, w_out, x2, mod, layer, *, seq, tm=512):
    n, d = x2.shape
    per_b = seq // tm
    g1b = gate_col0 // d
    kg = o_gla.shape[1]
    km = o_moba.shape[1]
    resident = dict(pipeline_mode=pl.Buffered(1))
    return pl.pallas_call(
        _merge_kernel,
        grid=(n // tm,),
        in_specs=[
            pl.BlockSpec((tm, kg), lambda i: (i, 0)),
            pl.BlockSpec((tm, km), lambda i: (i, 0)),
            pl.BlockSpec((None, kg, d), lambda i: (layer, 0, 0), **resident),
            pl.BlockSpec((None, km, d), lambda i: (layer, 0, 0), **resident),
            pl.BlockSpec((tm, d), lambda i: (i, g1b)),
            pl.BlockSpec((tm, d), lambda i: (i, g1b + 1)),
            pl.BlockSpec((None, d, d), lambda i: (layer, 0, 0), **resident),
            pl.BlockSpec((tm, d), lambda i: (i, 0)),
            pl.BlockSpec((None, None, 1, d), lambda i: (i // per_b, 2, 0, 0)),
        ],
        out_specs=pl.BlockSpec((tm, d), lambda i: (i, 0)),
        out_shape=jax.ShapeDtypeStruct((n, d), F32),
        compiler_params=_params(("arbitrary",)),
        name="merge",
    )(o_gla, o_moba, w_up_gla, w_up_moba, proj, proj, w_out, x2, mod)


def _ffn_kernel(x_ref, g_ref, sc_ref, sh_ref, gt_ref, wa_ref, wu_ref, wo_ref, fg_ref, o_ref, h_ref, *, final_norm):
    j = pl.program_id(1)

    def part(h):
        a = jnp.dot(h, wa_ref[...], preferred_element_type=F32)
        u = jnp.dot(h, wu_ref[...], preferred_element_type=F32)
        return jnp.dot((jax.nn.silu(a) * u).astype(BF16), wo_ref[...], preferred_element_type=F32)

    @pl.when(j == 0)
    def _():
        h = _mod_norm(x_ref[...], g_ref[...], sc_ref[...], sh_ref[...]).astype(BF16)
        h_ref[...] = h
        o_ref[...] = part(h)

    @pl.when(j > 0)
    def _():
        o_ref[...] += part(h_ref[...])

    @pl.when(j == pl.num_programs(1) - 1)
    def _():
        y = x_ref[...] + gt_ref[...] * o_ref[...]
        if final_norm:
            ms = jnp.mean(y * y, axis=-1, keepdims=True)
            y = y * lax.rsqrt(ms + EPS) * fg_ref[...]
        o_ref[...] = y


def _ffn(x2, g, mod, w_in, w_out, final_g, layer, *, seq, final_norm, tm=1024, tf=256):
    n, d = x2.shape
    dff = w_out.shape[1]
    per_b = seq // tm
    nf = dff // tf
    kern = functools.partial(_ffn_kernel, final_norm=final_norm)
    return pl.pallas_call(
        kern,
        grid=(n // tm, nf),
        in_specs=[
            pl.BlockSpec((tm, d), lambda i, j: (i, 0)),
            pl.BlockSpec((1, d), lambda i, j: (0, 0)),
            pl.BlockSpec((None, None, 1, d), lambda i, j: (i // per_b, 4, 0, 0)),
            pl.BlockSpec((None, None, 1, d), lambda i, j: (i // per_b, 3, 0, 0)),
            pl.BlockSpec((None, None, 1, d), lambda i, j: (i // per_b, 5, 0, 0)),
            pl.BlockSpec((None, d, tf), lambda i, j: (layer, 0, j)),
            pl.BlockSpec((None, d, tf), lambda i, j: (layer, 0, nf + j)),
            pl.BlockSpec((None, tf, d), lambda i, j: (layer, j, 0)),
            pl.BlockSpec((1, d), lambda i, j: (0, 0)),
        ],
        out_specs=pl.BlockSpec((tm, d), lambda i, j: (i, 0)),
        out_shape=jax.ShapeDtypeStruct((n, d), F32),
        scratch_shapes=[pltpu.VMEM((tm, d), BF16)],
        compiler_params=_params(("arbitrary", "arbitrary")),
        name="ffn",
    )(x2, g, mod, mod, mod, w_in, w_in, w_out, final_g)


def kernel(x, c, ada_w, ada_b, norm1_g, w_in, gla_gate_w2, gla_gate_b, gla_norm_g, w_up_gla, w_up_moba,
           w_out, norm2_g, w_ffn_in, w_ffn_out, final_g):
    batch, seq, d = x.shape
    depth = ada_w.shape[0]
    rank, key_w = gla_gate_w2.shape[1:]
    val_w = w_up_gla.shape[1]
    moba_w = w_up_moba.shape[1]
    n = batch * seq
    lr0 = 2 * key_w + 2 * val_w
    moba0 = lr0
    gate0 = moba0 + 3 * moba_w

    mod = _adaln(c, ada_w, ada_b).reshape(depth, batch, 6, 1, d)
    w_main, w_lr = _repack_w_in(jnp.swapaxes(w_in, 1, 2), lr0, rank)
    w_up_gla_b, w_up_moba_b, w_out_b = w_up_gla.astype(BF16), w_up_moba.astype(BF16), w_out.astype(BF16)
    w_ffn_in_b, w_ffn_out_b = w_ffn_in.astype(BF16), w_ffn_out.astype(BF16)
    x2 = x.reshape(n, d)
    for l in range(depth):
        w2p = jnp.pad(gla_gate_w2[l], ((0, LANES - rank), (0, 0))).astype(BF16)
        proj, lr = _in_proj(x2, norm1_g[l][None], mod[l], w_main, w_lr, l, seq=seq)
        o_gla = _gla(proj, lr, w2p, gla_gate_b[l][None], gla_norm_g[l][None],
                     batch=batch, seq=seq, key_w=key_w, val_w=val_w)
        o_moba = _moba(proj, batch=batch, seq=seq, col0=moba0, width=moba_w)
        x2 = _merge(o_gla, o_moba, w_up_gla_b, w_up_moba_b, proj, gate0, w_out_b, x2, mod[l], l, seq=seq)
        x2 = _ffn(x2, norm2_g[l][None], mod[l], w_ffn_in_b, w_ffn_out_b, final_g[None], l,
                  seq=seq, final_norm=(l == depth - 1))
    return x2.reshape(batch, seq, d)
```

```python
import functools

import jax
import jax.numpy as jnp
from jax import lax
from jax.experimental import pallas as pl
from jax.experimental.pallas import tpu as pltpu

F32 = jnp.float32
BF16 = jnp.bfloat16

GLA_HEADS = 4
GLA_GATE_NORM = 16.0
GLA_CHUNK = 64
MOBA_HEAD_DIM = 128
MOBA_BLOCK = 256
MOBA_TOPK = 3
EPS = 1e-6
LOG2E = 1.4426950408889634

V7X_VMEM_BYTES = 64 * 1024 * 1024
VMEM_LIMIT_BYTES = V7X_VMEM_BYTES - 8 * 1024 * 1024
LANES = 128
BF16_SUBLANES = 16

_NT = (((1,), (1,)), ((), ()))


def _params(semantics):
    return pltpu.CompilerParams(dimension_semantics=semantics, vmem_limit_bytes=VMEM_LIMIT_BYTES)


def _mod_norm(x, g, sc, sh):
    ms = jnp.mean(x * x, axis=-1, keepdims=True)
    y = x * lax.rsqrt(ms + EPS) * g
    return y * (1.0 + sc) + sh


def _adaln_kernel(c_ref, w_ref, b_ref, o_ref):
    c_act = jax.nn.silu(c_ref[...])
    o_ref[...] = (
        jnp.dot(c_act.astype(BF16), w_ref[...].astype(BF16), preferred_element_type=F32) + b_ref[...]
    )


def _adaln(c, ada_w, ada_b, *, tn=1024):
    depth, d, n6 = ada_w.shape
    b = c.shape[0]
    return pl.pallas_call(
        _adaln_kernel,
        grid=(depth, n6 // tn),
        in_specs=[
            pl.BlockSpec((b, d), lambda l, j: (0, 0)),
            pl.BlockSpec((None, d, tn), lambda l, j: (l, 0, j)),
            pl.BlockSpec((None, 1, tn), lambda l, j: (l, 0, j)),
        ],
        out_specs=pl.BlockSpec((None, b, tn), lambda l, j: (l, 0, j)),
        out_shape=jax.ShapeDtypeStruct((depth, b, n6), F32),
        compiler_params=_params(("arbitrary", "arbitrary")),
        name="adaln",
    )(c, ada_w, ada_b.reshape(depth, 1, n6))


def _repack_kernel(a_ref, b_ref, o_ref, ol_ref, *, lr_tile, rank):
    t = pl.program_id(1)

    @pl.when(t < lr_tile)
    def _():
        o_ref[...] = a_ref[...].astype(BF16)

    @pl.when(t >= lr_tile)
    def _():
        o_ref[...] = jnp.concatenate([a_ref[rank:, :], b_ref[...]], axis=0).astype(BF16)

    @pl.when(t == lr_tile)
    def _():
        pad = jnp.zeros((ol_ref.shape[0] - rank, ol_ref.shape[1]), F32)
        ol_ref[...] = jnp.concatenate([a_ref[0:rank, :], pad], axis=0).astype(BF16)


def _repack_w_in(w_in_t, lr0, rank, *, tw=1024):
    depth, c, d = w_in_t.shape
    assert lr0 % tw == 0 and (c - rank) % tw == 0 and tw % rank == 0 and rank % BF16_SUBLANES == 0
    kern = functools.partial(_repack_kernel, lr_tile=lr0 // tw, rank=rank)
    return pl.pallas_call(
        kern,
        grid=(depth, (c - rank) // tw),
        in_specs=[
            pl.BlockSpec((None, tw, d), lambda l, t: (l, t, 0)),
            pl.BlockSpec((None, rank, d), lambda l, t: (l, (t + 1) * (tw // rank), 0)),
        ],
        out_specs=[
            pl.BlockSpec((None, tw, d), lambda l, t: (l, t, 0)),
            pl.BlockSpec((None, LANES, d), lambda l, t: (l, 0, 0)),
        ],
        out_shape=[
            jax.ShapeDtypeStruct((depth, c - rank, d), BF16),
            jax.ShapeDtypeStruct((depth, LANES, d), BF16),
        ],
        compiler_params=_params(("arbitrary", "arbitrary")),
        name="repack_w_in",
    )(w_in_t, w_in_t)


def _in_proj_kernel(x_ref, g_ref, sc_ref, sh_ref, w_ref, wl_ref, o_ref, ol_ref, h_ref):
    j = pl.program_id(1)

    def project(h):
        o_ref[...] = lax.dot_general(h, w_ref[...], _NT, preferred_element_type=F32).astype(o_ref.dtype)

    @pl.when(j == 0)
    def _():
        h = _mod_norm(x_ref[...], g_ref[...], sc_ref[...], sh_ref[...]).astype(BF16)
        h_ref[...] = h
        ol_ref[...] = lax.dot_general(h, wl_ref[...], _NT, preferred_element_type=F32).astype(ol_ref.dtype)
        project(h)

    @pl.when(j > 0)
    def _():
        project(h_ref[...])


def _in_proj(x2, g, mod, w_main, w_lr, layer, *, seq, tm=1024, tn=2048):
    n, d = x2.shape
    c = w_main.shape[1]
    per_b = seq // tm
    return pl.pallas_call(
        _in_proj_kernel,
        grid=(n // tm, c // tn),
        in_specs=[
            pl.BlockSpec((tm, d), lambda i, j: (i, 0)),
            pl.BlockSpec((1, d), lambda i, j: (0, 0)),
            pl.BlockSpec((None, None, 1, d), lambda i, j: (i // per_b, 1, 0, 0)),
            pl.BlockSpec((None, None, 1, d), lambda i, j: (i // per_b, 0, 0, 0)),
            pl.BlockSpec((None, tn, d), lambda i, j: (layer, j, 0)),
            pl.BlockSpec((None, LANES, d), lambda i, j: (layer, 0, 0)),
        ],
        out_specs=[
            pl.BlockSpec((tm, tn), lambda i, j: (i, j)),
            pl.BlockSpec((tm, LANES), lambda i, j: (i, 0)),
        ],
        out_shape=[
            jax.ShapeDtypeStruct((n, c), BF16),
            jax.ShapeDtypeStruct((n, LANES), BF16),
        ],
        scratch_shapes=[pltpu.VMEM((tm, d), BF16)],
        compiler_params=_params(("arbitrary", "arbitrary")),
        name="in_proj",
    )(x2, g, mod, mod, w_main, w_lr)


def _split3(x):
    hi = x.astype(BF16)
    r1 = x - hi.astype(F32)
    mid = r1.astype(BF16)
    lo = (r1 - mid.astype(F32)).astype(BF16)
    return hi, mid, lo


def _gla_kernel(q_ref, k_ref, v_ref, gr_ref, lr_ref, w2_ref, gb_ref, ng_ref, o_ref,
                qm_ref, km_ref, qe_ref, ke_ref, el_ref, st_ref, *, heads, dk, dv, gate_rows):
    seq = q_ref.shape[0]
    cs = GLA_CHUNK
    kw = heads * dk
    per_trip = gate_rows // cs
    row = lax.broadcasted_iota(jnp.int32, (cs, cs), 0)
    col = lax.broadcasted_iota(jnp.int32, (cs, cs), 1)
    causal = col <= row
    tril = causal.astype(BF16)
    scale = dk ** -0.5

    def decay(pi, carry):
        base = pl.multiple_of(pi * gate_rows, gate_rows)
        r = pl.ds(base, gate_rows)
        xg = jnp.dot(lr_ref[r, :], w2_ref[...], preferred_element_type=F32) + gb_ref[...]
        g = (jnp.minimum(xg, 0.0) - jnp.log1p(jnp.exp(-jnp.abs(xg)))) / GLA_GATE_NORM
        parts = jnp.concatenate(_split3(g), axis=1)
        q = q_ref[r, :].astype(F32) * scale
        k = k_ref[r, :].astype(F32)
        for c in range(per_trip):
            rows = slice(c * cs, (c + 1) * cs)
            out = pl.ds(base + c * cs, cs)
            bs = jnp.dot(tril, parts[rows, :], preferred_element_type=F32)
            b = bs[:, :kw] + bs[:, kw:2 * kw] + bs[:, 2 * kw:]
            b_last = b[cs - 1:cs, :]
            b_mid = b[cs // 2 - 1:cs // 2, :]
            qm_ref[out, :] = (q[rows, :] * jnp.exp(b - b_mid)).astype(BF16)
            km_ref[out, :] = (k[rows, :] * jnp.exp(b_mid - b)).astype(BF16)
            qe_ref[out, :] = (q[rows, :] * jnp.exp(b)).astype(BF16)
            ke_ref[out, :] = (k[rows, :] * jnp.exp(b_last - b)).astype(BF16)
            el_ref[pl.ds(pi * per_trip + c, 1), :] = jnp.exp(b_last)
        return carry

    lax.fori_loop(0, seq // gate_rows, decay, 0)
    st_ref[...] = jnp.zeros_like(st_ref)

    def chunk(ci, carry):
        r = pl.ds(pl.multiple_of(ci * cs, cs), cs)
        e_last = el_ref[pl.ds(ci, 1), :]
        hs = range(heads)
        ks = [slice(h * dk, (h + 1) * dk) for h in hs]
        vs = [slice(h * dv, (h + 1) * dv) for h in hs]
        attn = [lax.dot_general(qm_ref[r, ks[h]], km_ref[r, ks[h]], _NT, preferred_element_type=F32) for h in hs]
        v = [v_ref[r, vs[h]] for h in hs]
        st = [st_ref[h] for h in hs]
        o_inter = [lax.dot_general(qe_ref[r, ks[h]], st[h].astype(BF16), _NT, preferred_element_type=F32)
                   for h in hs]
        kv = [jnp.dot(v[h].astype(F32).T.astype(BF16), ke_ref[r, ks[h]], preferred_element_type=F32) for h in hs]
        attn = [jnp.where(causal, attn[h], 0.0).astype(BF16) for h in hs]
        o = [jnp.dot(attn[h], v[h], preferred_element_type=F32) + o_inter[h] for h in hs]
        for h in hs:
            st_ref[h] = st[h] * e_last[:, ks[h]] + kv[h]
        for h in hs:
            ms = jnp.mean(o[h] * o[h], axis=-1, keepdims=True)
            y = o[h] * lax.rsqrt(ms + EPS) * ng_ref[...]
            o_ref[r, vs[h]] = (y * jax.nn.silu(gr_ref[r, vs[h]].astype(F32))).astype(o_ref.dtype)
        return carry

    lax.fori_loop(0, seq // cs, chunk, 0, unroll=2)


def _gla(proj, lr, w2p, gate_b, norm_g, *, batch, seq, key_w, val_w, gate_rows=512):
    n = proj.shape[0]
    dk = key_w // GLA_HEADS
    dv = val_w // GLA_HEADS
    vb = 2 * key_w // val_w
    rb = vb + 1
    kern = functools.partial(_gla_kernel, heads=GLA_HEADS, dk=dk, dv=dv, gate_rows=gate_rows)
    return pl.pallas_call(
        kern,
        grid=(batch,),
        in_specs=[
            pl.BlockSpec((seq, key_w), lambda b: (b, 0)),
            pl.BlockSpec((seq, key_w), lambda b: (b, 1)),
            pl.BlockSpec((seq, val_w), lambda b: (b, vb)),
            pl.BlockSpec((seq, val_w), lambda b: (b, rb)),
            pl.BlockSpec((seq, LANES), lambda b: (b, 0)),
            pl.BlockSpec((LANES, key_w), lambda b: (0, 0)),
            pl.BlockSpec((1, key_w), lambda b: (0, 0)),
            pl.BlockSpec((1, dv), lambda b: (0, 0)),
        ],
        out_specs=pl.BlockSpec((seq, val_w), lambda b: (b, 0)),
        out_shape=jax.ShapeDtypeStruct((n, val_w), BF16),
        scratch_shapes=[
            pltpu.VMEM((seq, key_w), BF16),
            pltpu.VMEM((seq, key_w), BF16),
            pltpu.VMEM((seq, key_w), BF16),
            pltpu.VMEM((seq, key_w), BF16),
            pltpu.VMEM((seq // GLA_CHUNK, key_w), F32),
            pltpu.VMEM((GLA_HEADS, dv, dk), F32),
        ],
        compiler_params=_params(("arbitrary",)),
        name="gla",
    )(proj, proj, proj, proj, lr, w2p, gate_b, norm_g)


def _moba_kernel(q_ref, k_ref, v_ref, o_ref, vt_ref, s_ref, pt_ref, *, heads):
    bs = MOBA_BLOCK
    hd = MOBA_HEAD_DIM
    seq = k_ref.shape[0]
    nb = seq // bs
    scale = hd ** -0.5
    hs = range(heads)
    cols = [slice(h * hd, (h + 1) * hd) for h in hs]

    krow = lax.broadcasted_iota(jnp.int32, (bs, bs), 0)
    qcol = lax.broadcasted_iota(jnp.int32, (bs, bs), 1)
    own_bias = jnp.where(krow <= qcol, 0.0, -jnp.inf)
    ones_row = (lax.broadcasted_iota(jnp.int32, (BF16_SUBLANES, seq), 0) == 0).astype(F32).astype(BF16)

    def select(h):
        km = jnp.mean(k_ref[:, cols[h]].astype(F32).reshape(nb, bs, hd), axis=1)
        km = jnp.concatenate([km, jnp.zeros((BF16_SUBLANES - nb, hd), F32)], axis=0).astype(BF16)
        sc = lax.dot_general(km, q_ref[:, cols[h]], _NT, preferred_element_type=F32)[0:nb, :]
        blk = lax.broadcasted_iota(jnp.int32, sc.shape, 0)
        qpos = lax.broadcasted_iota(jnp.int32, sc.shape, 1)
        past = (blk + 1) * bs <= qpos
        sc = jnp.where(past, sc, -jnp.inf)
        beaten = jnp.zeros(sc.shape, F32)
        for jp in range(nb):
            c = sc[jp:jp + 1, :]
            ahead = (c > sc) | ((c == sc) & (blk > jp))
            beaten = beaten + ahead.astype(F32)
        vt_ref[h, 0:hd, :] = v_ref[:, cols[h]].astype(F32).T.astype(BF16)
        vt_ref[h, hd:hd + BF16_SUBLANES, :] = ones_row
        return jnp.where(past & (beaten < MOBA_TOPK), 0.0, -jnp.inf)

    sel_bias = [select(h) for h in hs]

    def scores(h, i):
        nk = (i + 1) * bs
        s_ref[h, i % 2, 0:nk, :] = lax.dot_general(
            k_ref[0:nk, cols[h]], q_ref[i * bs:nk, cols[h]], _NT, preferred_element_type=F32)

    def softmax(h, i):
        slot = i % 2
        qs = slice(i * bs, (i + 1) * bs)
        nk = (i + 1) * bs
        biases = [sel_bias[h][j:j + 1, qs] for j in range(i)]
        m = (s_ref[h, slot, i * bs:nk, :] + own_bias).max(axis=0, keepdims=True)
        for j in range(i):
            m = jnp.maximum(m, s_ref[h, slot, j * bs:(j + 1) * bs, :].max(axis=0, keepdims=True) + biases[j])
        own = s_ref[h, slot, i * bs:nk, :] + own_bias
        pt_ref[h, slot, i * bs:nk, :] = jnp.exp2((own - m) * (scale * LOG2E)).astype(BF16)
        for j in range(i):
            p = jnp.exp2((s_ref[h, slot, j * bs:(j + 1) * bs, :] - (m - biases[j])) * (scale * LOG2E))
            pt_ref[h, slot, j * bs:(j + 1) * bs, :] = p.astype(BF16)

    def attend(h, i):
        nk = (i + 1) * bs
        ot = jnp.dot(vt_ref[h, :, 0:nk], pt_ref[h, i % 2, 0:nk, :], preferred_element_type=F32)
        o_ref[i * bs:nk, cols[h]] = (ot[0:hd, :] / ot[hd:hd + 1, :]).T.astype(o_ref.dtype)

    for h in hs:
        scores(h, 0)
    for i in range(nb):
        for h in hs:
            if i + 1 < nb:
                scores(h, i + 1)
        for h in hs:
            if i > 0:
                attend(h, i - 1)
        for h in hs:
            softmax(h, i)
    for h in hs:
        attend(h, nb - 1)


def _moba(proj, *, batch, seq, col0, width, heads_per_step=2):
    n = proj.shape[0]
    hd = MOBA_HEAD_DIM
    wb = heads_per_step * hd
    groups = width // wb
    qb = col0 // wb
    kb = qb + groups
    vb = kb + groups
    kern = functools.partial(_moba_kernel, heads=heads_per_step)
    return pl.pallas_call(
        kern,
        grid=(batch, groups),
        in_specs=[
            pl.BlockSpec((seq, wb), lambda b, h: (b, qb + h)),
            pl.BlockSpec((seq, wb), lambda b, h: (b, kb + h)),
            pl.BlockSpec((seq, wb), lambda b, h: (b, vb + h)),
        ],
        out_specs=pl.BlockSpec((seq, wb), lambda b, h: (b, h)),
        out_shape=jax.ShapeDtypeStruct((n, width), BF16),
        scratch_shapes=[
            pltpu.VMEM((heads_per_step, hd + BF16_SUBLANES, seq), BF16),
            pltpu.VMEM((heads_per_step, 2, seq, MOBA_BLOCK), F32),
            pltpu.VMEM((heads_per_step, 2, seq, MOBA_BLOCK), BF16),
        ],
        compiler_params=_params(("arbitrary", "arbitrary")),
        name="moba",
    )(proj, proj, proj)


def _merge_kernel(og_ref, om_ref, wg_ref, wm_ref, g1_ref, g2_ref, wo_ref, x_ref, gt_ref, o_ref):
    yg = jnp.dot(og_ref[...], wg_ref[...], preferred_element_type=F32)
    ym = jnp.dot(om_ref[...], wm_ref[...], preferred_element_type=F32)
    z = jax.nn.sigmoid(g1_ref[...].astype(F32)) * yg + jax.nn.sigmoid(g2_ref[...].astype(F32)) * ym
    y = jnp.dot(z.astype(BF16), wo_ref[...], preferred_element_type=F32)
    o_ref[...] = x_ref[...] + gt_ref[...] * y


def _merge(o_gla, o_moba, w_up_gla, w_up_moba, proj, gate_col0, w_out, x2, mod, layer, *, seq, tm=512):
    n, d = x2.shape
    per_b = seq // tm
    g1b = gate_col0 // d
    kg = o_gla.shape[1]
    km = o_moba.shape[1]
    resident = dict(pipeline_mode=pl.Buffered(1))
    return pl.pallas_call(
        _merge_kernel,
        grid=(n // tm,),
        in_specs=[
            pl.BlockSpec((tm, kg), lambda i: (i, 0)),
            pl.BlockSpec((tm, km), lambda i: (i, 0)),
            pl.BlockSpec((None, kg, d), lambda i: (layer, 0, 0), **resident),
            pl.BlockSpec((None, km, d), lambda i: (layer, 0, 0), **resident),
            pl.BlockSpec((tm, d), lambda i: (i, g1b)),
            pl.BlockSpec((tm, d), lambda i: (i, g1b + 1)),
            pl.BlockSpec((None, d, d), lambda i: (layer, 0, 0), **resident),
            pl.BlockSpec((tm, d), lambda i: (i, 0)),
            pl.BlockSpec((None, None, 1, d), lambda i: (i // per_b, 2, 0, 0)),
        ],
        out_specs=pl.BlockSpec((tm, d), lambda i: (i, 0)),
        out_shape=jax.ShapeDtypeStruct((n, d), F32),
        compiler_params=_params(("arbitrary",)),
        name="merge",
    )(o_gla, o_moba, w_up_gla, w_up_moba, proj, proj, w_out, x2, mod)


def _ffn_kernel(x_ref, g_ref, sc_ref, sh_ref, gt_ref, wa_ref, wu_ref, wo_ref, fg_ref, o_ref, h_ref, *, final_norm):
    j = pl.program_id(1)

    def part(h):
        a = jnp.dot(h, wa_ref[...], preferred_element_type=F32)
        u = jnp.dot(h, wu_ref[...], preferred_element_type=F32)
        return jnp.dot((jax.nn.silu(a) * u).astype(BF16), wo_ref[...], preferred_element_type=F32)

    @pl.when(j == 0)
    def _():
        h = _mod_norm(x_ref[...], g_ref[...], sc_ref[...], sh_ref[...]).astype(BF16)
        h_ref[...] = h
        o_ref[...] = part(h)

    @pl.when(j > 0)
    def _():
        o_ref[...] += part(h_ref[...])

    @pl.when(j == pl.num_programs(1) - 1)
    def _():
        y = x_ref[...] + gt_ref[...] * o_ref[...]
        if final_norm:
            ms = jnp.mean(y * y, axis=-1, keepdims=True)
            y = y * lax.rsqrt(ms + EPS) * fg_ref[...]
        o_ref[...] = y


def _ffn(x2, g, mod, w_in, w_out, final_g, layer, *, seq, final_norm, tm=1024, tf=256):
    n, d = x2.shape
    dff = w_out.shape[1]
    per_b = seq // tm
    nf = dff // tf
    kern = functools.partial(_ffn_kernel, final_norm=final_norm)
    return pl.pallas_call(
        kern,
        grid=(n // tm, nf),
        in_specs=[
            pl.BlockSpec((tm, d), lambda i, j: (i, 0)),
            pl.BlockSpec((1, d), lambda i, j: (0, 0)),
            pl.BlockSpec((None, None, 1, d), lambda i, j: (i // per_b, 4, 0, 0)),
            pl.BlockSpec((None, None, 1, d), lambda i, j: (i // per_b, 3, 0, 0)),
            pl.BlockSpec((None, None, 1, d), lambda i, j: (i // per_b, 5, 0, 0)),
            pl.BlockSpec((None, d, tf), lambda i, j: (layer, 0, j)),
            pl.BlockSpec((None, d, tf), lambda i, j: (layer, 0, nf + j)),
            pl.BlockSpec((None, tf, d), lambda i, j: (layer, j, 0)),
            pl.BlockSpec((1, d), lambda i, j: (0, 0)),
        ],
        out_specs=pl.BlockSpec((tm, d), lambda i, j: (i, 0)),
        out_shape=jax.ShapeDtypeStruct((n, d), F32),
        scratch_shapes=[pltpu.VMEM((tm, d), BF16)],
        compiler_params=_params(("arbitrary", "arbitrary")),
        name="ffn",
    )(x2, g, mod, mod, mod, w_in, w_in, w_out, final_g)


def kernel(x, c, ada_w, ada_b, norm1_g, w_in, gla_gate_w2, gla_gate_b, gla_norm_g, w_up_gla, w_up_moba,
           w_out, norm2_g, w_ffn_in, w_ffn_out, final_g):
    batch, seq, d = x.shape
    depth = ada_w.shape[0]
    rank, key_w = gla_gate_w2.shape[1:]
    val_w = w_up_gla.shape[1]
    moba_w = w_up_moba.shape[1]
    n = batch * seq
    lr0 = 2 * key_w + 2 * val_w
    moba0 = lr0
    gate0 = moba0 + 3 * moba_w

    mod = _adaln(c, ada_w, ada_b).reshape(depth, batch, 6, 1, d)
    w_main, w_lr = _repack_w_in(jnp.swapaxes(w_in, 1, 2), lr0, rank)
    w_up_gla_b, w_up_moba_b, w_out_b = w_up_gla.astype(BF16), w_up_moba.astype(BF16), w_out.astype(BF16)
    w_ffn_in_b, w_ffn_out_b = w_ffn_in.astype(BF16), w_ffn_out.astype(BF16)
    x2 = x.reshape(n, d)
    for l in range(depth):
        w2p = jnp.pad(gla_gate_w2[l], ((0, LANES - rank), (0, 0))).astype(BF16)
        proj, lr = _in_proj(x2, norm1_g[l][None], mod[l], w_main, w_lr, l, seq=seq)
        o_gla = _gla(proj, lr, w2p, gla_gate_b[l][None], gla_norm_g[l][None],
                     batch=batch, seq=seq, key_w=key_w, val_w=val_w)
        o_moba = _moba(proj, batch=batch, seq=seq, col0=moba0, width=moba_w)
        x2 = _merge(o_gla, o_moba, w_up_gla_b, w_up_moba_b, proj, gate0, w_out_b, x2, mod[l], l, seq=seq)
        x2 = _ffn(x2, norm2_g[l][None], mod[l], w_ffn_in_b, w_ffn_out_b, final_g[None], l,
                  seq=seq, final_norm=(l == depth - 1))
    return x2.reshape(batch, seq, d)
```

```python
import functools

import jax
import jax.numpy as jnp
from jax import lax
from jax.experimental import pallas as pl
from jax.experimental.pallas import tpu as pltpu

F32 = jnp.float32
BF16 = jnp.bfloat16

GLA_HEADS = 4
GLA_GATE_NORM = 16.0
GLA_CHUNK = 64
MOBA_HEAD_DIM = 128
MOBA_BLOCK = 256
MOBA_TOPK = 3
EPS = 1e-6
LOG2E = 1.4426950408889634

V7X_VMEM_BYTES = 64 * 1024 * 1024
VMEM_LIMIT_BYTES = V7X_VMEM_BYTES - 8 * 1024 * 1024
FFN_VMEM_LIMIT_BYTES = V7X_VMEM_BYTES - 4 * 1024 * 1024
LANES = 128
BF16_SUBLANES = 16

_NT = (((1,), (1,)), ((), ()))


def _params(semantics, vmem_limit_bytes=VMEM_LIMIT_BYTES):
    return pltpu.CompilerParams(dimension_semantics=semantics, vmem_limit_bytes=vmem_limit_bytes)


def _mod_norm(x, g, sc, sh):
    ms = jnp.mean(x * x, axis=-1, keepdims=True)
    y = x * lax.rsqrt(ms + EPS) * g
    return y * (1.0 + sc) + sh


def _adaln_kernel(c_ref, w_ref, b_ref, o_ref):
    c_act = jax.nn.silu(c_ref[...])
    o_ref[...] = (
        jnp.dot(c_act.astype(BF16), w_ref[...].astype(BF16), preferred_element_type=F32) + b_ref[...]
    )


def _adaln(c, ada_w, ada_b, *, tn=1024):
    depth, d, n6 = ada_w.shape
    b = c.shape[0]
    return pl.pallas_call(
        _adaln_kernel,
        grid=(depth, n6 // tn),
        in_specs=[
            pl.BlockSpec((b, d), lambda l, j: (0, 0)),
            pl.BlockSpec((None, d, tn), lambda l, j: (l, 0, j)),
            pl.BlockSpec((None, 1, tn), lambda l, j: (l, 0, j)),
        ],
        out_specs=pl.BlockSpec((None, b, tn), lambda l, j: (l, 0, j)),
        out_shape=jax.ShapeDtypeStruct((depth, b, n6), F32),
        compiler_params=_params(("arbitrary", "arbitrary")),
        name="adaln",
    )(c, ada_w, ada_b.reshape(depth, 1, n6))


def _repack_kernel(a_ref, b_ref, o_ref, ol_ref, *, lr_tile, rank):
    t = pl.program_id(1)

    @pl.when(t < lr_tile)
    def _():
        o_ref[...] = a_ref[...].astype(BF16)

    @pl.when(t >= lr_tile)
    def _():
        o_ref[...] = jnp.concatenate([a_ref[rank:, :], b_ref[...]], axis=0).astype(BF16)

    @pl.when(t == lr_tile)
    def _():
        pad = jnp.zeros((ol_ref.shape[0] - rank, ol_ref.shape[1]), F32)
        ol_ref[...] = jnp.concatenate([a_ref[0:rank, :], pad], axis=0).astype(BF16)


def _repack_w_in(w_in_t, lr0, rank, *, tw=1024):
    depth, c, d = w_in_t.shape
    assert lr0 % tw == 0 and (c - rank) % tw == 0 and tw % rank == 0 and rank % BF16_SUBLANES == 0
    kern = functools.partial(_repack_kernel, lr_tile=lr0 // tw, rank=rank)
    return pl.pallas_call(
        kern,
        grid=(depth, (c - rank) // tw),
        in_specs=[
            pl.BlockSpec((None, tw, d), lambda l, t: (l, t, 0)),
            pl.BlockSpec((None, rank, d), lambda l, t: (l, (t + 1) * (tw // rank), 0)),
        ],
        out_specs=[
            pl.BlockSpec((None, tw, d), lambda l, t: (l, t, 0)),
            pl.BlockSpec((None, LANES, d), lambda l, t: (l, 0, 0)),
        ],
        out_shape=[
            jax.ShapeDtypeStruct((depth, c - rank, d), BF16),
            jax.ShapeDtypeStruct((depth, LANES, d), BF16),
        ],
        compiler_params=_params(("arbitrary", "arbitrary")),
        name="repack_w_in",
    )(w_in_t, w_in_t)


def _in_proj_kernel(x_ref, g_ref, sc_ref, sh_ref, w_ref, wl_ref, o_ref, ol_ref, h_ref):
    j = pl.program_id(1)

    def project(h):
        o_ref[...] = lax.dot_general(h, w_ref[...], _NT, preferred_element_type=F32).astype(o_ref.dtype)

    @pl.when(j == 0)
    def _():
        h = _mod_norm(x_ref[...], g_ref[...], sc_ref[...], sh_ref[...]).astype(BF16)
        h_ref[...] = h
        ol_ref[...] = lax.dot_general(h, wl_ref[...], _NT, preferred_element_type=F32).astype(ol_ref.dtype)
        project(h)

    @pl.when(j > 0)
    def _():
        project(h_ref[...])


def _in_proj(x2, g, mod, w_main, w_lr, layer, *, seq, tm=1024, tn=2048):
    n, d = x2.shape
    c = w_main.shape[1]
    per_b = seq // tm
    return pl.pallas_call(
        _in_proj_kernel,
        grid=(n // tm, c // tn),
        in_specs=[
            pl.BlockSpec((tm, d), lambda i, j: (i, 0)),
            pl.BlockSpec((1, d), lambda i, j: (0, 0)),
            pl.BlockSpec((None, None, 1, d), lambda i, j: (i // per_b, 1, 0, 0)),
            pl.BlockSpec((None, None, 1, d), lambda i, j: (i // per_b, 0, 0, 0)),
            pl.BlockSpec((None, tn, d), lambda i, j: (layer, j, 0)),
            pl.BlockSpec((None, LANES, d), lambda i, j: (layer, 0, 0)),
        ],
        out_specs=[
            pl.BlockSpec((tm, tn), lambda i, j: (i, j)),
            pl.BlockSpec((tm, LANES), lambda i, j: (i, 0)),
        ],
        out_shape=[
            jax.ShapeDtypeStruct((n, c), BF16),
            jax.ShapeDtypeStruct((n, LANES), BF16),
        ],
        scratch_shapes=[pltpu.VMEM((tm, d), BF16)],
        compiler_params=_params(("arbitrary", "arbitrary")),
        name="in_proj",
    )(x2, g, mod, mod, w_main, w_lr)


def _split3(x):
    hi = x.astype(BF16)
    r1 = x - hi.astype(F32)
    mid = r1.astype(BF16)
    lo = (r1 - mid.astype(F32)).astype(BF16)
    return hi, mid, lo


def _gla_kernel(q_ref, k_ref, v_ref, gr_ref, lr_ref, w2_ref, gb_ref, ng_ref, o_ref,
                qm_ref, km_ref, qe_ref, ke_ref, el_ref, st_ref, *, heads, dk, dv, gate_rows):
    seq = q_ref.shape[0]
    cs = GLA_CHUNK
    kw = heads * dk
    per_trip = gate_rows // cs
    row = lax.broadcasted_iota(jnp.int32, (cs, cs), 0)
    col = lax.broadcasted_iota(jnp.int32, (cs, cs), 1)
    causal = col <= row
    tril = causal.astype(BF16)
    scale = dk ** -0.5

    def decay(pi, carry):
        base = pl.multiple_of(pi * gate_rows, gate_rows)
        r = pl.ds(base, gate_rows)
        xg = jnp.dot(lr_ref[r, :], w2_ref[...], preferred_element_type=F32) + gb_ref[...]
        g = (jnp.minimum(xg, 0.0) - jnp.log1p(jnp.exp(-jnp.abs(xg)))) / GLA_GATE_NORM
        parts = jnp.concatenate(_split3(g), axis=1)
        q = q_ref[r, :].astype(F32) * scale
        k = k_ref[r, :].astype(F32)
        for c in range(per_trip):
            rows = slice(c * cs, (c + 1) * cs)
            out = pl.ds(base + c * cs, cs)
            bs = jnp.dot(tril, parts[rows, :], preferred_element_type=F32)
            b = bs[:, :kw] + bs[:, kw:2 * kw] + bs[:, 2 * kw:]
            b_last = b[cs - 1:cs, :]
            b_mid = b[cs // 2 - 1:cs // 2, :]
            qm_ref[out, :] = (q[rows, :] * jnp.exp(b - b_mid)).astype(BF16)
            km_ref[out, :] = (k[rows, :] * jnp.exp(b_mid - b)).astype(BF16)
            qe_ref[out, :] = (q[rows, :] * jnp.exp(b)).astype(BF16)
            ke_ref[out, :] = (k[rows, :] * jnp.exp(b_last - b)).astype(BF16)
            el_ref[pl.ds(pi * per_trip + c, 1), :] = jnp.exp(b_last)
        return carry

    lax.fori_loop(0, seq // gate_rows, decay, 0)
    st_ref[...] = jnp.zeros_like(st_ref)

    def chunk(ci, carry):
        r = pl.ds(pl.multiple_of(ci * cs, cs), cs)
        e_last = el_ref[pl.ds(ci, 1), :]
        hs = range(heads)
        ks = [slice(h * dk, (h + 1) * dk) for h in hs]
        vs = [slice(h * dv, (h + 1) * dv) for h in hs]
        attn = [lax.dot_general(qm_ref[r, ks[h]], km_ref[r, ks[h]], _NT, preferred_element_type=F32) for h in hs]
        v = [v_ref[r, vs[h]] for h in hs]
        st = [st_ref[h] for h in hs]
        o_inter = [lax.dot_general(qe_ref[r, ks[h]], st[h].astype(BF16), _NT, preferred_element_type=F32)
                   for h in hs]
        kv = [jnp.dot(v[h].astype(F32).T.astype(BF16), ke_ref[r, ks[h]], preferred_element_type=F32) for h in hs]
        attn = [jnp.where(causal, attn[h], 0.0).astype(BF16) for h in hs]
        o = [jnp.dot(attn[h], v[h], preferred_element_type=F32) + o_inter[h] for h in hs]
        for h in hs:
            st_ref[h] = st[h] * e_last[:, ks[h]] + kv[h]
        for h in hs:
            ms = jnp.mean(o[h] * o[h], axis=-1, keepdims=True)
            y = o[h] * lax.rsqrt(ms + EPS) * ng_ref[...]
            o_ref[r, vs[h]] = (y * jax.nn.silu(gr_ref[r, vs[h]].astype(F32))).astype(o_ref.dtype)
        return carry

    lax.fori_loop(0, seq // cs, chunk, 0, unroll=2)


def _gla(proj, lr, w2p, gate_b, norm_g, *, batch, seq, key_w, val_w, gate_rows=512):
    n = proj.shape[0]
    dk = key_w // GLA_HEADS
    dv = val_w // GLA_HEADS
    vb = 2 * key_w // val_w
    rb = vb + 1
    kern = functools.partial(_gla_kernel, heads=GLA_HEADS, dk=dk, dv=dv, gate_rows=gate_rows)
    return pl.pallas_call(
        kern,
        grid=(batch,),
        in_specs=[
            pl.BlockSpec((seq, key_w), lambda b: (b, 0)),
            pl.BlockSpec((seq, key_w), lambda b: (b, 1)),
            pl.BlockSpec((seq, val_w), lambda b: (b, vb)),
            pl.BlockSpec((seq, val_w), lambda b: (b, rb)),
            pl.BlockSpec((seq, LANES), lambda b: (b, 0)),
            pl.BlockSpec((LANES, key_w), lambda b: (0, 0)),
            pl.BlockSpec((1, key_w), lambda b: (0, 0)),
            pl.BlockSpec((1, dv), lambda b: (0, 0)),
        ],
        out_specs=pl.BlockSpec((seq, val_w), lambda b: (b, 0)),
        out_shape=jax.ShapeDtypeStruct((n, val_w), BF16),
        scratch_shapes=[
            pltpu.VMEM((seq, key_w), BF16),
            pltpu.VMEM((seq, key_w), BF16),
            pltpu.VMEM((seq, key_w), BF16),
            pltpu.VMEM((seq, key_w), BF16),
            pltpu.VMEM((seq // GLA_CHUNK, key_w), F32),
            pltpu.VMEM((GLA_HEADS, dv, dk), F32),
        ],
        compiler_params=_params(("arbitrary",)),
        name="gla",
    )(proj, proj, proj, proj, lr, w2p, gate_b, norm_g)


def _moba_kernel(q_ref, k_ref, v_ref, o_ref, vt_ref, s_ref, pt_ref, *, heads):
    bs = MOBA_BLOCK
    hd = MOBA_HEAD_DIM
    seq = k_ref.shape[0]
    nb = seq // bs
    scale = hd ** -0.5
    hs = range(heads)
    cols = [slice(h * hd, (h + 1) * hd) for h in hs]

    krow = lax.broadcasted_iota(jnp.int32, (bs, bs), 0)
    qcol = lax.broadcasted_iota(jnp.int32, (bs, bs), 1)
    own_bias = jnp.where(krow <= qcol, 0.0, -jnp.inf)
    ones_row = (lax.broadcasted_iota(jnp.int32, (BF16_SUBLANES, seq), 0) == 0).astype(F32).astype(BF16)

    def select(h):
        km = jnp.mean(k_ref[:, cols[h]].astype(F32).reshape(nb, bs, hd), axis=1)
        km = jnp.concatenate([km, jnp.zeros((BF16_SUBLANES - nb, hd), F32)], axis=0).astype(BF16)
        sc = lax.dot_general(km, q_ref[:, cols[h]], _NT, preferred_element_type=F32)[0:nb, :]
        blk = lax.broadcasted_iota(jnp.int32, sc.shape, 0)
        qpos = lax.broadcasted_iota(jnp.int32, sc.shape, 1)
        past = (blk + 1) * bs <= qpos
        sc = jnp.where(past, sc, -jnp.inf)
        beaten = jnp.zeros(sc.shape, F32)
        for jp in range(nb):
            c = sc[jp:jp + 1, :]
            ahead = (c > sc) | ((c == sc) & (blk > jp))
            beaten = beaten + ahead.astype(F32)
        vt_ref[h, 0:hd, :] = v_ref[:, cols[h]].astype(F32).T.astype(BF16)
        vt_ref[h, hd:hd + BF16_SUBLANES, :] = ones_row
        return jnp.where(past & (beaten < MOBA_TOPK), 0.0, -jnp.inf)

    sel_bias = [select(h) for h in hs]

    def scores(h, i):
        nk = (i + 1) * bs
        s_ref[h, i % 2, 0:nk, :] = lax.dot_general(
            k_ref[0:nk, cols[h]], q_ref[i * bs:nk, cols[h]], _NT, preferred_element_type=F32)

    def softmax(h, i):
        slot = i % 2
        qs = slice(i * bs, (i + 1) * bs)
        nk = (i + 1) * bs
        biases = [sel_bias[h][j:j + 1, qs] for j in range(i)]
        m = (s_ref[h, slot, i * bs:nk, :] + own_bias).max(axis=0, keepdims=True)
        for j in range(i):
            m = jnp.maximum(m, s_ref[h, slot, j * bs:(j + 1) * bs, :].max(axis=0, keepdims=True) + biases[j])
        own = s_ref[h, slot, i * bs:nk, :] + own_bias
        pt_ref[h, slot, i * bs:nk, :] = jnp.exp2((own - m) * (scale * LOG2E)).astype(BF16)
        for j in range(i):
            p = jnp.exp2((s_ref[h, slot, j * bs:(j + 1) * bs, :] - (m - biases[j])) * (scale * LOG2E))
            pt_ref[h, slot, j * bs:(j + 1) * bs, :] = p.astype(BF16)

    def attend(h, i):
        nk = (i + 1) * bs
        ot = jnp.dot(vt_ref[h, :, 0:nk], pt_ref[h, i % 2, 0:nk, :], preferred_element_type=F32)
        o_ref[i * bs:nk, cols[h]] = (ot[0:hd, :] / ot[hd:hd + 1, :]).T.astype(o_ref.dtype)

    for h in hs:
        scores(h, 0)
    for i in range(nb):
        for h in hs:
            if i + 1 < nb:
                scores(h, i + 1)
        for h in hs:
            if i > 0:
                attend(h, i - 1)
        for h in hs:
            softmax(h, i)
    for h in hs:
        attend(h, nb - 1)


def _moba(proj, *, batch, seq, col0, width, heads_per_step=2):
    n = proj.shape[0]
    hd = MOBA_HEAD_DIM
    wb = heads_per_step * hd
    groups = width // wb
    qb = col0 // wb
    kb = qb + groups
    vb = kb + groups
    kern = functools.partial(_moba_kernel, heads=heads_per_step)
    return pl.pallas_call(
        kern,
        grid=(batch, groups),
        in_specs=[
            pl.BlockSpec((seq, wb), lambda b, h: (b, qb + h)),
            pl.BlockSpec((seq, wb), lambda b, h: (b, kb + h)),
            pl.BlockSpec((seq, wb), lambda b, h: (b, vb + h)),
        ],
        out_specs=pl.BlockSpec((seq, wb), lambda b, h: (b, h)),
        out_shape=jax.ShapeDtypeStruct((n, width), BF16),
        scratch_shapes=[
            pltpu.VMEM((heads_per_step, hd + BF16_SUBLANES, seq), BF16),
            pltpu.VMEM((heads_per_step, 2, seq, MOBA_BLOCK), F32),
            pltpu.VMEM((heads_per_step, 2, seq, MOBA_BLOCK), BF16),
        ],
        compiler_params=_params(("arbitrary", "arbitrary")),
        name="moba",
    )(proj, proj, proj)


def _merge_kernel(og_ref, om_ref, wg_ref, wm_ref, g1_ref, g2_ref, wo_ref, x_ref, gt_ref, o_ref):
    yg = jnp.dot(og_ref[...], wg_ref[...], preferred_element_type=F32)
    ym = jnp.dot(om_ref[...], wm_ref[...], preferred_element_type=F32)
    z = jax.nn.sigmoid(g1_ref[...].astype(F32)) * yg + jax.nn.sigmoid(g2_ref[...].astype(F32)) * ym
    y = jnp.dot(z.astype(BF16), wo_ref[...], preferred_element_type=F32)
    o_ref[...] = x_ref[...] + gt_ref[...] * y


def _merge(o_gla, o_moba, w_up_gla, w_up_moba, proj, gate_col0, w_out, x2, mod, layer, *, seq, tm=512):
    n, d = x2.shape
    per_b = seq // tm
    g1b = gate_col0 // d
    kg = o_gla.shape[1]
    km = o_moba.shape[1]
    resident = dict(pipeline_mode=pl.Buffered(1))
    return pl.pallas_call(
        _merge_kernel,
        grid=(n // tm,),
        in_specs=[
            pl.BlockSpec((tm, kg), lambda i: (i, 0)),
            pl.BlockSpec((tm, km), lambda i: (i, 0)),
            pl.BlockSpec((None, kg, d), lambda i: (layer, 0, 0), **resident),
            pl.BlockSpec((None, km, d), lambda i: (layer, 0, 0), **resident),
            pl.BlockSpec((tm, d), lambda i: (i, g1b)),
            pl.BlockSpec((tm, d), lambda i: (i, g1b + 1)),
            pl.BlockSpec((None, d, d), lambda i: (layer, 0, 0), **resident),
            pl.BlockSpec((tm, d), lambda i: (i, 0)),
            pl.BlockSpec((None, None, 1, d), lambda i: (i // per_b, 2, 0, 0)),
        ],
        out_specs=pl.BlockSpec((tm, d), lambda i: (i, 0)),
        out_shape=jax.ShapeDtypeStruct((n, d), F32),
        compiler_params=_params(("arbitrary",)),
        name="merge",
    )(o_gla, o_moba, w_up_gla, w_up_moba, proj, proj, w_out, x2, mod)


def _ffn_kernel(x_ref, g_ref, sc_ref, sh_ref, gt_ref, wa_ref, wu_ref, wo_ref, fg_ref, o_ref, h_ref, *, final_norm):
    j = pl.program_id(1)

    def part(h):
        a = jnp.dot(h, wa_ref[...], preferred_element_type=F32)
        u = jnp.dot(h, wu_ref[...], preferred_element_type=F32)
        return jnp.dot((jax.nn.silu(a) * u).astype(BF16), wo_ref[...], preferred_element_type=F32)

    @pl.when(j == 0)
    def _():
        h = _mod_norm(x_ref[...], g_ref[...], sc_ref[...], sh_ref[...]).astype(BF16)
        h_ref[...] = h
        o_ref[...] = part(h)

    @pl.when(j > 0)
    def _():
        o_ref[...] += part(h_ref[...])

    @pl.when(j == pl.num_programs(1) - 1)
    def _():
        y = x_ref[...] + gt_ref[...] * o_ref[...]
        if final_norm:
            ms = jnp.mean(y * y, axis=-1, keepdims=True)
            y = y * lax.rsqrt(ms + EPS) * fg_ref[...]
        o_ref[...] = y


def _ffn(x2, g, mod, w_in, w_out, final_g, layer, *, seq, final_norm, tm=1024, tf=512):
    n, d = x2.shape
    dff = w_out.shape[1]
    per_b = seq // tm
    nf = dff // tf
    kern = functools.partial(_ffn_kernel, final_norm=final_norm)
    return pl.pallas_call(
        kern,
        grid=(n // tm, nf),
        in_specs=[
            pl.BlockSpec((tm, d), lambda i, j: (i, 0)),
            pl.BlockSpec((1, d), lambda i, j: (0, 0)),
            pl.BlockSpec((None, None, 1, d), lambda i, j: (i // per_b, 4, 0, 0)),
            pl.BlockSpec((None, None, 1, d), lambda i, j: (i // per_b, 3, 0, 0)),
            pl.BlockSpec((None, None, 1, d), lambda i, j: (i // per_b, 5, 0, 0)),
            pl.BlockSpec((None, d, tf), lambda i, j: (layer, 0, j)),
            pl.BlockSpec((None, d, tf), lambda i, j: (layer, 0, nf + j)),
            pl.BlockSpec((None, tf, d), lambda i, j: (layer, j, 0)),
            pl.BlockSpec((1, d), lambda i, j: (0, 0)),
        ],
        out_specs=pl.BlockSpec((tm, d), lambda i, j: (i, 0)),
        out_shape=jax.ShapeDtypeStruct((n, d), F32),
        scratch_shapes=[pltpu.VMEM((tm, d), BF16)],
        compiler_params=_params(("arbitrary", "arbitrary"), FFN_VMEM_LIMIT_BYTES),
        name="ffn",
    )(x2, g, mod, mod, mod, w_in, w_in, w_out, final_g)


def kernel(x, c, ada_w, ada_b, norm1_g, w_in, gla_gate_w2, gla_gate_b, gla_norm_g, w_up_gla, w_up_moba,
           w_out, norm2_g, w_ffn_in, w_ffn_out, final_g):
    batch, seq, d = x.shape
    depth = ada_w.shape[0]
    rank, key_w = gla_gate_w2.shape[1:]
    val_w = w_up_gla.shape[1]
    moba_w = w_up_moba.shape[1]
    n = batch * seq
    lr0 = 2 * key_w + 2 * val_w
    moba0 = lr0
    gate0 = moba0 + 3 * moba_w

    mod = _adaln(c, ada_w, ada_b).reshape(depth, batch, 6, 1, d)
    w_main, w_lr = _repack_w_in(jnp.swapaxes(w_in, 1, 2), lr0, rank)
    w_up_gla_b, w_up_moba_b, w_out_b = w_up_gla.astype(BF16), w_up_moba.astype(BF16), w_out.astype(BF16)
    w_ffn_in_b, w_ffn_out_b = w_ffn_in.astype(BF16), w_ffn_out.astype(BF16)
    x2 = x.reshape(n, d)
    for l in range(depth):
        w2p = jnp.pad(gla_gate_w2[l], ((0, LANES - rank), (0, 0))).astype(BF16)
        proj, lr = _in_proj(x2, norm1_g[l][None], mod[l], w_main, w_lr, l, seq=seq)
        o_gla = _gla(proj, lr, w2p, gla_gate_b[l][None], gla_norm_g[l][None],
                     batch=batch, seq=seq, key_w=key_w, val_w=val_w)
        o_moba = _moba(proj, batch=batch, seq=seq, col0=moba0, width=moba_w)
        x2 = _merge(o_gla, o_moba, w_up_gla_b, w_up_moba_b, proj, gate0, w_out_b, x2, mod[l], l, seq=seq)
        x2 = _ffn(x2, norm2_g[l][None], mod[l], w_ffn_in_b, w_ffn_out_b, final_g[None], l,
                  seq=seq, final_norm=(l == depth - 1))
    return x2.reshape(batch, seq, d)
```

```python
import functools

import jax
import jax.numpy as jnp
from jax import lax
from jax.experimental import pallas as pl
from jax.experimental.pallas import tpu as pltpu

F32 = jnp.float32
BF16 = jnp.bfloat16

GLA_HEADS = 4
GLA_GATE_NORM = 16.0
GLA_CHUNK = 64
MOBA_HEAD_DIM = 128
MOBA_BLOCK = 256
MOBA_TOPK = 3
EPS = 1e-6
LOG2E = 1.4426950408889634

V7X_VMEM_BYTES = 64 * 1024 * 1024
VMEM_LIMIT_BYTES = V7X_VMEM_BYTES - 8 * 1024 * 1024
BIG_VMEM_LIMIT_BYTES = V7X_VMEM_BYTES - 4 * 1024 * 1024
LANES = 128
BF16_SUBLANES = 16

_NT = (((1,), (1,)), ((), ()))
_TN = (((0,), (0,)), ((), ()))


def _params(semantics, vmem_limit_bytes=VMEM_LIMIT_BYTES):
    return pltpu.CompilerParams(dimension_semantics=semantics, vmem_limit_bytes=vmem_limit_bytes)


def _mod_norm(x, g, sc, sh):
    ms = jnp.mean(x * x, axis=-1, keepdims=True)
    y = x * lax.rsqrt(ms + EPS) * g
    return y * (1.0 + sc) + sh


def _adaln_kernel(c_ref, w_ref, b_ref, o_ref):
    c_act = jax.nn.silu(c_ref[...])
    o_ref[...] = (
        jnp.dot(c_act.astype(BF16), w_ref[...].astype(BF16), preferred_element_type=F32) + b_ref[...]
    )


def _adaln(c, ada_w, ada_b, *, tn=1024):
    depth, d, n6 = ada_w.shape
    b = c.shape[0]
    return pl.pallas_call(
        _adaln_kernel,
        grid=(depth, n6 // tn),
        in_specs=[
            pl.BlockSpec((b, d), lambda l, j: (0, 0)),
            pl.BlockSpec((None, d, tn), lambda l, j: (l, 0, j)),
            pl.BlockSpec((None, 1, tn), lambda l, j: (l, 0, j)),
        ],
        out_specs=pl.BlockSpec((None, b, tn), lambda l, j: (l, 0, j)),
        out_shape=jax.ShapeDtypeStruct((depth, b, n6), F32),
        compiler_params=_params(("arbitrary", "arbitrary")),
        name="adaln",
    )(c, ada_w, ada_b.reshape(depth, 1, n6))


def _repack_kernel(a_ref, b_ref, o_ref, ol_ref, *, lr_tile, rank):
    t = pl.program_id(1)

    @pl.when(t < lr_tile)
    def _():
        o_ref[...] = a_ref[...].astype(BF16)

    @pl.when(t >= lr_tile)
    def _():
        o_ref[...] = jnp.concatenate([a_ref[rank:, :], b_ref[...]], axis=0).astype(BF16)

    @pl.when(t == lr_tile)
    def _():
        pad = jnp.zeros((ol_ref.shape[0] - rank, ol_ref.shape[1]), F32)
        ol_ref[...] = jnp.concatenate([a_ref[0:rank, :], pad], axis=0).astype(BF16)


def _repack_w_in(w_in_t, lr0, rank, *, tw=1024):
    depth, c, d = w_in_t.shape
    assert lr0 % tw == 0 and (c - rank) % tw == 0 and tw % rank == 0 and rank % BF16_SUBLANES == 0
    kern = functools.partial(_repack_kernel, lr_tile=lr0 // tw, rank=rank)
    return pl.pallas_call(
        kern,
        grid=(depth, (c - rank) // tw),
        in_specs=[
            pl.BlockSpec((None, tw, d), lambda l, t: (l, t, 0)),
            pl.BlockSpec((None, rank, d), lambda l, t: (l, (t + 1) * (tw // rank), 0)),
        ],
        out_specs=[
            pl.BlockSpec((None, tw, d), lambda l, t: (l, t, 0)),
            pl.BlockSpec((None, LANES, d), lambda l, t: (l, 0, 0)),
        ],
        out_shape=[
            jax.ShapeDtypeStruct((depth, c - rank, d), BF16),
            jax.ShapeDtypeStruct((depth, LANES, d), BF16),
        ],
        compiler_params=_params(("arbitrary", "arbitrary")),
        name="repack_w_in",
    )(w_in_t, w_in_t)


def _in_proj_kernel(x_ref, g_ref, sc_ref, sh_ref, w_ref, wl_ref, *rest, n_cast):
    cast_in = rest[:n_cast]
    o_ref, ol_ref = rest[n_cast:n_cast + 2]
    cast_out = rest[n_cast + 2:2 * n_cast + 2]
    h_ref = rest[-1]
    j = pl.program_id(1)

    def project(h):
        o_ref[...] = lax.dot_general(h, w_ref[...], _NT, preferred_element_type=F32).astype(o_ref.dtype)
        for src, dst in zip(cast_in, cast_out):
            dst[...] = src[...].astype(BF16)

    @pl.when(j == 0)
    def _():
        h = _mod_norm(x_ref[...], g_ref[...], sc_ref[...], sh_ref[...]).astype(BF16)
        h_ref[...] = h
        ol_ref[...] = lax.dot_general(h, wl_ref[...], _NT, preferred_element_type=F32).astype(ol_ref.dtype)
        project(h)

    @pl.when(j > 0)
    def _():
        project(h_ref[...])


def _in_proj(x2, g, mod, w_main, w_lr, layer, casts, *, seq, tm=1024, tn=2048):
    n, d = x2.shape
    c = w_main.shape[1]
    per_b = seq // tm
    ncol = c // tn
    steps = (n // tm) * ncol
    cast_in_specs, cast_out_specs, cast_shapes = [], [], []
    for w, rows in casts:
        nblk = w.shape[1] // rows
        assert w.shape[1] % rows == 0 and rows % BF16_SUBLANES == 0 and nblk <= steps

        def block(i, j, nblk=nblk):
            return jnp.minimum(i * ncol + j, nblk - 1)

        cast_in_specs.append(pl.BlockSpec((None, rows, w.shape[2]), lambda i, j, b=block: (layer, b(i, j), 0)))
        cast_out_specs.append(pl.BlockSpec((None, rows, w.shape[2]), lambda i, j, b=block: (0, b(i, j), 0)))
        cast_shapes.append(jax.ShapeDtypeStruct((1,) + w.shape[1:], BF16))
    kern = functools.partial(_in_proj_kernel, n_cast=len(casts))
    outs = pl.pallas_call(
        kern,
        grid=(n // tm, ncol),
        in_specs=[
            pl.BlockSpec((tm, d), lambda i, j: (i, 0)),
            pl.BlockSpec((1, d), lambda i, j: (0, 0)),
            pl.BlockSpec((None, None, 1, d), lambda i, j: (i // per_b, 1, 0, 0)),
            pl.BlockSpec((None, None, 1, d), lambda i, j: (i // per_b, 0, 0, 0)),
            pl.BlockSpec((None, tn, d), lambda i, j: (layer, j, 0)),
            pl.BlockSpec((None, LANES, d), lambda i, j: (layer, 0, 0)),
        ] + cast_in_specs,
        out_specs=[
            pl.BlockSpec((tm, tn), lambda i, j: (i, j)),
            pl.BlockSpec((tm, LANES), lambda i, j: (i, 0)),
        ] + cast_out_specs,
        out_shape=[
            jax.ShapeDtypeStruct((n, c), BF16),
            jax.ShapeDtypeStruct((n, LANES), BF16),
        ] + cast_shapes,
        scratch_shapes=[pltpu.VMEM((tm, d), BF16)],
        compiler_params=_params(("arbitrary", "arbitrary"), BIG_VMEM_LIMIT_BYTES),
        name="in_proj",
    )(x2, g, mod, mod, w_main, w_lr, *[w for w, _ in casts])
    return outs[0], outs[1], outs[2:]


def _split3(x):
    hi = x.astype(BF16)
    r1 = x - hi.astype(F32)
    mid = r1.astype(BF16)
    lo = (r1 - mid.astype(F32)).astype(BF16)
    return hi, mid, lo


def _gla_kernel(q_ref, k_ref, v_ref, gr_ref, lr_ref, w2_ref, gb_ref, ng_ref, o_ref,
                qm_ref, km_ref, qe_ref, ke_ref, el_ref, st_ref, *, heads, dk, dv, gate_rows):
    seq = q_ref.shape[0]
    cs = GLA_CHUNK
    kw = heads * dk
    per_trip = gate_rows // cs
    row = lax.broadcasted_iota(jnp.int32, (cs, cs), 0)
    col = lax.broadcasted_iota(jnp.int32, (cs, cs), 1)
    causal = col <= row
    tril = causal.astype(BF16)
    scale = dk ** -0.5

    def decay(pi, carry):
        base = pl.multiple_of(pi * gate_rows, gate_rows)
        r = pl.ds(base, gate_rows)
        xg = jnp.dot(lr_ref[r, :], w2_ref[...], preferred_element_type=F32) + gb_ref[...]
        g = (jnp.minimum(xg, 0.0) - jnp.log1p(jnp.exp(-jnp.abs(xg)))) / GLA_GATE_NORM
        parts = jnp.concatenate(_split3(g), axis=1)
        q = q_ref[r, :].astype(F32) * scale
        k = k_ref[r, :].astype(F32)
        for c in range(per_trip):
            rows = slice(c * cs, (c + 1) * cs)
            out = pl.ds(base + c * cs, cs)
            bs = jnp.dot(tril, parts[rows, :], preferred_element_type=F32)
            b = bs[:, :kw] + bs[:, kw:2 * kw] + bs[:, 2 * kw:]
            b_last = b[cs - 1:cs, :]
            b_mid = b[cs // 2 - 1:cs // 2, :]
            qm_ref[out, :] = (q[rows, :] * jnp.exp(b - b_mid)).astype(BF16)
            km_ref[out, :] = (k[rows, :] * jnp.exp(b_mid - b)).astype(BF16)
            qe_ref[out, :] = (q[rows, :] * jnp.exp(b)).astype(BF16)
            ke_ref[out, :] = (k[rows, :] * jnp.exp(b_last - b)).astype(BF16)
            el_ref[pl.ds(pi * per_trip + c, 1), :] = jnp.exp(b_last)
        return carry

    lax.fori_loop(0, seq // gate_rows, decay, 0)
    st_ref[...] = jnp.zeros_like(st_ref)

    def chunk(ci, carry):
        r = pl.ds(pl.multiple_of(ci * cs, cs), cs)
        e_last = el_ref[pl.ds(ci, 1), :]
        hs = range(heads)
        ks = [slice(h * dk, (h + 1) * dk) for h in hs]
        vs = [slice(h * dv, (h + 1) * dv) for h in hs]
        attn = [lax.dot_general(qm_ref[r, ks[h]], km_ref[r, ks[h]], _NT, preferred_element_type=F32) for h in hs]
        v = [v_ref[r, vs[h]] for h in hs]
        st = [st_ref[h] for h in hs]
        o_inter = [lax.dot_general(qe_ref[r, ks[h]], st[h].astype(BF16), _NT, preferred_element_type=F32)
                   for h in hs]
        kv = [lax.dot_general(v[h], ke_ref[r, ks[h]], _TN, preferred_element_type=F32) for h in hs]
        attn = [jnp.where(causal, attn[h], 0.0).astype(BF16) for h in hs]
        o = [jnp.dot(attn[h], v[h], preferred_element_type=F32) + o_inter[h] for h in hs]
        for h in hs:
            st_ref[h] = st[h] * e_last[:, ks[h]] + kv[h]
        for h in hs:
            ms = jnp.mean(o[h] * o[h], axis=-1, keepdims=True)
            y = o[h] * lax.rsqrt(ms + EPS) * ng_ref[...]
            o_ref[r, vs[h]] = (y * jax.nn.silu(gr_ref[r, vs[h]].astype(F32))).astype(o_ref.dtype)
        return carry

    lax.fori_loop(0, seq // cs, chunk, 0, unroll=2)


def _gla(proj, lr, w2p, gate_b, norm_g, *, batch, seq, key_w, val_w, gate_rows=512):
    n = proj.shape[0]
    dk = key_w // GLA_HEADS
    dv = val_w // GLA_HEADS
    vb = 2 * key_w // val_w
    rb = vb + 1
    kern = functools.partial(_gla_kernel, heads=GLA_HEADS, dk=dk, dv=dv, gate_rows=gate_rows)
    return pl.pallas_call(
        kern,
        grid=(batch,),
        in_specs=[
            pl.BlockSpec((seq, key_w), lambda b: (b, 0)),
            pl.BlockSpec((seq, key_w), lambda b: (b, 1)),
            pl.BlockSpec((seq, val_w), lambda b: (b, vb)),
            pl.BlockSpec((seq, val_w), lambda b: (b, rb)),
            pl.BlockSpec((seq, LANES), lambda b: (b, 0)),
            pl.BlockSpec((LANES, key_w), lambda b: (0, 0)),
            pl.BlockSpec((1, key_w), lambda b: (0, 0)),
            pl.BlockSpec((1, dv), lambda b: (0, 0)),
        ],
        out_specs=pl.BlockSpec((seq, val_w), lambda b: (b, 0)),
        out_shape=jax.ShapeDtypeStruct((n, val_w), BF16),
        scratch_shapes=[
            pltpu.VMEM((seq, key_w), BF16),
            pltpu.VMEM((seq, key_w), BF16),
            pltpu.VMEM((seq, key_w), BF16),
            pltpu.VMEM((seq, key_w), BF16),
            pltpu.VMEM((seq // GLA_CHUNK, key_w), F32),
            pltpu.VMEM((GLA_HEADS, dv, dk), F32),
        ],
        compiler_params=_params(("arbitrary",)),
        name="gla",
    )(proj, proj, proj, proj, lr, w2p, gate_b, norm_g)


def _moba_kernel(q_ref, k_ref, v_ref, o_ref, vt_ref, s_ref, pt_ref, *, heads):
    bs = MOBA_BLOCK
    hd = MOBA_HEAD_DIM
    seq = k_ref.shape[0]
    nb = seq // bs
    scale = hd ** -0.5
    hs = range(heads)
    cols = [slice(h * hd, (h + 1) * hd) for h in hs]

    krow = lax.broadcasted_iota(jnp.int32, (bs, bs), 0)
    qcol = lax.broadcasted_iota(jnp.int32, (bs, bs), 1)
    own_bias = jnp.where(krow <= qcol, 0.0, -jnp.inf)
    ones_row = (lax.broadcasted_iota(jnp.int32, (BF16_SUBLANES, seq), 0) == 0).astype(F32).astype(BF16)

    def select(h):
        km = jnp.mean(k_ref[:, cols[h]].astype(F32).reshape(nb, bs, hd), axis=1)
        km = jnp.concatenate([km, jnp.zeros((BF16_SUBLANES - nb, hd), F32)], axis=0).astype(BF16)
        sc = lax.dot_general(km, q_ref[:, cols[h]], _NT, preferred_element_type=F32)[0:nb, :]
        blk = lax.broadcasted_iota(jnp.int32, sc.shape, 0)
        qpos = lax.broadcasted_iota(jnp.int32, sc.shape, 1)
        past = (blk + 1) * bs <= qpos
        sc = jnp.where(past, sc, -jnp.inf)
        beaten = jnp.zeros(sc.shape, F32)
        for jp in range(nb):
            c = sc[jp:jp + 1, :]
            ahead = (c > sc) | ((c == sc) & (blk > jp))
            beaten = beaten + ahead.astype(F32)
        vt_ref[h, 0:hd, :] = v_ref[:, cols[h]].astype(F32).T.astype(BF16)
        vt_ref[h, hd:hd + BF16_SUBLANES, :] = ones_row
        return jnp.where(past & (beaten < MOBA_TOPK), 0.0, -jnp.inf)

    sel_bias = [select(h) for h in hs]

    def scores(h, i):
        nk = (i + 1) * bs
        s_ref[h, i % 2, 0:nk, :] = lax.dot_general(
            k_ref[0:nk, cols[h]], q_ref[i * bs:nk, cols[h]], _NT, preferred_element_type=F32)

    def softmax(h, i):
        slot = i % 2
        qs = slice(i * bs, (i + 1) * bs)
        nk = (i + 1) * bs
        biases = [sel_bias[h][j:j + 1, qs] for j in range(i)]
        m = (s_ref[h, slot, i * bs:nk, :] + own_bias).max(axis=0, keepdims=True)
        for j in range(i):
            m = jnp.maximum(m, s_ref[h, slot, j * bs:(j + 1) * bs, :].max(axis=0, keepdims=True) + biases[j])
        own = s_ref[h, slot, i * bs:nk, :] + own_bias
        pt_ref[h, slot, i * bs:nk, :] = jnp.exp2((own - m) * (scale * LOG2E)).astype(BF16)
        for j in range(i):
            p = jnp.exp2((s_ref[h, slot, j * bs:(j + 1) * bs, :] - (m - biases[j])) * (scale * LOG2E))
            pt_ref[h, slot, j * bs:(j + 1) * bs, :] = p.astype(BF16)

    def attend(h, i):
        nk = (i + 1) * bs
        ot = jnp.dot(vt_ref[h, :, 0:nk], pt_ref[h, i % 2, 0:nk, :], preferred_element_type=F32)
        o_ref[i * bs:nk, cols[h]] = (ot[0:hd, :] / ot[hd:hd + 1, :]).T.astype(o_ref.dtype)

    for h in hs:
        scores(h, 0)
    for i in range(nb):
        for h in hs:
            if i + 1 < nb:
                scores(h, i + 1)
        for h in hs:
            if i > 0:
                attend(h, i - 1)
        for h in hs:
            softmax(h, i)
    for h in hs:
        attend(h, nb - 1)


def _moba(proj, *, batch, seq, col0, width, heads_per_step=2):
    n = proj.shape[0]
    hd = MOBA_HEAD_DIM
    wb = heads_per_step * hd
    groups = width // wb
    qb = col0 // wb
    kb = qb + groups
    vb = kb + groups
    kern = functools.partial(_moba_kernel, heads=heads_per_step)
    return pl.pallas_call(
        kern,
        grid=(batch, groups),
        in_specs=[
            pl.BlockSpec((seq, wb), lambda b, h: (b, qb + h)),
            pl.BlockSpec((seq, wb), lambda b, h: (b, kb + h)),
            pl.BlockSpec((seq, wb), lambda b, h: (b, vb + h)),
        ],
        out_specs=pl.BlockSpec((seq, wb), lambda b, h: (b, h)),
        out_shape=jax.ShapeDtypeStruct((n, width), BF16),
        scratch_shapes=[
            pltpu.VMEM((heads_per_step, hd + BF16_SUBLANES, seq), BF16),
            pltpu.VMEM((heads_per_step, 2, seq, MOBA_BLOCK), F32),
            pltpu.VMEM((heads_per_step, 2, seq, MOBA_BLOCK), BF16),
        ],
        compiler_params=_params(("arbitrary", "arbitrary")),
        name="moba",
    )(proj, proj, proj)


def _merge_kernel(og_ref, om_ref, wg_ref, wm_ref, g1_ref, g2_ref, wo_ref, x_ref, gt_ref, o_ref):
    yg = jnp.dot(og_ref[...], wg_ref[...], preferred_element_type=F32)
    ym = jnp.dot(om_ref[...], wm_ref[...], preferred_element_type=F32)
    z = jax.nn.sigmoid(g1_ref[...].astype(F32)) * yg + jax.nn.sigmoid(g2_ref[...].astype(F32)) * ym
    y = jnp.dot(z.astype(BF16), wo_ref[...], preferred_element_type=F32)
    o_ref[...] = x_ref[...] + gt_ref[...] * y


def _merge(o_gla, o_moba, w_up_gla, w_up_moba, proj, gate_col0, w_out, x2, mod, layer, *, seq, tm=512):
    n, d = x2.shape
    per_b = seq // tm
    g1b = gate_col0 // d
    kg = o_gla.shape[1]
    km = o_moba.shape[1]
    resident = dict(pipeline_mode=pl.Buffered(1))
    return pl.pallas_call(
        _merge_kernel,
        grid=(n // tm,),
        in_specs=[
            pl.BlockSpec((tm, kg), lambda i: (i, 0)),
            pl.BlockSpec((tm, km), lambda i: (i, 0)),
            pl.BlockSpec((None, kg, d), lambda i: (layer, 0, 0), **resident),
            pl.BlockSpec((None, km, d), lambda i: (layer, 0, 0), **resident),
            pl.BlockSpec((tm, d), lambda i: (i, g1b)),
            pl.BlockSpec((tm, d), lambda i: (i, g1b + 1)),
            pl.BlockSpec((None, d, d), lambda i: (layer, 0, 0), **resident),
            pl.BlockSpec((tm, d), lambda i: (i, 0)),
            pl.BlockSpec((None, None, 1, d), lambda i: (i // per_b, 2, 0, 0)),
        ],
        out_specs=pl.BlockSpec((tm, d), lambda i: (i, 0)),
        out_shape=jax.ShapeDtypeStruct((n, d), F32),
        compiler_params=_params(("arbitrary",)),
        name="merge",
    )(o_gla, o_moba, w_up_gla, w_up_moba, proj, proj, w_out, x2, mod)


def _ffn_kernel(x_ref, g_ref, sc_ref, sh_ref, gt_ref, wa_ref, wu_ref, wo_ref, fg_ref, o_ref, h_ref, *, final_norm):
    j = pl.program_id(1)

    def part(h):
        a = jnp.dot(h, wa_ref[...], preferred_element_type=F32)
        u = jnp.dot(h, wu_ref[...], preferred_element_type=F32)
        return jnp.dot((jax.nn.silu(a) * u).astype(BF16), wo_ref[...], preferred_element_type=F32)

    @pl.when(j == 0)
    def _():
        h = _mod_norm(x_ref[...], g_ref[...], sc_ref[...], sh_ref[...]).astype(BF16)
        h_ref[...] = h
        o_ref[...] = part(h)

    @pl.when(j > 0)
    def _():
        o_ref[...] += part(h_ref[...])

    @pl.when(j == pl.num_programs(1) - 1)
    def _():
        y = x_ref[...] + gt_ref[...] * o_ref[...]
        if final_norm:
            ms = jnp.mean(y * y, axis=-1, keepdims=True)
            y = y * lax.rsqrt(ms + EPS) * fg_ref[...]
        o_ref[...] = y


def _ffn(x2, g, mod, w_in, w_out, final_g, layer, *, seq, final_norm, tm=1024, tf=512):
    n, d = x2.shape
    dff = w_out.shape[1]
    per_b = seq // tm
    nf = dff // tf
    kern = functools.partial(_ffn_kernel, final_norm=final_norm)
    return pl.pallas_call(
        kern,
        grid=(n // tm, nf),
        in_specs=[
            pl.BlockSpec((tm, d), lambda i, j: (i, 0)),
            pl.BlockSpec((1, d), lambda i, j: (0, 0)),
            pl.BlockSpec((None, None, 1, d), lambda i, j: (i // per_b, 4, 0, 0)),
            pl.BlockSpec((None, None, 1, d), lambda i, j: (i // per_b, 3, 0, 0)),
            pl.BlockSpec((None, None, 1, d), lambda i, j: (i // per_b, 5, 0, 0)),
            pl.BlockSpec((None, d, tf), lambda i, j: (layer, 0, j)),
            pl.BlockSpec((None, d, tf), lambda i, j: (layer, 0, nf + j)),
            pl.BlockSpec((None, tf, d), lambda i, j: (layer, j, 0)),
            pl.BlockSpec((1, d), lambda i, j: (0, 0)),
        ],
        out_specs=pl.BlockSpec((tm, d), lambda i, j: (i, 0)),
        out_shape=jax.ShapeDtypeStruct((n, d), F32),
        scratch_shapes=[pltpu.VMEM((tm, d), BF16)],
        compiler_params=_params(("arbitrary", "arbitrary"), BIG_VMEM_LIMIT_BYTES),
        name="ffn",
    )(x2, g, mod, mod, mod, w_in, w_in, w_out, final_g)


def kernel(x, c, ada_w, ada_b, norm1_g, w_in, gla_gate_w2, gla_gate_b, gla_norm_g, w_up_gla, w_up_moba,
           w_out, norm2_g, w_ffn_in, w_ffn_out, final_g):
    batch, seq, d = x.shape
    depth = ada_w.shape[0]
    rank, key_w = gla_gate_w2.shape[1:]
    val_w = w_up_gla.shape[1]
    moba_w = w_up_moba.shape[1]
    n = batch * seq
    lr0 = 2 * key_w + 2 * val_w
    moba0 = lr0
    gate0 = moba0 + 3 * moba_w

    mod = _adaln(c, ada_w, ada_b).reshape(depth, batch, 6, 1, d)
    w_main, w_lr = _repack_w_in(jnp.swapaxes(w_in, 1, 2), lr0, rank)
    x2 = x.reshape(n, d)
    for l in range(depth):
        w2p = jnp.pad(gla_gate_w2[l], ((0, LANES - rank), (0, 0))).astype(BF16)
        casts = [(w_up_gla, 16), (w_up_moba, 16), (w_out, 32), (w_ffn_in, 32), (w_ffn_out, 128)]
        proj, lr, (w_up_gla_b, w_up_moba_b, w_out_b, w_ffn_in_b, w_ffn_out_b) = _in_proj(
            x2, norm1_g[l][None], mod[l], w_main, w_lr, l, casts, seq=seq)
        o_gla = _gla(proj, lr, w2p, gla_gate_b[l][None], gla_norm_g[l][None],
                     batch=batch, seq=seq, key_w=key_w, val_w=val_w)
        o_moba = _moba(proj, batch=batch, seq=seq, col0=moba0, width=moba_w)
        x2 = _merge(o_gla, o_moba, w_up_gla_b, w_up_moba_b, proj, gate0, w_out_b, x2, mod[l], 0, seq=seq)
        x2 = _ffn(x2, norm2_g[l][None], mod[l], w_ffn_in_b, w_ffn_out_b, final_g[None], 0,
                  seq=seq, final_norm=(l == depth - 1))
    return x2.reshape(batch, seq, d)
```

```python
import functools

import jax
import jax.numpy as jnp
from jax import lax
from jax.experimental import pallas as pl
from jax.experimental.pallas import tpu as pltpu

F32 = jnp.float32
BF16 = jnp.bfloat16

GLA_HEADS = 4
GLA_GATE_NORM = 16.0
GLA_CHUNK = 64
MOBA_HEAD_DIM = 128
MOBA_BLOCK = 256
MOBA_TOPK = 3
EPS = 1e-6
LOG2E = 1.4426950408889634

V7X_VMEM_BYTES = 64 * 1024 * 1024
VMEM_LIMIT_BYTES = V7X_VMEM_BYTES - 8 * 1024 * 1024
BIG_VMEM_LIMIT_BYTES = V7X_VMEM_BYTES - 4 * 1024 * 1024
LANES = 128
BF16_SUBLANES = 16

_NT = (((1,), (1,)), ((), ()))
_TN = (((0,), (0,)), ((), ()))


def _params(semantics, vmem_limit_bytes=VMEM_LIMIT_BYTES):
    return pltpu.CompilerParams(dimension_semantics=semantics, vmem_limit_bytes=vmem_limit_bytes)


def _mod_norm(x, g, sc, sh):
    ms = jnp.mean(x * x, axis=-1, keepdims=True)
    y = x * lax.rsqrt(ms + EPS) * g
    return y * (1.0 + sc) + sh


def _adaln_kernel(c_ref, w_ref, b_ref, o_ref):
    c_act = jax.nn.silu(c_ref[...])
    o_ref[...] = (
        jnp.dot(c_act.astype(BF16), w_ref[...].astype(BF16), preferred_element_type=F32) + b_ref[...]
    )


def _adaln(c, ada_w, ada_b, *, tn=1024):
    depth, d, n6 = ada_w.shape
    b = c.shape[0]
    return pl.pallas_call(
        _adaln_kernel,
        grid=(depth, n6 // tn),
        in_specs=[
            pl.BlockSpec((b, d), lambda l, j: (0, 0)),
            pl.BlockSpec((None, d, tn), lambda l, j: (l, 0, j)),
            pl.BlockSpec((None, 1, tn), lambda l, j: (l, 0, j)),
        ],
        out_specs=pl.BlockSpec((None, b, tn), lambda l, j: (l, 0, j)),
        out_shape=jax.ShapeDtypeStruct((depth, b, n6), F32),
        compiler_params=_params(("arbitrary", "arbitrary")),
        name="adaln",
    )(c, ada_w, ada_b.reshape(depth, 1, n6))


def _repack_kernel(a_ref, b_ref, o_ref, ol_ref, *, lr_tile, rank):
    t = pl.program_id(1)

    @pl.when(t < lr_tile)
    def _():
        o_ref[...] = a_ref[...].astype(BF16)

    @pl.when(t >= lr_tile)
    def _():
        o_ref[...] = jnp.concatenate([a_ref[rank:, :], b_ref[...]], axis=0).astype(BF16)

    @pl.when(t == lr_tile)
    def _():
        pad = jnp.zeros((ol_ref.shape[0] - rank, ol_ref.shape[1]), F32)
        ol_ref[...] = jnp.concatenate([a_ref[0:rank, :], pad], axis=0).astype(BF16)


def _repack_w_in(w_in_t, lr0, rank, *, tw=1024):
    depth, c, d = w_in_t.shape
    assert lr0 % tw == 0 and (c - rank) % tw == 0 and tw % rank == 0 and rank % BF16_SUBLANES == 0
    kern = functools.partial(_repack_kernel, lr_tile=lr0 // tw, rank=rank)
    return pl.pallas_call(
        kern,
        grid=(depth, (c - rank) // tw),
        in_specs=[
            pl.BlockSpec((None, tw, d), lambda l, t: (l, t, 0)),
            pl.BlockSpec((None, rank, d), lambda l, t: (l, (t + 1) * (tw // rank), 0)),
        ],
        out_specs=[
            pl.BlockSpec((None, tw, d), lambda l, t: (l, t, 0)),
            pl.BlockSpec((None, LANES, d), lambda l, t: (l, 0, 0)),
        ],
        out_shape=[
            jax.ShapeDtypeStruct((depth, c - rank, d), BF16),
            jax.ShapeDtypeStruct((depth, LANES, d), BF16),
        ],
        compiler_params=_params(("arbitrary", "arbitrary")),
        name="repack_w_in",
    )(w_in_t, w_in_t)


def _in_proj_kernel(x_ref, g_ref, sc_ref, sh_ref, w_ref, wl_ref, *rest, n_cast, norm_chunks=4):
    cast_in = rest[:n_cast]
    o_ref, ol_ref = rest[n_cast:n_cast + 2]
    cast_out = rest[n_cast + 2:2 * n_cast + 2]
    h_ref = rest[-1]
    j = pl.program_id(1)

    def side_jobs():
        for src, dst in zip(cast_in, cast_out):
            dst[...] = src[...].astype(BF16)

    @pl.when(j == 0)
    def _():
        rows = x_ref.shape[0] // norm_chunks
        for c in range(norm_chunks):
            r = slice(c * rows, (c + 1) * rows)
            h = _mod_norm(x_ref[r, :], g_ref[...], sc_ref[...], sh_ref[...]).astype(BF16)
            h_ref[r, :] = h
            ol_ref[r, :] = lax.dot_general(h, wl_ref[...], _NT, preferred_element_type=F32).astype(ol_ref.dtype)
            o_ref[r, :] = lax.dot_general(h, w_ref[...], _NT, preferred_element_type=F32).astype(o_ref.dtype)
        side_jobs()

    @pl.when(j > 0)
    def _():
        o_ref[...] = lax.dot_general(h_ref[...], w_ref[...], _NT, preferred_element_type=F32).astype(o_ref.dtype)
        side_jobs()


def _in_proj(x2, g, mod, w_main, w_lr, layer, casts, *, seq, tm=1024, tn=2048):
    n, d = x2.shape
    c = w_main.shape[1]
    per_b = seq // tm
    ncol = c // tn
    steps = (n // tm) * ncol
    cast_in_specs, cast_out_specs, cast_shapes = [], [], []
    for w, rows in casts:
        nblk = w.shape[1] // rows
        assert w.shape[1] % rows == 0 and rows % BF16_SUBLANES == 0 and nblk <= steps

        def block(i, j, nblk=nblk):
            return jnp.minimum(i * ncol + j, nblk - 1)

        cast_in_specs.append(pl.BlockSpec((None, rows, w.shape[2]), lambda i, j, b=block: (layer, b(i, j), 0)))
        cast_out_specs.append(pl.BlockSpec((None, rows, w.shape[2]), lambda i, j, b=block: (0, b(i, j), 0)))
        cast_shapes.append(jax.ShapeDtypeStruct((1,) + w.shape[1:], BF16))
    kern = functools.partial(_in_proj_kernel, n_cast=len(casts))
    outs = pl.pallas_call(
        kern,
        grid=(n // tm, ncol),
        in_specs=[
            pl.BlockSpec((tm, d), lambda i, j: (i, 0)),
            pl.BlockSpec((1, d), lambda i, j: (0, 0)),
            pl.BlockSpec((None, None, 1, d), lambda i, j: (i // per_b, 1, 0, 0)),
            pl.BlockSpec((None, None, 1, d), lambda i, j: (i // per_b, 0, 0, 0)),
            pl.BlockSpec((None, tn, d), lambda i, j: (layer, j, 0)),
            pl.BlockSpec((None, LANES, d), lambda i, j: (layer, 0, 0)),
        ] + cast_in_specs,
        out_specs=[
            pl.BlockSpec((tm, tn), lambda i, j: (i, j)),
            pl.BlockSpec((tm, LANES), lambda i, j: (i, 0)),
        ] + cast_out_specs,
        out_shape=[
            jax.ShapeDtypeStruct((n, c), BF16),
            jax.ShapeDtypeStruct((n, LANES), BF16),
        ] + cast_shapes,
        scratch_shapes=[pltpu.VMEM((tm, d), BF16)],
        compiler_params=_params(("arbitrary", "arbitrary"), BIG_VMEM_LIMIT_BYTES),
        name="in_proj",
    )(x2, g, mod, mod, w_main, w_lr, *[w for w, _ in casts])
    return outs[0], outs[1], outs[2:]


def _split3(x):
    hi = x.astype(BF16)
    r1 = x - hi.astype(F32)
    mid = r1.astype(BF16)
    lo = (r1 - mid.astype(F32)).astype(BF16)
    return hi, mid, lo


def _gla_kernel(q_ref, k_ref, v_ref, gr_ref, lr_ref, w2_ref, gb_ref, ng_ref, o_ref,
                qm_ref, km_ref, qe_ref, ke_ref, el_ref, st_ref, *, heads, dk, dv, gate_rows):
    seq = q_ref.shape[0]
    cs = GLA_CHUNK
    kw = heads * dk
    per_trip = gate_rows // cs
    row = lax.broadcasted_iota(jnp.int32, (cs, cs), 0)
    col = lax.broadcasted_iota(jnp.int32, (cs, cs), 1)
    causal = col <= row
    tril = causal.astype(BF16)
    scale = dk ** -0.5

    def decay(pi, carry):
        base = pl.multiple_of(pi * gate_rows, gate_rows)
        r = pl.ds(base, gate_rows)
        xg = jnp.dot(lr_ref[r, :], w2_ref[...], preferred_element_type=F32) + gb_ref[...]
        g = (jnp.minimum(xg, 0.0) - jnp.log1p(jnp.exp(-jnp.abs(xg)))) / GLA_GATE_NORM
        parts = jnp.concatenate(_split3(g), axis=1)
        q = q_ref[r, :].astype(F32) * scale
        k = k_ref[r, :].astype(F32)
        for c in range(per_trip):
            rows = slice(c * cs, (c + 1) * cs)
            out = pl.ds(base + c * cs, cs)
            bs = jnp.dot(tril, parts[rows, :], preferred_element_type=F32)
            b = bs[:, :kw] + bs[:, kw:2 * kw] + bs[:, 2 * kw:]
            b_last = b[cs - 1:cs, :]
            b_mid = b[cs // 2 - 1:cs // 2, :]
            qm_ref[out, :] = (q[rows, :] * jnp.exp(b - b_mid)).astype(BF16)
            km_ref[out, :] = (k[rows, :] * jnp.exp(b_mid - b)).astype(BF16)
            qe_ref[out, :] = (q[rows, :] * jnp.exp(b)).astype(BF16)
            ke_ref[out, :] = (k[rows, :] * jnp.exp(b_last - b)).astype(BF16)
            el_ref[pl.ds(pi * per_trip + c, 1), :] = jnp.exp(b_last)
        return carry

    lax.fori_loop(0, seq // gate_rows, decay, 0)
    st_ref[...] = jnp.zeros_like(st_ref)

    def chunk(ci, carry):
        r = pl.ds(pl.multiple_of(ci * cs, cs), cs)
        e_last = el_ref[pl.ds(ci, 1), :]
        hs = range(heads)
        ks = [slice(h * dk, (h + 1) * dk) for h in hs]
        vs = [slice(h * dv, (h + 1) * dv) for h in hs]
        attn = [lax.dot_general(qm_ref[r, ks[h]], km_ref[r, ks[h]], _NT, preferred_element_type=F32) for h in hs]
        v = [v_ref[r, vs[h]] for h in hs]
        st = [st_ref[h] for h in hs]
        o_inter = [lax.dot_general(qe_ref[r, ks[h]], st[h].astype(BF16), _NT, preferred_element_type=F32)
                   for h in hs]
        kv = [lax.dot_general(v[h], ke_ref[r, ks[h]], _TN, preferred_element_type=F32) for h in hs]
        attn = [jnp.where(causal, attn[h], 0.0).astype(BF16) for h in hs]
        o = [jnp.dot(attn[h], v[h], preferred_element_type=F32) + o_inter[h] for h in hs]
        for h in hs:
            st_ref[h] = st[h] * e_last[:, ks[h]] + kv[h]
        for h in hs:
            ms = jnp.mean(o[h] * o[h], axis=-1, keepdims=True)
            y = o[h] * lax.rsqrt(ms + EPS) * ng_ref[...]
            o_ref[r, vs[h]] = (y * jax.nn.silu(gr_ref[r, vs[h]].astype(F32))).astype(o_ref.dtype)
        return carry

    lax.fori_loop(0, seq // cs, chunk, 0, unroll=2)


def _gla(proj, lr, w2p, gate_b, norm_g, *, batch, seq, key_w, val_w, gate_rows=512):
    n = proj.shape[0]
    dk = key_w // GLA_HEADS
    dv = val_w // GLA_HEADS
    vb = 2 * key_w // val_w
    rb = vb + 1
    kern = functools.partial(_gla_kernel, heads=GLA_HEADS, dk=dk, dv=dv, gate_rows=gate_rows)
    return pl.pallas_call(
        kern,
        grid=(batch,),
        in_specs=[
            pl.BlockSpec((seq, key_w), lambda b: (b, 0)),
            pl.BlockSpec((seq, key_w), lambda b: (b, 1)),
            pl.BlockSpec((seq, val_w), lambda b: (b, vb)),
            pl.BlockSpec((seq, val_w), lambda b: (b, rb)),
            pl.BlockSpec((seq, LANES), lambda b: (b, 0)),
            pl.BlockSpec((LANES, key_w), lambda b: (0, 0)),
            pl.BlockSpec((1, key_w), lambda b: (0, 0)),
            pl.BlockSpec((1, dv), lambda b: (0, 0)),
        ],
        out_specs=pl.BlockSpec((seq, val_w), lambda b: (b, 0)),
        out_shape=jax.ShapeDtypeStruct((n, val_w), BF16),
        scratch_shapes=[
            pltpu.VMEM((seq, key_w), BF16),
            pltpu.VMEM((seq, key_w), BF16),
            pltpu.VMEM((seq, key_w), BF16),
            pltpu.VMEM((seq, key_w), BF16),
            pltpu.VMEM((seq // GLA_CHUNK, key_w), F32),
            pltpu.VMEM((GLA_HEADS, dv, dk), F32),
        ],
        compiler_params=_params(("arbitrary",)),
        name="gla",
    )(proj, proj, proj, proj, lr, w2p, gate_b, norm_g)


def _moba_kernel(q_ref, k_ref, v_ref, o_ref, vt_ref, s_ref, pt_ref, *, heads):
    bs = MOBA_BLOCK
    hd = MOBA_HEAD_DIM
    seq = k_ref.shape[0]
    nb = seq // bs
    scale = hd ** -0.5
    hs = range(heads)
    cols = [slice(h * hd, (h + 1) * hd) for h in hs]

    krow = lax.broadcasted_iota(jnp.int32, (bs, bs), 0)
    qcol = lax.broadcasted_iota(jnp.int32, (bs, bs), 1)
    own_bias = jnp.where(krow <= qcol, 0.0, -jnp.inf)
    ones_row = (lax.broadcasted_iota(jnp.int32, (BF16_SUBLANES, seq), 0) == 0).astype(F32).astype(BF16)

    def select(h):
        km = jnp.mean(k_ref[:, cols[h]].astype(F32).reshape(nb, bs, hd), axis=1)
        km = jnp.concatenate([km, jnp.zeros((BF16_SUBLANES - nb, hd), F32)], axis=0).astype(BF16)
        sc = lax.dot_general(km, q_ref[:, cols[h]], _NT, preferred_element_type=F32)[0:nb, :]
        blk = lax.broadcasted_iota(jnp.int32, sc.shape, 0)
        qpos = lax.broadcasted_iota(jnp.int32, sc.shape, 1)
        past = (blk + 1) * bs <= qpos
        sc = jnp.where(past, sc, -jnp.inf)
        beaten = jnp.zeros(sc.shape, F32)
        for jp in range(nb):
            c = sc[jp:jp + 1, :]
            ahead = (c > sc) | ((c == sc) & (blk > jp))
            beaten = beaten + ahead.astype(F32)
        vt_ref[h, 0:hd, :] = v_ref[:, cols[h]].astype(F32).T.astype(BF16)
        vt_ref[h, hd:hd + BF16_SUBLANES, :] = ones_row
        return jnp.where(past & (beaten < MOBA_TOPK), 0.0, -jnp.inf)

    sel_bias = [select(h) for h in hs]

    def scores(h, i):
        nk = (i + 1) * bs
        s_ref[h, i % 2, 0:nk, :] = lax.dot_general(
            k_ref[0:nk, cols[h]], q_ref[i * bs:nk, cols[h]], _NT, preferred_element_type=F32)

    def softmax(h, i):
        slot = i % 2
        qs = slice(i * bs, (i + 1) * bs)
        nk = (i + 1) * bs
        biases = [sel_bias[h][j:j + 1, qs] for j in range(i)]
        m = (s_ref[h, slot, i * bs:nk, :] + own_bias).max(axis=0, keepdims=True)
        for j in range(i):
            m = jnp.maximum(m, s_ref[h, slot, j * bs:(j + 1) * bs, :].max(axis=0, keepdims=True) + biases[j])
        own = s_ref[h, slot, i * bs:nk, :] + own_bias
        pt_ref[h, slot, i * bs:nk, :] = jnp.exp2((own - m) * (scale * LOG2E)).astype(BF16)
        for j in range(i):
            p = jnp.exp2((s_ref[h, slot, j * bs:(j + 1) * bs, :] - (m - biases[j])) * (scale * LOG2E))
            pt_ref[h, slot, j * bs:(j + 1) * bs, :] = p.astype(BF16)

    def attend(h, i):
        nk = (i + 1) * bs
        ot = jnp.dot(vt_ref[h, :, 0:nk], pt_ref[h, i % 2, 0:nk, :], preferred_element_type=F32)
        o_ref[i * bs:nk, cols[h]] = (ot[0:hd, :] / ot[hd:hd + 1, :]).T.astype(o_ref.dtype)

    for h in hs:
        scores(h, 0)
    for i in range(nb):
        for h in hs:
            if i + 1 < nb:
                scores(h, i + 1)
        for h in hs:
            if i > 0:
                attend(h, i - 1)
        for h in hs:
            softmax(h, i)
    for h in hs:
        attend(h, nb - 1)


def _moba(proj, *, batch, seq, col0, width, heads_per_step=2):
    n = proj.shape[0]
    hd = MOBA_HEAD_DIM
    wb = heads_per_step * hd
    groups = width // wb
    qb = col0 // wb
    kb = qb + groups
    vb = kb + groups
    kern = functools.partial(_moba_kernel, heads=heads_per_step)
    return pl.pallas_call(
        kern,
        grid=(batch, groups),
        in_specs=[
            pl.BlockSpec((seq, wb), lambda b, h: (b, qb + h)),
            pl.BlockSpec((seq, wb), lambda b, h: (b, kb + h)),
            pl.BlockSpec((seq, wb), lambda b, h: (b, vb + h)),
        ],
        out_specs=pl.BlockSpec((seq, wb), lambda b, h: (b, h)),
        out_shape=jax.ShapeDtypeStruct((n, width), BF16),
        scratch_shapes=[
            pltpu.VMEM((heads_per_step, hd + BF16_SUBLANES, seq), BF16),
            pltpu.VMEM((heads_per_step, 2, seq, MOBA_BLOCK), F32),
            pltpu.VMEM((heads_per_step, 2, seq, MOBA_BLOCK), BF16),
        ],
        compiler_params=_params(("arbitrary", "arbitrary")),
        name="moba",
    )(proj, proj, proj)


def _merge_kernel(og_ref, om_ref, wg_ref, wm_ref, g1_ref, g2_ref, wo_ref, x_ref, gt_ref, o_ref):
    yg = jnp.dot(og_ref[...], wg_ref[...], preferred_element_type=F32)
    ym = jnp.dot(om_ref[...], wm_ref[...], preferred_element_type=F32)
    z = jax.nn.sigmoid(g1_ref[...].astype(F32)) * yg + jax.nn.sigmoid(g2_ref[...].astype(F32)) * ym
    y = jnp.dot(z.astype(BF16), wo_ref[...], preferred_element_type=F32)
    o_ref[...] = x_ref[...] + gt_ref[...] * y


def _merge(o_gla, o_moba, w_up_gla, w_up_moba, proj, gate_col0, w_out, x2, mod, layer, *, seq, tm=512):
    n, d = x2.shape
    per_b = seq // tm
    g1b = gate_col0 // d
    kg = o_gla.shape[1]
    km = o_moba.shape[1]
    resident = dict(pipeline_mode=pl.Buffered(1))
    return pl.pallas_call(
        _merge_kernel,
        grid=(n // tm,),
        in_specs=[
            pl.BlockSpec((tm, kg), lambda i: (i, 0)),
            pl.BlockSpec((tm, km), lambda i: (i, 0)),
            pl.BlockSpec((None, kg, d), lambda i: (layer, 0, 0), **resident),
            pl.BlockSpec((None, km, d), lambda i: (layer, 0, 0), **resident),
            pl.BlockSpec((tm, d), lambda i: (i, g1b)),
            pl.BlockSpec((tm, d), lambda i: (i, g1b + 1)),
            pl.BlockSpec((None, d, d), lambda i: (layer, 0, 0), **resident),
            pl.BlockSpec((tm, d), lambda i: (i, 0)),
            pl.BlockSpec((None, None, 1, d), lambda i: (i // per_b, 2, 0, 0)),
        ],
        out_specs=pl.BlockSpec((tm, d), lambda i: (i, 0)),
        out_shape=jax.ShapeDtypeStruct((n, d), F32),
        compiler_params=_params(("arbitrary",)),
        name="merge",
    )(o_gla, o_moba, w_up_gla, w_up_moba, proj, proj, w_out, x2, mod)


def _ffn_kernel(x_ref, g_ref, sc_ref, sh_ref, gt_ref, wa_ref, wu_ref, wo_ref, fg_ref, o_ref, h_ref, *, final_norm):
    j = pl.program_id(1)

    def part(h):
        a = jnp.dot(h, wa_ref[...], preferred_element_type=F32)
        u = jnp.dot(h, wu_ref[...], preferred_element_type=F32)
        return jnp.dot((jax.nn.silu(a) * u).astype(BF16), wo_ref[...], preferred_element_type=F32)

    @pl.when(j == 0)
    def _():
        h = _mod_norm(x_ref[...], g_ref[...], sc_ref[...], sh_ref[...]).astype(BF16)
        h_ref[...] = h
        o_ref[...] = part(h)

    last = pl.num_programs(1) - 1

    @pl.when((j > 0) & (j < last))
    def _():
        o_ref[...] += part(h_ref[...])

    @pl.when(j == last)
    def _():
        y = x_ref[...] + gt_ref[...] * (o_ref[...] + part(h_ref[...]))
        if final_norm:
            ms = jnp.mean(y * y, axis=-1, keepdims=True)
            y = y * lax.rsqrt(ms + EPS) * fg_ref[...]
        o_ref[...] = y


def _ffn(x2, g, mod, w_in, w_out, final_g, layer, *, seq, final_norm, tm=1024, tf=512):
    n, d = x2.shape
    dff = w_out.shape[1]
    per_b = seq // tm
    nf = dff // tf
    kern = functools.partial(_ffn_kernel, final_norm=final_norm)
    return pl.pallas_call(
        kern,
        grid=(n // tm, nf),
        in_specs=[
            pl.BlockSpec((tm, d), lambda i, j: (i, 0)),
            pl.BlockSpec((1, d), lambda i, j: (0, 0)),
            pl.BlockSpec((None, None, 1, d), lambda i, j: (i // per_b, 4, 0, 0)),
            pl.BlockSpec((None, None, 1, d), lambda i, j: (i // per_b, 3, 0, 0)),
            pl.BlockSpec((None, None, 1, d), lambda i, j: (i // per_b, 5, 0, 0)),
            pl.BlockSpec((None, d, tf), lambda i, j: (layer, 0, j)),
            pl.BlockSpec((None, d, tf), lambda i, j: (layer, 0, nf + j)),
            pl.BlockSpec((None, tf, d), lambda i, j: (layer, j, 0)),
            pl.BlockSpec((1, d), lambda i, j: (0, 0)),
        ],
        out_specs=pl.BlockSpec((tm, d), lambda i, j: (i, 0)),
        out_shape=jax.ShapeDtypeStruct((n, d), F32),
        scratch_shapes=[pltpu.VMEM((tm, d), BF16)],
        compiler_params=_params(("arbitrary", "arbitrary"), BIG_VMEM_LIMIT_BYTES),
        name="ffn",
    )(x2, g, mod, mod, mod, w_in, w_in, w_out, final_g)


def kernel(x, c, ada_w, ada_b, norm1_g, w_in, gla_gate_w2, gla_gate_b, gla_norm_g, w_up_gla, w_up_moba,
           w_out, norm2_g, w_ffn_in, w_ffn_out, final_g):
    batch, seq, d = x.shape
    depth = ada_w.shape[0]
    rank, key_w = gla_gate_w2.shape[1:]
    val_w = w_up_gla.shape[1]
    moba_w = w_up_moba.shape[1]
    n = batch * seq
    lr0 = 2 * key_w + 2 * val_w
    moba0 = lr0
    gate0 = moba0 + 3 * moba_w

    mod = _adaln(c, ada_w, ada_b).reshape(depth, batch, 6, 1, d)
    w_main, w_lr = _repack_w_in(jnp.swapaxes(w_in, 1, 2), lr0, rank)
    x2 = x.reshape(n, d)
    for l in range(depth):
        w2p = jnp.pad(gla_gate_w2[l], ((0, LANES - rank), (0, 0))).astype(BF16)
        casts = [(w_up_gla, 16), (w_up_moba, 16), (w_out, 32), (w_ffn_in, 32), (w_ffn_out, 128)]
        proj, lr, (w_up_gla_b, w_up_moba_b, w_out_b, w_ffn_in_b, w_ffn_out_b) = _in_proj(
            x2, norm1_g[l][None], mod[l], w_main, w_lr, l, casts, seq=seq)
        o_gla = _gla(proj, lr, w2p, gla_gate_b[l][None], gla_norm_g[l][None],
                     batch=batch, seq=seq, key_w=key_w, val_w=val_w)
        o_moba = _moba(proj, batch=batch, seq=seq, col0=moba0, width=moba_w)
        x2 = _merge(o_gla, o_moba, w_up_gla_b, w_up_moba_b, proj, gate0, w_out_b, x2, mod[l], 0, seq=seq)
        x2 = _ffn(x2, norm2_g[l][None], mod[l], w_ffn_in_b, w_ffn_out_b, final_g[None], 0,
                  seq=seq, final_norm=(l == depth - 1))
    return x2.reshape(batch, seq, d)
```

```python
import functools

import jax
import jax.numpy as jnp
from jax import lax
from jax.experimental import pallas as pl
from jax.experimental.pallas import tpu as pltpu

F32 = jnp.float32
BF16 = jnp.bfloat16

GLA_HEADS = 4
GLA_GATE_NORM = 16.0
GLA_CHUNK = 64
MOBA_HEAD_DIM = 128
MOBA_BLOCK = 256
MOBA_TOPK = 3
EPS = 1e-6
LOG2E = 1.4426950408889634

V7X_VMEM_BYTES = 64 * 1024 * 1024
VMEM_LIMIT_BYTES = V7X_VMEM_BYTES - 8 * 1024 * 1024
BIG_VMEM_LIMIT_BYTES = V7X_VMEM_BYTES - 4 * 1024 * 1024
LANES = 128
BF16_SUBLANES = 16


class _Tiles:
    adaln_cols = 1024
    repack_rows = 1024
    in_proj_rows, in_proj_cols = 1024, 2048
    in_proj_norm_chunks = 4
    gla_decay_rows = 512
    moba_heads = 2
    merge_rows = 512
    ffn_rows, ffn_cols = 1024, 512

_NT = (((1,), (1,)), ((), ()))
_TN = (((0,), (0,)), ((), ()))


def _params(semantics, vmem_limit_bytes=VMEM_LIMIT_BYTES):
    return pltpu.CompilerParams(dimension_semantics=semantics, vmem_limit_bytes=vmem_limit_bytes)


def _mod_norm(x, g, sc, sh):
    ms = jnp.mean(x * x, axis=-1, keepdims=True)
    y = x * lax.rsqrt(ms + EPS) * g
    return y * (1.0 + sc) + sh


def _adaln_kernel(c_ref, w_ref, b_ref, o_ref):
    c_act = jax.nn.silu(c_ref[...])
    o_ref[...] = (
        jnp.dot(c_act.astype(BF16), w_ref[...].astype(BF16), preferred_element_type=F32) + b_ref[...]
    )


def _adaln(c, ada_w, ada_b, *, tn=_Tiles.adaln_cols):
    depth, d, n6 = ada_w.shape
    b = c.shape[0]
    return pl.pallas_call(
        _adaln_kernel,
        grid=(depth, n6 // tn),
        in_specs=[
            pl.BlockSpec((b, d), lambda l, j: (0, 0)),
            pl.BlockSpec((None, d, tn), lambda l, j: (l, 0, j)),
            pl.BlockSpec((None, 1, tn), lambda l, j: (l, 0, j)),
        ],
        out_specs=pl.BlockSpec((None, b, tn), lambda l, j: (l, 0, j)),
        out_shape=jax.ShapeDtypeStruct((depth, b, n6), F32),
        compiler_params=_params(("arbitrary", "arbitrary")),
        name="adaln",
    )(c, ada_w, ada_b.reshape(depth, 1, n6))


def _repack_kernel(a_ref, b_ref, o_ref, ol_ref, *, lr_tile, rank):
    t = pl.program_id(1)

    @pl.when(t < lr_tile)
    def _():
        o_ref[...] = a_ref[...].astype(BF16)

    @pl.when(t >= lr_tile)
    def _():
        o_ref[...] = jnp.concatenate([a_ref[rank:, :], b_ref[...]], axis=0).astype(BF16)

    @pl.when(t == lr_tile)
    def _():
        pad = jnp.zeros((ol_ref.shape[0] - rank, ol_ref.shape[1]), F32)
        ol_ref[...] = jnp.concatenate([a_ref[0:rank, :], pad], axis=0).astype(BF16)


def _repack_w_in(w_in_t, lr0, rank, *, tw=_Tiles.repack_rows):
    depth, c, d = w_in_t.shape
    assert lr0 % tw == 0 and (c - rank) % tw == 0 and tw % rank == 0 and rank % BF16_SUBLANES == 0
    kern = functools.partial(_repack_kernel, lr_tile=lr0 // tw, rank=rank)
    return pl.pallas_call(
        kern,
        grid=(depth, (c - rank) // tw),
        in_specs=[
            pl.BlockSpec((None, tw, d), lambda l, t: (l, t, 0)),
            pl.BlockSpec((None, rank, d), lambda l, t: (l, (t + 1) * (tw // rank), 0)),
        ],
        out_specs=[
            pl.BlockSpec((None, tw, d), lambda l, t: (l, t, 0)),
            pl.BlockSpec((None, LANES, d), lambda l, t: (l, 0, 0)),
        ],
        out_shape=[
            jax.ShapeDtypeStruct((depth, c - rank, d), BF16),
            jax.ShapeDtypeStruct((depth, LANES, d), BF16),
        ],
        compiler_params=_params(("arbitrary", "arbitrary")),
        name="repack_w_in",
    )(w_in_t, w_in_t)


def _cast_rows(rows, steps):
    for r in range(BF16_SUBLANES, rows + 1, BF16_SUBLANES):
        if rows % r == 0 and rows // r <= steps:
            return r
    raise ValueError(f"no bf16-aligned row block covers {rows} rows in {steps} steps")


def _in_proj_kernel(x_ref, g_ref, sc_ref, sh_ref, w_ref, wl_ref, *rest, n_cast, norm_chunks):
    cast_in = rest[:n_cast]
    o_ref, ol_ref = rest[n_cast:n_cast + 2]
    cast_out = rest[n_cast + 2:2 * n_cast + 2]
    h_ref = rest[-1]
    j = pl.program_id(1)

    def side_jobs():
        for src, dst in zip(cast_in, cast_out):
            dst[...] = src[...].astype(BF16)

    @pl.when(j == 0)
    def _():
        rows = x_ref.shape[0] // norm_chunks
        for c in range(norm_chunks):
            r = slice(c * rows, (c + 1) * rows)
            h = _mod_norm(x_ref[r, :], g_ref[...], sc_ref[...], sh_ref[...]).astype(BF16)
            h_ref[r, :] = h
            ol_ref[r, :] = lax.dot_general(h, wl_ref[...], _NT, preferred_element_type=F32).astype(ol_ref.dtype)
            o_ref[r, :] = lax.dot_general(h, w_ref[...], _NT, preferred_element_type=F32).astype(o_ref.dtype)
        side_jobs()

    @pl.when(j > 0)
    def _():
        o_ref[...] = lax.dot_general(h_ref[...], w_ref[...], _NT, preferred_element_type=F32).astype(o_ref.dtype)
        side_jobs()


def _in_proj(x2, g, mod, w_main, w_lr, layer, cast_weights, *, seq,
             tm=_Tiles.in_proj_rows, tn=_Tiles.in_proj_cols):
    n, d = x2.shape
    c = w_main.shape[1]
    per_b = seq // tm
    ncol = c // tn
    steps = (n // tm) * ncol
    cast_in_specs, cast_out_specs, cast_shapes = [], [], []
    casts = [(w, _cast_rows(w.shape[1], steps)) for w in cast_weights]
    for w, rows in casts:
        nblk = w.shape[1] // rows

        def block(i, j, nblk=nblk):
            return jnp.minimum(i * ncol + j, nblk - 1)

        cast_in_specs.append(pl.BlockSpec((None, rows, w.shape[2]), lambda i, j, b=block: (layer, b(i, j), 0)))
        cast_out_specs.append(pl.BlockSpec((None, rows, w.shape[2]), lambda i, j, b=block: (0, b(i, j), 0)))
        cast_shapes.append(jax.ShapeDtypeStruct((1,) + w.shape[1:], BF16))
    kern = functools.partial(_in_proj_kernel, n_cast=len(casts), norm_chunks=_Tiles.in_proj_norm_chunks)
    outs = pl.pallas_call(
        kern,
        grid=(n // tm, ncol),
        in_specs=[
            pl.BlockSpec((tm, d), lambda i, j: (i, 0)),
            pl.BlockSpec((1, d), lambda i, j: (0, 0)),
            pl.BlockSpec((None, None, 1, d), lambda i, j: (i // per_b, 1, 0, 0)),
            pl.BlockSpec((None, None, 1, d), lambda i, j: (i // per_b, 0, 0, 0)),
            pl.BlockSpec((None, tn, d), lambda i, j: (layer, j, 0)),
            pl.BlockSpec((None, LANES, d), lambda i, j: (layer, 0, 0)),
        ] + cast_in_specs,
        out_specs=[
            pl.BlockSpec((tm, tn), lambda i, j: (i, j)),
            pl.BlockSpec((tm, LANES), lambda i, j: (i, 0)),
        ] + cast_out_specs,
        out_shape=[
            jax.ShapeDtypeStruct((n, c), BF16),
            jax.ShapeDtypeStruct((n, LANES), BF16),
        ] + cast_shapes,
        scratch_shapes=[pltpu.VMEM((tm, d), BF16)],
        compiler_params=_params(("arbitrary", "arbitrary"), BIG_VMEM_LIMIT_BYTES),
        name="in_proj",
    )(x2, g, mod, mod, w_main, w_lr, *[w for w, _ in casts])
    return outs[0], outs[1], outs[2:]


def _split3(x):
    hi = x.astype(BF16)
    r1 = x - hi.astype(F32)
    mid = r1.astype(BF16)
    lo = (r1 - mid.astype(F32)).astype(BF16)
    return hi, mid, lo


def _gla_kernel(q_ref, k_ref, v_ref, gr_ref, lr_ref, w2_ref, gb_ref, ng_ref, o_ref,
                qm_ref, km_ref, qe_ref, ke_ref, el_ref, st_ref, *, heads, dk, dv, gate_rows):
    seq = q_ref.shape[0]
    cs = GLA_CHUNK
    kw = heads * dk
    per_trip = gate_rows // cs
    row = lax.broadcasted_iota(jnp.int32, (cs, cs), 0)
    col = lax.broadcasted_iota(jnp.int32, (cs, cs), 1)
    causal = col <= row
    tril = causal.astype(BF16)
    scale = dk ** -0.5

    def decay(pi, carry):
        base = pl.multiple_of(pi * gate_rows, gate_rows)
        r = pl.ds(base, gate_rows)
        xg = jnp.dot(lr_ref[r, :], w2_ref[...], preferred_element_type=F32) + gb_ref[...]
        g = (jnp.minimum(xg, 0.0) - jnp.log1p(jnp.exp(-jnp.abs(xg)))) / GLA_GATE_NORM
        parts = jnp.concatenate(_split3(g), axis=1)
        q = q_ref[r, :].astype(F32) * scale
        k = k_ref[r, :].astype(F32)
        for c in range(per_trip):
            rows = slice(c * cs, (c + 1) * cs)
            out = pl.ds(base + c * cs, cs)
            bs = jnp.dot(tril, parts[rows, :], preferred_element_type=F32)
            b = bs[:, :kw] + bs[:, kw:2 * kw] + bs[:, 2 * kw:]
            b_last = b[cs - 1:cs, :]
            b_mid = b[cs // 2 - 1:cs // 2, :]
            qm_ref[out, :] = (q[rows, :] * jnp.exp(b - b_mid)).astype(BF16)
            km_ref[out, :] = (k[rows, :] * jnp.exp(b_mid - b)).astype(BF16)
            qe_ref[out, :] = (q[rows, :] * jnp.exp(b)).astype(BF16)
            ke_ref[out, :] = (k[rows, :] * jnp.exp(b_last - b)).astype(BF16)
            el_ref[pl.ds(pi * per_trip + c, 1), :] = jnp.exp(b_last)
        return carry

    lax.fori_loop(0, seq // gate_rows, decay, 0)
    st_ref[...] = jnp.zeros_like(st_ref)

    def chunk(ci, carry):
        r = pl.ds(pl.multiple_of(ci * cs, cs), cs)
        e_last = el_ref[pl.ds(ci, 1), :]
        hs = range(heads)
        ks = [slice(h * dk, (h + 1) * dk) for h in hs]
        vs = [slice(h * dv, (h + 1) * dv) for h in hs]
        attn = [lax.dot_general(qm_ref[r, ks[h]], km_ref[r, ks[h]], _NT, preferred_element_type=F32) for h in hs]
        v = [v_ref[r, vs[h]] for h in hs]
        st = [st_ref[h] for h in hs]
        o_inter = [lax.dot_general(qe_ref[r, ks[h]], st[h].astype(BF16), _NT, preferred_element_type=F32)
                   for h in hs]
        kv = [lax.dot_general(v[h], ke_ref[r, ks[h]], _TN, preferred_element_type=F32) for h in hs]
        attn = [jnp.where(causal, attn[h], 0.0).astype(BF16) for h in hs]
        o = [jnp.dot(attn[h], v[h], preferred_element_type=F32) + o_inter[h] for h in hs]
        for h in hs:
            st_ref[h] = st[h] * e_last[:, ks[h]] + kv[h]
        for h in hs:
            ms = jnp.mean(o[h] * o[h], axis=-1, keepdims=True)
            y = o[h] * lax.rsqrt(ms + EPS) * ng_ref[...]
            o_ref[r, vs[h]] = (y * jax.nn.silu(gr_ref[r, vs[h]].astype(F32))).astype(o_ref.dtype)
        return carry

    lax.fori_loop(0, seq // cs, chunk, 0, unroll=2)


def _gla(proj, lr, w2p, gate_b, norm_g, *, batch, seq, key_w, val_w, gate_rows=_Tiles.gla_decay_rows):
    n = proj.shape[0]
    dk = key_w // GLA_HEADS
    dv = val_w // GLA_HEADS
    vb = 2 * key_w // val_w
    rb = vb + 1
    kern = functools.partial(_gla_kernel, heads=GLA_HEADS, dk=dk, dv=dv, gate_rows=gate_rows)
    return pl.pallas_call(
        kern,
        grid=(batch,),
        in_specs=[
            pl.BlockSpec((seq, key_w), lambda b: (b, 0)),
            pl.BlockSpec((seq, key_w), lambda b: (b, 1)),
            pl.BlockSpec((seq, val_w), lambda b: (b, vb)),
            pl.BlockSpec((seq, val_w), lambda b: (b, rb)),
            pl.BlockSpec((seq, LANES), lambda b: (b, 0)),
            pl.BlockSpec((LANES, key_w), lambda b: (0, 0)),
            pl.BlockSpec((1, key_w), lambda b: (0, 0)),
            pl.BlockSpec((1, dv), lambda b: (0, 0)),
        ],
        out_specs=pl.BlockSpec((seq, val_w), lambda b: (b, 0)),
        out_shape=jax.ShapeDtypeStruct((n, val_w), BF16),
        scratch_shapes=[
            pltpu.VMEM((seq, key_w), BF16),
            pltpu.VMEM((seq, key_w), BF16),
            pltpu.VMEM((seq, key_w), BF16),
            pltpu.VMEM((seq, key_w), BF16),
            pltpu.VMEM((seq // GLA_CHUNK, key_w), F32),
            pltpu.VMEM((GLA_HEADS, dv, dk), F32),
        ],
        compiler_params=_params(("arbitrary",)),
        name="gla",
    )(proj, proj, proj, proj, lr, w2p, gate_b, norm_g)


def _moba_kernel(q_ref, k_ref, v_ref, o_ref, vt_ref, s_ref, pt_ref, *, heads):
    bs = MOBA_BLOCK
    hd = MOBA_HEAD_DIM
    seq = k_ref.shape[0]
    nb = seq // bs
    scale = hd ** -0.5
    hs = range(heads)
    cols = [slice(h * hd, (h + 1) * hd) for h in hs]

    krow = lax.broadcasted_iota(jnp.int32, (bs, bs), 0)
    qcol = lax.broadcasted_iota(jnp.int32, (bs, bs), 1)
    own_bias = jnp.where(krow <= qcol, 0.0, -jnp.inf)
    ones_row = (lax.broadcasted_iota(jnp.int32, (BF16_SUBLANES, seq), 0) == 0).astype(F32).astype(BF16)

    def select(h):
        km = jnp.mean(k_ref[:, cols[h]].astype(F32).reshape(nb, bs, hd), axis=1)
        km = jnp.concatenate([km, jnp.zeros((BF16_SUBLANES - nb, hd), F32)], axis=0).astype(BF16)
        sc = lax.dot_general(km, q_ref[:, cols[h]], _NT, preferred_element_type=F32)[0:nb, :]
        blk = lax.broadcasted_iota(jnp.int32, sc.shape, 0)
        qpos = lax.broadcasted_iota(jnp.int32, sc.shape, 1)
        past = (blk + 1) * bs <= qpos
        sc = jnp.where(past, sc, -jnp.inf)
        beaten = jnp.zeros(sc.shape, F32)
        for jp in range(nb):
            c = sc[jp:jp + 1, :]
            ahead = (c > sc) | ((c == sc) & (blk > jp))
            beaten = beaten + ahead.astype(F32)
        vt_ref[h, 0:hd, :] = v_ref[:, cols[h]].astype(F32).T.astype(BF16)
        vt_ref[h, hd:hd + BF16_SUBLANES, :] = ones_row
        return jnp.where(past & (beaten < MOBA_TOPK), 0.0, -jnp.inf)

    sel_bias = [select(h) for h in hs]

    def scores(h, i):
        nk = (i + 1) * bs
        s_ref[h, i % 2, 0:nk, :] = lax.dot_general(
            k_ref[0:nk, cols[h]], q_ref[i * bs:nk, cols[h]], _NT, preferred_element_type=F32)

    def softmax(h, i):
        slot = i % 2
        qs = slice(i * bs, (i + 1) * bs)
        nk = (i + 1) * bs
        biases = [sel_bias[h][j:j + 1, qs] for j in range(i)]
        m = (s_ref[h, slot, i * bs:nk, :] + own_bias).max(axis=0, keepdims=True)
        for j in range(i):
            m = jnp.maximum(m, s_ref[h, slot, j * bs:(j + 1) * bs, :].max(axis=0, keepdims=True) + biases[j])
        own = s_ref[h, slot, i * bs:nk, :] + own_bias
        pt_ref[h, slot, i * bs:nk, :] = jnp.exp2((own - m) * (scale * LOG2E)).astype(BF16)
        for j in range(i):
            p = jnp.exp2((s_ref[h, slot, j * bs:(j + 1) * bs, :] - (m - biases[j])) * (scale * LOG2E))
            pt_ref[h, slot, j * bs:(j + 1) * bs, :] = p.astype(BF16)

    def attend(h, i):
        nk = (i + 1) * bs
        ot = jnp.dot(vt_ref[h, :, 0:nk], pt_ref[h, i % 2, 0:nk, :], preferred_element_type=F32)
        o_ref[i * bs:nk, cols[h]] = (ot[0:hd, :] / ot[hd:hd + 1, :]).T.astype(o_ref.dtype)

    for h in hs:
        scores(h, 0)
    for i in range(nb):
        for h in hs:
            if i + 1 < nb:
                scores(h, i + 1)
        for h in hs:
            if i > 0:
                attend(h, i - 1)
        for h in hs:
            softmax(h, i)
    for h in hs:
        attend(h, nb - 1)


def _moba(proj, *, batch, seq, col0, width, heads_per_step=_Tiles.moba_heads):
    n = proj.shape[0]
    hd = MOBA_HEAD_DIM
    wb = heads_per_step * hd
    groups = width // wb
    qb = col0 // wb
    kb = qb + groups
    vb = kb + groups
    kern = functools.partial(_moba_kernel, heads=heads_per_step)
    return pl.pallas_call(
        kern,
        grid=(batch, groups),
        in_specs=[
            pl.BlockSpec((seq, wb), lambda b, h: (b, qb + h)),
            pl.BlockSpec((seq, wb), lambda b, h: (b, kb + h)),
            pl.BlockSpec((seq, wb), lambda b, h: (b, vb + h)),
        ],
        out_specs=pl.BlockSpec((seq, wb), lambda b, h: (b, h)),
        out_shape=jax.ShapeDtypeStruct((n, width), BF16),
        scratch_shapes=[
            pltpu.VMEM((heads_per_step, hd + BF16_SUBLANES, seq), BF16),
            pltpu.VMEM((heads_per_step, 2, seq, MOBA_BLOCK), F32),
            pltpu.VMEM((heads_per_step, 2, seq, MOBA_BLOCK), BF16),
        ],
        compiler_params=_params(("arbitrary", "arbitrary")),
        name="moba",
    )(proj, proj, proj)


def _merge_kernel(og_ref, om_ref, wg_ref, wm_ref, g1_ref, g2_ref, wo_ref, x_ref, gt_ref, o_ref):
    yg = jnp.dot(og_ref[...], wg_ref[...], preferred_element_type=F32)
    ym = jnp.dot(om_ref[...], wm_ref[...], preferred_element_type=F32)
    z = jax.nn.sigmoid(g1_ref[...].astype(F32)) * yg + jax.nn.sigmoid(g2_ref[...].astype(F32)) * ym
    y = jnp.dot(z.astype(BF16), wo_ref[...], preferred_element_type=F32)
    o_ref[...] = x_ref[...] + gt_ref[...] * y


def _merge(o_gla, o_moba, w_up_gla, w_up_moba, proj, gate_col0, w_out, x2, mod, layer, *, seq,
           tm=_Tiles.merge_rows):
    n, d = x2.shape
    per_b = seq // tm
    g1b = gate_col0 // d
    kg = o_gla.shape[1]
    km = o_moba.shape[1]
    resident = dict(pipeline_mode=pl.Buffered(1))
    return pl.pallas_call(
        _merge_kernel,
        grid=(n // tm,),
        in_specs=[
            pl.BlockSpec((tm, kg), lambda i: (i, 0)),
            pl.BlockSpec((tm, km), lambda i: (i, 0)),
            pl.BlockSpec((None, kg, d), lambda i: (layer, 0, 0), **resident),
            pl.BlockSpec((None, km, d), lambda i: (layer, 0, 0), **resident),
            pl.BlockSpec((tm, d), lambda i: (i, g1b)),
            pl.BlockSpec((tm, d), lambda i: (i, g1b + 1)),
            pl.BlockSpec((None, d, d), lambda i: (layer, 0, 0), **resident),
            pl.BlockSpec((tm, d), lambda i: (i, 0)),
            pl.BlockSpec((None, None, 1, d), lambda i: (i // per_b, 2, 0, 0)),
        ],
        out_specs=pl.BlockSpec((tm, d), lambda i: (i, 0)),
        out_shape=jax.ShapeDtypeStruct((n, d), F32),
        compiler_params=_params(("arbitrary",)),
        name="merge",
    )(o_gla, o_moba, w_up_gla, w_up_moba, proj, proj, w_out, x2, mod)


def _ffn_kernel(x_ref, g_ref, sc_ref, sh_ref, gt_ref, wa_ref, wu_ref, wo_ref, fg_ref, o_ref, h_ref, *, final_norm):
    j = pl.program_id(1)

    def part(h):
        a = jnp.dot(h, wa_ref[...], preferred_element_type=F32)
        u = jnp.dot(h, wu_ref[...], preferred_element_type=F32)
        return jnp.dot((jax.nn.silu(a) * u).astype(BF16), wo_ref[...], preferred_element_type=F32)

    @pl.when(j == 0)
    def _():
        h = _mod_norm(x_ref[...], g_ref[...], sc_ref[...], sh_ref[...]).astype(BF16)
        h_ref[...] = h
        o_ref[...] = part(h)

    last = pl.num_programs(1) - 1

    @pl.when((j > 0) & (j < last))
    def _():
        o_ref[...] += part(h_ref[...])

    @pl.when(j == last)
    def _():
        y = x_ref[...] + gt_ref[...] * (o_ref[...] + part(h_ref[...]))
        if final_norm:
            ms = jnp.mean(y * y, axis=-1, keepdims=True)
            y = y * lax.rsqrt(ms + EPS) * fg_ref[...]
        o_ref[...] = y


def _ffn(x2, g, mod, w_in, w_out, final_g, layer, *, seq, final_norm,
         tm=_Tiles.ffn_rows, tf=_Tiles.ffn_cols):
    n, d = x2.shape
    dff = w_out.shape[1]
    per_b = seq // tm
    nf = dff // tf
    kern = functools.partial(_ffn_kernel, final_norm=final_norm)
    return pl.pallas_call(
        kern,
        grid=(n // tm, nf),
        in_specs=[
            pl.BlockSpec((tm, d), lambda i, j: (i, 0)),
            pl.BlockSpec((1, d), lambda i, j: (0, 0)),
            pl.BlockSpec((None, None, 1, d), lambda i, j: (i // per_b, 4, 0, 0)),
            pl.BlockSpec((None, None, 1, d), lambda i, j: (i // per_b, 3, 0, 0)),
            pl.BlockSpec((None, None, 1, d), lambda i, j: (i // per_b, 5, 0, 0)),
            pl.BlockSpec((None, d, tf), lambda i, j: (layer, 0, j)),
            pl.BlockSpec((None, d, tf), lambda i, j: (layer, 0, nf + j)),
            pl.BlockSpec((None, tf, d), lambda i, j: (layer, j, 0)),
            pl.BlockSpec((1, d), lambda i, j: (0, 0)),
        ],
        out_specs=pl.BlockSpec((tm, d), lambda i, j: (i, 0)),
        out_shape=jax.ShapeDtypeStruct((n, d), F32),
        scratch_shapes=[pltpu.VMEM((tm, d), BF16)],
        compiler_params=_params(("arbitrary", "arbitrary"), BIG_VMEM_LIMIT_BYTES),
        name="ffn",
    )(x2, g, mod, mod, mod, w_in, w_in, w_out, final_g)


def kernel(x, c, ada_w, ada_b, norm1_g, w_in, gla_gate_w2, gla_gate_b, gla_norm_g, w_up_gla, w_up_moba,
           w_out, norm2_g, w_ffn_in, w_ffn_out, final_g):
    batch, seq, d = x.shape
    depth = ada_w.shape[0]
    rank, key_w = gla_gate_w2.shape[1:]
    val_w = w_up_gla.shape[1]
    moba_w = w_up_moba.shape[1]
    n = batch * seq
    lr0 = 2 * key_w + 2 * val_w
    moba0 = lr0
    gate0 = moba0 + 3 * moba_w

    mod = _adaln(c, ada_w, ada_b).reshape(depth, batch, 6, 1, d)
    w_main, w_lr = _repack_w_in(jnp.swapaxes(w_in, 1, 2), lr0, rank)
    x2 = x.reshape(n, d)
    for l in range(depth):
        w2p = jnp.pad(gla_gate_w2[l], ((0, LANES - rank), (0, 0))).astype(BF16)
        proj, lr, (w_up_gla_b, w_up_moba_b, w_out_b, w_ffn_in_b, w_ffn_out_b) = _in_proj(
            x2, norm1_g[l][None], mod[l], w_main, w_lr, l, [w_up_gla, w_up_moba, w_out, w_ffn_in, w_ffn_out],
            seq=seq)
        o_gla = _gla(proj, lr, w2p, gla_gate_b[l][None], gla_norm_g[l][None],
                     batch=batch, seq=seq, key_w=key_w, val_w=val_w)
        o_moba = _moba(proj, batch=batch, seq=seq, col0=moba0, width=moba_w)
        x2 = _merge(o_gla, o_moba, w_up_gla_b, w_up_moba_b, proj, gate0, w_out_b, x2, mod[l], 0, seq=seq)
        x2 = _ffn(x2, norm2_g[l][None], mod[l], w_ffn_in_b, w_ffn_out_b, final_g[None], 0,
                  seq=seq, final_norm=(l == depth - 1))
    return x2.reshape(batch, seq, d)
```

```python
import functools

import jax
import jax.numpy as jnp
from jax import lax
from jax.experimental import pallas as pl
from jax.experimental.pallas import tpu as pltpu

F32 = jnp.float32
BF16 = jnp.bfloat16

GLA_HEADS = 4
GLA_GATE_NORM = 16.0
GLA_CHUNK = 64
MOBA_HEAD_DIM = 128
MOBA_BLOCK = 256
MOBA_TOPK = 3
EPS = 1e-6
LOG2E = 1.4426950408889634

V7X_VMEM_BYTES = 64 * 1024 * 1024
VMEM_LIMIT_BYTES = V7X_VMEM_BYTES - 8 * 1024 * 1024
BIG_VMEM_LIMIT_BYTES = V7X_VMEM_BYTES - 4 * 1024 * 1024
LANES = 128
BF16_SUBLANES = 16


class _Tiles:
    adaln_cols = 1024
    repack_rows = 1024
    in_proj_rows, in_proj_cols = 1024, 2048
    in_proj_norm_chunks = 4
    gla_decay_rows = 512
    moba_heads = 2
    merge_rows = 512
    ffn_rows, ffn_cols = 1024, 512

_NT = (((1,), (1,)), ((), ()))
_TN = (((0,), (0,)), ((), ()))


def _params(semantics, vmem_limit_bytes=VMEM_LIMIT_BYTES):
    return pltpu.CompilerParams(dimension_semantics=semantics, vmem_limit_bytes=vmem_limit_bytes)


def _mod_norm(x, g, sc, sh):
    ms = jnp.mean(x * x, axis=-1, keepdims=True)
    y = x * lax.rsqrt(ms + EPS) * g
    return y * (1.0 + sc) + sh


def _adaln_kernel(c_ref, w_ref, b_ref, o_ref):
    c_act = jax.nn.silu(c_ref[...])
    o_ref[...] = (
        jnp.dot(c_act.astype(BF16), w_ref[...].astype(BF16), preferred_element_type=F32) + b_ref[...]
    )


def _adaln(c, ada_w, ada_b, *, tn=_Tiles.adaln_cols):
    depth, d, n6 = ada_w.shape
    b = c.shape[0]
    return pl.pallas_call(
        _adaln_kernel,
        grid=(depth, n6 // tn),
        in_specs=[
            pl.BlockSpec((b, d), lambda l, j: (0, 0)),
            pl.BlockSpec((None, d, tn), lambda l, j: (l, 0, j)),
            pl.BlockSpec((None, 1, tn), lambda l, j: (l, 0, j)),
        ],
        out_specs=pl.BlockSpec((None, b, tn), lambda l, j: (l, 0, j)),
        out_shape=jax.ShapeDtypeStruct((depth, b, n6), F32),
        compiler_params=_params(("arbitrary", "arbitrary")),
        name="adaln",
    )(c, ada_w, ada_b.reshape(depth, 1, n6))


def _repack_kernel(a_ref, b_ref, o_ref, ol_ref, *, lr_tile, rank):
    t = pl.program_id(1)

    @pl.when(t < lr_tile)
    def _():
        o_ref[...] = a_ref[...].astype(BF16)

    @pl.when(t >= lr_tile)
    def _():
        o_ref[...] = jnp.concatenate([a_ref[rank:, :], b_ref[...]], axis=0).astype(BF16)

    @pl.when(t == lr_tile)
    def _():
        pad = jnp.zeros((ol_ref.shape[0] - rank, ol_ref.shape[1]), F32)
        ol_ref[...] = jnp.concatenate([a_ref[0:rank, :], pad], axis=0).astype(BF16)


def _repack_w_in(w_in_t, lr0, rank, *, tw=_Tiles.repack_rows):
    depth, c, d = w_in_t.shape
    assert lr0 % tw == 0 and (c - rank) % tw == 0 and tw % rank == 0 and rank % BF16_SUBLANES == 0
    kern = functools.partial(_repack_kernel, lr_tile=lr0 // tw, rank=rank)
    return pl.pallas_call(
        kern,
        grid=(depth, (c - rank) // tw),
        in_specs=[
            pl.BlockSpec((None, tw, d), lambda l, t: (l, t, 0)),
            pl.BlockSpec((None, rank, d), lambda l, t: (l, (t + 1) * (tw // rank), 0)),
        ],
        out_specs=[
            pl.BlockSpec((None, tw, d), lambda l, t: (l, t, 0)),
            pl.BlockSpec((None, LANES, d), lambda l, t: (l, 0, 0)),
        ],
        out_shape=[
            jax.ShapeDtypeStruct((depth, c - rank, d), BF16),
            jax.ShapeDtypeStruct((depth, LANES, d), BF16),
        ],
        compiler_params=_params(("arbitrary", "arbitrary")),
        name="repack_w_in",
    )(w_in_t, w_in_t)


def _cast_rows(rows, steps):
    for r in range(BF16_SUBLANES, rows + 1, BF16_SUBLANES):
        if rows % r == 0 and rows // r <= steps:
            return r
    raise ValueError(f"no bf16-aligned row block covers {rows} rows in {steps} steps")


def _in_proj_kernel(x_ref, g_ref, sc_ref, sh_ref, w_ref, wl_ref, *rest, n_cast, norm_chunks):
    cast_in = rest[:n_cast]
    o_ref, ol_ref = rest[n_cast:n_cast + 2]
    cast_out = rest[n_cast + 2:2 * n_cast + 2]
    h_ref = rest[-1]
    j = pl.program_id(1)

    def side_jobs():
        for src, dst in zip(cast_in, cast_out):
            dst[...] = src[...].astype(BF16)

    @pl.when(j == 0)
    def _():
        rows = x_ref.shape[0] // norm_chunks
        for c in range(norm_chunks):
            r = slice(c * rows, (c + 1) * rows)
            h = _mod_norm(x_ref[r, :], g_ref[...], sc_ref[...], sh_ref[...]).astype(BF16)
            h_ref[r, :] = h
            ol_ref[r, :] = lax.dot_general(h, wl_ref[...], _NT, preferred_element_type=F32).astype(ol_ref.dtype)
            o_ref[r, :] = lax.dot_general(h, w_ref[...], _NT, preferred_element_type=F32).astype(o_ref.dtype)
        side_jobs()

    @pl.when(j > 0)
    def _():
        o_ref[...] = lax.dot_general(h_ref[...], w_ref[...], _NT, preferred_element_type=F32).astype(o_ref.dtype)
        side_jobs()


def _in_proj(x2, g, mod, w_main, w_lr, layer, cast_weights, *, seq,
             tm=_Tiles.in_proj_rows, tn=_Tiles.in_proj_cols):
    n, d = x2.shape
    c = w_main.shape[1]
    per_b = seq // tm
    ncol = c // tn
    steps = (n // tm) * ncol
    cast_in_specs, cast_out_specs, cast_shapes = [], [], []
    casts = [(w, _cast_rows(w.shape[1], steps)) for w in cast_weights]
    for w, rows in casts:
        nblk = w.shape[1] // rows

        def block(i, j, nblk=nblk):
            return jnp.minimum(i * ncol + j, nblk - 1)

        cast_in_specs.append(pl.BlockSpec((None, rows, w.shape[2]), lambda i, j, b=block: (layer, b(i, j), 0)))
        cast_out_specs.append(pl.BlockSpec((None, rows, w.shape[2]), lambda i, j, b=block: (0, b(i, j), 0)))
        cast_shapes.append(jax.ShapeDtypeStruct((1,) + w.shape[1:], BF16))
    kern = functools.partial(_in_proj_kernel, n_cast=len(casts), norm_chunks=_Tiles.in_proj_norm_chunks)
    outs = pl.pallas_call(
        kern,
        grid=(n // tm, ncol),
        in_specs=[
            pl.BlockSpec((tm, d), lambda i, j: (i, 0)),
            pl.BlockSpec((1, d), lambda i, j: (0, 0)),
            pl.BlockSpec((None, None, 1, d), lambda i, j: (i // per_b, 1, 0, 0)),
            pl.BlockSpec((None, None, 1, d), lambda i, j: (i // per_b, 0, 0, 0)),
            pl.BlockSpec((None, tn, d), lambda i, j: (layer, j, 0)),
            pl.BlockSpec((None, LANES, d), lambda i, j: (layer, 0, 0)),
        ] + cast_in_specs,
        out_specs=[
            pl.BlockSpec((tm, tn), lambda i, j: (i, j)),
            pl.BlockSpec((tm, LANES), lambda i, j: (i, 0)),
        ] + cast_out_specs,
        out_shape=[
            jax.ShapeDtypeStruct((n, c), BF16),
            jax.ShapeDtypeStruct((n, LANES), BF16),
        ] + cast_shapes,
        scratch_shapes=[pltpu.VMEM((tm, d), BF16)],
        compiler_params=_params(("arbitrary", "arbitrary"), BIG_VMEM_LIMIT_BYTES),
        name="in_proj",
    )(x2, g, mod, mod, w_main, w_lr, *[w for w, _ in casts])
    return outs[0], outs[1], outs[2:]


def _split3(x):
    hi = x.astype(BF16)
    r1 = x - hi.astype(F32)
    mid = r1.astype(BF16)
    lo = (r1 - mid.astype(F32)).astype(BF16)
    return hi, mid, lo


def _gla_kernel(q_ref, k_ref, v_ref, gr_ref, lr_ref, w2_ref, gb_ref, ng_ref, o_ref,
                qm_ref, km_ref, qe_ref, ke_ref, el_ref, st_ref, *, heads, dk, dv, gate_rows):
    seq = q_ref.shape[0]
    cs = GLA_CHUNK
    kw = heads * dk
    per_trip = gate_rows // cs
    row = lax.broadcasted_iota(jnp.int32, (cs, cs), 0)
    col = lax.broadcasted_iota(jnp.int32, (cs, cs), 1)
    causal = col <= row
    tril = causal.astype(BF16)
    scale = dk ** -0.5

    def decay(pi, carry):
        base = pl.multiple_of(pi * gate_rows, gate_rows)
        r = pl.ds(base, gate_rows)
        xg = jnp.dot(lr_ref[r, :], w2_ref[...], preferred_element_type=F32) + gb_ref[...]
        g = (jnp.minimum(xg, 0.0) - jnp.log(1.0 + jnp.exp(-jnp.abs(xg)))) * (1.0 / GLA_GATE_NORM)
        parts = jnp.concatenate(_split3(g), axis=1)
        q = q_ref[r, :].astype(F32) * scale
        k = k_ref[r, :].astype(F32)
        for c in range(per_trip):
            rows = slice(c * cs, (c + 1) * cs)
            out = pl.ds(base + c * cs, cs)
            bs = jnp.dot(tril, parts[rows, :], preferred_element_type=F32)
            b = bs[:, :kw] + bs[:, kw:2 * kw] + bs[:, 2 * kw:]
            b_last = b[cs - 1:cs, :]
            b_mid = b[cs // 2 - 1:cs // 2, :]
            qm_ref[out, :] = (q[rows, :] * jnp.exp(b - b_mid)).astype(BF16)
            km_ref[out, :] = (k[rows, :] * jnp.exp(b_mid - b)).astype(BF16)
            qe_ref[out, :] = (q[rows, :] * jnp.exp(b)).astype(BF16)
            ke_ref[out, :] = (k[rows, :] * jnp.exp(b_last - b)).astype(BF16)
            el_ref[pl.ds(pi * per_trip + c, 1), :] = jnp.exp(b_last)
        return carry

    lax.fori_loop(0, seq // gate_rows, decay, 0)
    st_ref[...] = jnp.zeros_like(st_ref)

    def chunk(ci, carry):
        r = pl.ds(pl.multiple_of(ci * cs, cs), cs)
        e_last = el_ref[pl.ds(ci, 1), :]
        hs = range(heads)
        ks = [slice(h * dk, (h + 1) * dk) for h in hs]
        vs = [slice(h * dv, (h + 1) * dv) for h in hs]
        attn = [lax.dot_general(qm_ref[r, ks[h]], km_ref[r, ks[h]], _NT, preferred_element_type=F32) for h in hs]
        v = [v_ref[r, vs[h]] for h in hs]
        st = [st_ref[h] for h in hs]
        o_inter = [lax.dot_general(qe_ref[r, ks[h]], st[h].astype(BF16), _NT, preferred_element_type=F32)
                   for h in hs]
        kv = [lax.dot_general(v[h], ke_ref[r, ks[h]], _TN, preferred_element_type=F32) for h in hs]
        attn = [jnp.where(causal, attn[h], 0.0).astype(BF16) for h in hs]
        o = [jnp.dot(attn[h], v[h], preferred_element_type=F32) + o_inter[h] for h in hs]
        for h in hs:
            st_ref[h] = st[h] * e_last[:, ks[h]] + kv[h]
        for h in hs:
            ms = jnp.mean(o[h] * o[h], axis=-1, keepdims=True)
            y = o[h] * lax.rsqrt(ms + EPS) * ng_ref[...]
            o_ref[r, vs[h]] = (y * jax.nn.silu(gr_ref[r, vs[h]].astype(F32))).astype(o_ref.dtype)
        return carry

    lax.fori_loop(0, seq // cs, chunk, 0, unroll=2)


def _gla(proj, lr, w2p, gate_b, norm_g, *, batch, seq, key_w, val_w, gate_rows=_Tiles.gla_decay_rows):
    n = proj.shape[0]
    dk = key_w // GLA_HEADS
    dv = val_w // GLA_HEADS
    vb = 2 * key_w // val_w
    rb = vb + 1
    kern = functools.partial(_gla_kernel, heads=GLA_HEADS, dk=dk, dv=dv, gate_rows=gate_rows)
    return pl.pallas_call(
        kern,
        grid=(batch,),
        in_specs=[
            pl.BlockSpec((seq, key_w), lambda b: (b, 0)),
            pl.BlockSpec((seq, key_w), lambda b: (b, 1)),
            pl.BlockSpec((seq, val_w), lambda b: (b, vb)),
            pl.BlockSpec((seq, val_w), lambda b: (b, rb)),
            pl.BlockSpec((seq, LANES), lambda b: (b, 0)),
            pl.BlockSpec((LANES, key_w), lambda b: (0, 0)),
            pl.BlockSpec((1, key_w), lambda b: (0, 0)),
            pl.BlockSpec((1, dv), lambda b: (0, 0)),
        ],
        out_specs=pl.BlockSpec((seq, val_w), lambda b: (b, 0)),
        out_shape=jax.ShapeDtypeStruct((n, val_w), BF16),
        scratch_shapes=[
            pltpu.VMEM((seq, key_w), BF16),
            pltpu.VMEM((seq, key_w), BF16),
            pltpu.VMEM((seq, key_w), BF16),
            pltpu.VMEM((seq, key_w), BF16),
            pltpu.VMEM((seq // GLA_CHUNK, key_w), F32),
            pltpu.VMEM((GLA_HEADS, dv, dk), F32),
        ],
        compiler_params=_params(("arbitrary",)),
        name="gla",
    )(proj, proj, proj, proj, lr, w2p, gate_b, norm_g)


def _moba_kernel(q_ref, k_ref, v_ref, o_ref, vt_ref, s_ref, pt_ref, *, heads):
    bs = MOBA_BLOCK
    hd = MOBA_HEAD_DIM
    seq = k_ref.shape[0]
    nb = seq // bs
    scale = hd ** -0.5
    hs = range(heads)
    cols = [slice(h * hd, (h + 1) * hd) for h in hs]

    krow = lax.broadcasted_iota(jnp.int32, (bs, bs), 0)
    qcol = lax.broadcasted_iota(jnp.int32, (bs, bs), 1)
    own_bias = jnp.where(krow <= qcol, 0.0, -jnp.inf)
    ones_row = (lax.broadcasted_iota(jnp.int32, (BF16_SUBLANES, seq), 0) == 0).astype(F32).astype(BF16)

    def select(h):
        km = jnp.mean(k_ref[:, cols[h]].astype(F32).reshape(nb, bs, hd), axis=1)
        km = jnp.concatenate([km, jnp.zeros((BF16_SUBLANES - nb, hd), F32)], axis=0).astype(BF16)
        sc = lax.dot_general(km, q_ref[:, cols[h]], _NT, preferred_element_type=F32)[0:nb, :]
        blk = lax.broadcasted_iota(jnp.int32, sc.shape, 0)
        qpos = lax.broadcasted_iota(jnp.int32, sc.shape, 1)
        past = (blk + 1) * bs <= qpos
        sc = jnp.where(past, sc, -jnp.inf)
        beaten = jnp.zeros(sc.shape, F32)
        for jp in range(nb):
            c = sc[jp:jp + 1, :]
            ahead = (c > sc) | ((c == sc) & (blk > jp))
            beaten = beaten + ahead.astype(F32)
        vt_ref[h, 0:hd, :] = v_ref[:, cols[h]].astype(F32).T.astype(BF16)
        vt_ref[h, hd:hd + BF16_SUBLANES, :] = ones_row
        return jnp.where(past & (beaten < MOBA_TOPK), 0.0, -jnp.inf)

    sel_bias = [select(h) for h in hs]

    def scores(h, i):
        nk = (i + 1) * bs
        s_ref[h, i % 2, 0:nk, :] = lax.dot_general(
            k_ref[0:nk, cols[h]], q_ref[i * bs:nk, cols[h]], _NT, preferred_element_type=F32)

    def softmax(h, i):
        slot = i % 2
        qs = slice(i * bs, (i + 1) * bs)
        nk = (i + 1) * bs
        biases = [sel_bias[h][j:j + 1, qs] for j in range(i)]
        m = (s_ref[h, slot, i * bs:nk, :] + own_bias).max(axis=0, keepdims=True)
        for j in range(i):
            m = jnp.maximum(m, s_ref[h, slot, j * bs:(j + 1) * bs, :].max(axis=0, keepdims=True) + biases[j])
        own = s_ref[h, slot, i * bs:nk, :] + own_bias
        pt_ref[h, slot, i * bs:nk, :] = jnp.exp2((own - m) * (scale * LOG2E)).astype(BF16)
        for j in range(i):
            p = jnp.exp2((s_ref[h, slot, j * bs:(j + 1) * bs, :] - (m - biases[j])) * (scale * LOG2E))
            pt_ref[h, slot, j * bs:(j + 1) * bs, :] = p.astype(BF16)

    def attend(h, i):
        nk = (i + 1) * bs
        ot = jnp.dot(vt_ref[h, :, 0:nk], pt_ref[h, i % 2, 0:nk, :], preferred_element_type=F32)
        o_ref[i * bs:nk, cols[h]] = (ot[0:hd, :] / ot[hd:hd + 1, :]).T.astype(o_ref.dtype)

    for h in hs:
        scores(h, 0)
    for i in range(nb):
        for h in hs:
            if i + 1 < nb:
                scores(h, i + 1)
        for h in hs:
            if i > 0:
                attend(h, i - 1)
        for h in hs:
            softmax(h, i)
    for h in hs:
        attend(h, nb - 1)


def _moba(proj, *, batch, seq, col0, width, heads_per_step=_Tiles.moba_heads):
    n = proj.shape[0]
    hd = MOBA_HEAD_DIM
    wb = heads_per_step * hd
    groups = width // wb
    qb = col0 // wb
    kb = qb + groups
    vb = kb + groups
    kern = functools.partial(_moba_kernel, heads=heads_per_step)
    return pl.pallas_call(
        kern,
        grid=(batch, groups),
        in_specs=[
            pl.BlockSpec((seq, wb), lambda b, h: (b, qb + h)),
            pl.BlockSpec((seq, wb), lambda b, h: (b, kb + h)),
            pl.BlockSpec((seq, wb), lambda b, h: (b, vb + h)),
        ],
        out_specs=pl.BlockSpec((seq, wb), lambda b, h: (b, h)),
        out_shape=jax.ShapeDtypeStruct((n, width), BF16),
        scratch_shapes=[
            pltpu.VMEM((heads_per_step, hd + BF16_SUBLANES, seq), BF16),
            pltpu.VMEM((heads_per_step, 2, seq, MOBA_BLOCK), F32),
            pltpu.VMEM((heads_per_step, 2, seq, MOBA_BLOCK), BF16),
        ],
        compiler_params=_params(("arbitrary", "arbitrary")),
        name="moba",
    )(proj, proj, proj)


def _merge_kernel(og_ref, om_ref, wg_ref, wm_ref, g1_ref, g2_ref, wo_ref, x_ref, gt_ref, o_ref):
    yg = jnp.dot(og_ref[...], wg_ref[...], preferred_element_type=F32)
    ym = jnp.dot(om_ref[...], wm_ref[...], preferred_element_type=F32)
    z = jax.nn.sigmoid(g1_ref[...].astype(F32)) * yg + jax.nn.sigmoid(g2_ref[...].astype(F32)) * ym
    y = jnp.dot(z.astype(BF16), wo_ref[...], preferred_element_type=F32)
    o_ref[...] = x_ref[...] + gt_ref[...] * y


def _merge(o_gla, o_moba, w_up_gla, w_up_moba, proj, gate_col0, w_out, x2, mod, layer, *, seq,
           tm=_Tiles.merge_rows):
    n, d = x2.shape
    per_b = seq // tm
    g1b = gate_col0 // d
    kg = o_gla.shape[1]
    km = o_moba.shape[1]
    resident = dict(pipeline_mode=pl.Buffered(1))
    return pl.pallas_call(
        _merge_kernel,
        grid=(n // tm,),
        in_specs=[
            pl.BlockSpec((tm, kg), lambda i: (i, 0)),
            pl.BlockSpec((tm, km), lambda i: (i, 0)),
            pl.BlockSpec((None, kg, d), lambda i: (layer, 0, 0), **resident),
            pl.BlockSpec((None, km, d), lambda i: (layer, 0, 0), **resident),
            pl.BlockSpec((tm, d), lambda i: (i, g1b)),
            pl.BlockSpec((tm, d), lambda i: (i, g1b + 1)),
            pl.BlockSpec((None, d, d), lambda i: (layer, 0, 0), **resident),
            pl.BlockSpec((tm, d), lambda i: (i, 0)),
            pl.BlockSpec((None, None, 1, d), lambda i: (i // per_b, 2, 0, 0)),
        ],
        out_specs=pl.BlockSpec((tm, d), lambda i: (i, 0)),
        out_shape=jax.ShapeDtypeStruct((n, d), F32),
        compiler_params=_params(("arbitrary",)),
        name="merge",
    )(o_gla, o_moba, w_up_gla, w_up_moba, proj, proj, w_out, x2, mod)


def _ffn_kernel(x_ref, g_ref, sc_ref, sh_ref, gt_ref, wa_ref, wu_ref, wo_ref, fg_ref, o_ref, h_ref, *, final_norm):
    j = pl.program_id(1)

    def part(h):
        a = jnp.dot(h, wa_ref[...], preferred_element_type=F32)
        u = jnp.dot(h, wu_ref[...], preferred_element_type=F32)
        return jnp.dot((jax.nn.silu(a) * u).astype(BF16), wo_ref[...], preferred_element_type=F32)

    @pl.when(j == 0)
    def _():
        h = _mod_norm(x_ref[...], g_ref[...], sc_ref[...], sh_ref[...]).astype(BF16)
        h_ref[...] = h
        o_ref[...] = part(h)

    last = pl.num_programs(1) - 1

    @pl.when((j > 0) & (j < last))
    def _():
        o_ref[...] += part(h_ref[...])

    @pl.when(j == last)
    def _():
        y = x_ref[...] + gt_ref[...] * (o_ref[...] + part(h_ref[...]))
        if final_norm:
            ms = jnp.mean(y * y, axis=-1, keepdims=True)
            y = y * lax.rsqrt(ms + EPS) * fg_ref[...]
        o_ref[...] = y


def _ffn(x2, g, mod, w_in, w_out, final_g, layer, *, seq, final_norm,
         tm=_Tiles.ffn_rows, tf=_Tiles.ffn_cols):
    n, d = x2.shape
    dff = w_out.shape[1]
    per_b = seq // tm
    nf = dff // tf
    kern = functools.partial(_ffn_kernel, final_norm=final_norm)
    return pl.pallas_call(
        kern,
        grid=(n // tm, nf),
        in_specs=[
            pl.BlockSpec((tm, d), lambda i, j: (i, 0)),
            pl.BlockSpec((1, d), lambda i, j: (0, 0)),
            pl.BlockSpec((None, None, 1, d), lambda i, j: (i // per_b, 4, 0, 0)),
            pl.BlockSpec((None, None, 1, d), lambda i, j: (i // per_b, 3, 0, 0)),
            pl.BlockSpec((None, None, 1, d), lambda i, j: (i // per_b, 5, 0, 0)),
            pl.BlockSpec((None, d, tf), lambda i, j: (layer, 0, j)),
            pl.BlockSpec((None, d, tf), lambda i, j: (layer, 0, nf + j)),
            pl.BlockSpec((None, tf, d), lambda i, j: (layer, j, 0)),
            pl.BlockSpec((1, d), lambda i, j: (0, 0)),
        ],
        out_specs=pl.BlockSpec((tm, d), lambda i, j: (i, 0)),
        out_shape=jax.ShapeDtypeStruct((n, d), F32),
        scratch_shapes=[pltpu.VMEM((tm, d), BF16)],
        compiler_params=_params(("arbitrary", "arbitrary"), BIG_VMEM_LIMIT_BYTES),
        name="ffn",
    )(x2, g, mod, mod, mod, w_in, w_in, w_out, final_g)


def kernel(x, c, ada_w, ada_b, norm1_g, w_in, gla_gate_w2, gla_gate_b, gla_norm_g, w_up_gla, w_up_moba,
           w_out, norm2_g, w_ffn_in, w_ffn_out, final_g):
    batch, seq, d = x.shape
    depth = ada_w.shape[0]
    rank, key_w = gla_gate_w2.shape[1:]
    val_w = w_up_gla.shape[1]
    moba_w = w_up_moba.shape[1]
    n = batch * seq
    lr0 = 2 * key_w + 2 * val_w
    moba0 = lr0
    gate0 = moba0 + 3 * moba_w

    mod = _adaln(c, ada_w, ada_b).reshape(depth, batch, 6, 1, d)
    w_main, w_lr = _repack_w_in(jnp.swapaxes(w_in, 1, 2), lr0, rank)
    x2 = x.reshape(n, d)
    for l in range(depth):
        w2p = jnp.pad(gla_gate_w2[l], ((0, LANES - rank), (0, 0))).astype(BF16)
        proj, lr, (w_up_gla_b, w_up_moba_b, w_out_b, w_ffn_in_b, w_ffn_out_b) = _in_proj(
            x2, norm1_g[l][None], mod[l], w_main, w_lr, l, [w_up_gla, w_up_moba, w_out, w_ffn_in, w_ffn_out],
            seq=seq)
        o_gla = _gla(proj, lr, w2p, gla_gate_b[l][None], gla_norm_g[l][None],
                     batch=batch, seq=seq, key_w=key_w, val_w=val_w)
        o_moba = _moba(proj, batch=batch, seq=seq, col0=moba0, width=moba_w)
        x2 = _merge(o_gla, o_moba, w_up_gla_b, w_up_moba_b, proj, gate0, w_out_b, x2, mod[l], 0, seq=seq)
        x2 = _ffn(x2, norm2_g[l][None], mod[l], w_ffn_in_b, w_ffn_out_b, final_g[None], 0,
                  seq=seq, final_norm=(l == depth - 1))
    return x2.reshape(batch, seq, d)
```

```python
import functools

import jax
import jax.numpy as jnp
from jax import lax
from jax.experimental import pallas as pl
from jax.experimental.pallas import tpu as pltpu

F32 = jnp.float32
BF16 = jnp.bfloat16

GLA_HEADS = 4
GLA_GATE_NORM = 16.0
GLA_CHUNK = 64
MOBA_HEAD_DIM = 128
MOBA_BLOCK = 256
MOBA_TOPK = 3
EPS = 1e-6
LOG2E = 1.4426950408889634

V7X_VMEM_BYTES = 64 * 1024 * 1024
VMEM_LIMIT_BYTES = V7X_VMEM_BYTES - 8 * 1024 * 1024
BIG_VMEM_LIMIT_BYTES = V7X_VMEM_BYTES - 4 * 1024 * 1024
LANES = 128
BF16_SUBLANES = 16


class _Tiles:
    adaln_cols = 2048
    repack_rows = 1024
    in_proj_rows, in_proj_cols = 1024, 2048
    in_proj_norm_chunks = 4
    gla_decay_rows = 512
    moba_heads = 2
    merge_rows = 512
    ffn_rows, ffn_cols = 1024, 512

_NT = (((1,), (1,)), ((), ()))
_TN = (((0,), (0,)), ((), ()))


def _params(semantics, vmem_limit_bytes=VMEM_LIMIT_BYTES):
    return pltpu.CompilerParams(dimension_semantics=semantics, vmem_limit_bytes=vmem_limit_bytes)


def _mod_norm(x, g, sc, sh):
    ms = jnp.mean(x * x, axis=-1, keepdims=True)
    y = x * lax.rsqrt(ms + EPS) * g
    return y * (1.0 + sc) + sh


def _adaln_kernel(c_ref, w_ref, b_ref, o_ref):
    c_act = jax.nn.silu(c_ref[...])
    o_ref[...] = (
        jnp.dot(c_act.astype(BF16), w_ref[...].astype(BF16), preferred_element_type=F32) + b_ref[...]
    )


def _adaln(c, ada_w, ada_b, *, tn=_Tiles.adaln_cols):
    depth, d, n6 = ada_w.shape
    b = c.shape[0]
    return pl.pallas_call(
        _adaln_kernel,
        grid=(depth, n6 // tn),
        in_specs=[
            pl.BlockSpec((b, d), lambda l, j: (0, 0)),
            pl.BlockSpec((None, d, tn), lambda l, j: (l, 0, j)),
            pl.BlockSpec((None, 1, tn), lambda l, j: (l, 0, j)),
        ],
        out_specs=pl.BlockSpec((None, b, tn), lambda l, j: (l, 0, j)),
        out_shape=jax.ShapeDtypeStruct((depth, b, n6), F32),
        compiler_params=_params(("arbitrary", "arbitrary")),
        name="adaln",
    )(c, ada_w, ada_b.reshape(depth, 1, n6))


def _repack_kernel(a_ref, b_ref, o_ref, ol_ref, *, lr_tile, rank):
    t = pl.program_id(1)

    @pl.when(t < lr_tile)
    def _():
        o_ref[...] = a_ref[...].astype(BF16)

    @pl.when(t >= lr_tile)
    def _():
        o_ref[...] = jnp.concatenate([a_ref[rank:, :], b_ref[...]], axis=0).astype(BF16)

    @pl.when(t == lr_tile)
    def _():
        pad = jnp.zeros((ol_ref.shape[0] - rank, ol_ref.shape[1]), F32)
        ol_ref[...] = jnp.concatenate([a_ref[0:rank, :], pad], axis=0).astype(BF16)


def _repack_w_in(w_in_t, lr0, rank, *, tw=_Tiles.repack_rows):
    depth, c, d = w_in_t.shape
    assert lr0 % tw == 0 and (c - rank) % tw == 0 and tw % rank == 0 and rank % BF16_SUBLANES == 0
    kern = functools.partial(_repack_kernel, lr_tile=lr0 // tw, rank=rank)
    return pl.pallas_call(
        kern,
        grid=(depth, (c - rank) // tw),
        in_specs=[
            pl.BlockSpec((None, tw, d), lambda l, t: (l, t, 0)),
            pl.BlockSpec((None, rank, d), lambda l, t: (l, (t + 1) * (tw // rank), 0)),
        ],
        out_specs=[
            pl.BlockSpec((None, tw, d), lambda l, t: (l, t, 0)),
            pl.BlockSpec((None, LANES, d), lambda l, t: (l, 0, 0)),
        ],
        out_shape=[
            jax.ShapeDtypeStruct((depth, c - rank, d), BF16),
            jax.ShapeDtypeStruct((depth, LANES, d), BF16),
        ],
        compiler_params=_params(("arbitrary", "arbitrary")),
        name="repack_w_in",
    )(w_in_t, w_in_t)


def _cast_rows(rows, steps):
    for r in range(BF16_SUBLANES, rows + 1, BF16_SUBLANES):
        if rows % r == 0 and rows // r <= steps:
            return r
    raise ValueError(f"no bf16-aligned row block covers {rows} rows in {steps} steps")


def _in_proj_kernel(x_ref, g_ref, sc_ref, sh_ref, w_ref, wl_ref, *rest, n_cast, norm_chunks):
    cast_in = rest[:n_cast]
    o_ref, ol_ref = rest[n_cast:n_cast + 2]
    cast_out = rest[n_cast + 2:2 * n_cast + 2]
    h_ref = rest[-1]
    j = pl.program_id(1)

    def side_jobs():
        for src, dst in zip(cast_in, cast_out):
            dst[...] = src[...].astype(BF16)

    @pl.when(j == 0)
    def _():
        rows = x_ref.shape[0] // norm_chunks
        for c in range(norm_chunks):
            r = slice(c * rows, (c + 1) * rows)
            h = _mod_norm(x_ref[r, :], g_ref[...], sc_ref[...], sh_ref[...]).astype(BF16)
            h_ref[r, :] = h
            ol_ref[r, :] = lax.dot_general(h, wl_ref[...], _NT, preferred_element_type=F32).astype(ol_ref.dtype)
            o_ref[r, :] = lax.dot_general(h, w_ref[...], _NT, preferred_element_type=F32).astype(o_ref.dtype)
        side_jobs()

    @pl.when(j > 0)
    def _():
        o_ref[...] = lax.dot_general(h_ref[...], w_ref[...], _NT, preferred_element_type=F32).astype(o_ref.dtype)
        side_jobs()


def _in_proj(x2, g, mod, w_main, w_lr, layer, cast_weights, *, seq,
             tm=_Tiles.in_proj_rows, tn=_Tiles.in_proj_cols):
    n, d = x2.shape
    c = w_main.shape[1]
    assert seq % tm == 0 and c % tn == 0 and tm % (_Tiles.in_proj_norm_chunks * BF16_SUBLANES) == 0
    per_b = seq // tm
    ncol = c // tn
    steps = (n // tm) * ncol
    cast_in_specs, cast_out_specs, cast_shapes = [], [], []
    casts = [(w, _cast_rows(w.shape[1], steps)) for w in cast_weights]
    for w, rows in casts:
        nblk = w.shape[1] // rows

        def block(i, j, nblk=nblk):
            return jnp.minimum(i * ncol + j, nblk - 1)

        cast_in_specs.append(pl.BlockSpec((None, rows, w.shape[2]), lambda i, j, b=block: (layer, b(i, j), 0)))
        cast_out_specs.append(pl.BlockSpec((None, rows, w.shape[2]), lambda i, j, b=block: (0, b(i, j), 0)))
        cast_shapes.append(jax.ShapeDtypeStruct((1,) + w.shape[1:], BF16))
    kern = functools.partial(_in_proj_kernel, n_cast=len(casts), norm_chunks=_Tiles.in_proj_norm_chunks)
    outs = pl.pallas_call(
        kern,
        grid=(n // tm, ncol),
        in_specs=[
            pl.BlockSpec((tm, d), lambda i, j: (i, 0)),
            pl.BlockSpec((1, d), lambda i, j: (0, 0)),
            pl.BlockSpec((None, None, 1, d), lambda i, j: (i // per_b, 1, 0, 0)),
            pl.BlockSpec((None, None, 1, d), lambda i, j: (i // per_b, 0, 0, 0)),
            pl.BlockSpec((None, tn, d), lambda i, j: (layer, j, 0)),
            pl.BlockSpec((None, LANES, d), lambda i, j: (layer, 0, 0)),
        ] + cast_in_specs,
        out_specs=[
            pl.BlockSpec((tm, tn), lambda i, j: (i, j)),
            pl.BlockSpec((tm, LANES), lambda i, j: (i, 0)),
        ] + cast_out_specs,
        out_shape=[
            jax.ShapeDtypeStruct((n, c), BF16),
            jax.ShapeDtypeStruct((n, LANES), BF16),
        ] + cast_shapes,
        scratch_shapes=[pltpu.VMEM((tm, d), BF16)],
        compiler_params=_params(("arbitrary", "arbitrary"), BIG_VMEM_LIMIT_BYTES),
        name="in_proj",
    )(x2, g, mod, mod, w_main, w_lr, *[w for w, _ in casts])
    return outs[0], outs[1], outs[2:]


def _split3(x):
    hi = x.astype(BF16)
    r1 = x - hi.astype(F32)
    mid = r1.astype(BF16)
    lo = (r1 - mid.astype(F32)).astype(BF16)
    return hi, mid, lo


def _gla_kernel(q_ref, k_ref, v_ref, gr_ref, lr_ref, w2_ref, gb_ref, ng_ref, o_ref,
                qm_ref, km_ref, qe_ref, ke_ref, el_ref, st_ref, *, heads, dk, dv, gate_rows):
    seq = q_ref.shape[0]
    cs = GLA_CHUNK
    kw = heads * dk
    per_trip = gate_rows // cs
    row = lax.broadcasted_iota(jnp.int32, (cs, cs), 0)
    col = lax.broadcasted_iota(jnp.int32, (cs, cs), 1)
    causal = col <= row
    tril = causal.astype(BF16)
    scale = dk ** -0.5

    def decay(pi, carry):
        base = pl.multiple_of(pi * gate_rows, gate_rows)
        r = pl.ds(base, gate_rows)
        xg = jnp.dot(lr_ref[r, :], w2_ref[...], preferred_element_type=F32) + gb_ref[...]
        g = (jnp.minimum(xg, 0.0) - jnp.log(1.0 + jnp.exp(-jnp.abs(xg)))) * (1.0 / GLA_GATE_NORM)
        parts = jnp.concatenate(_split3(g), axis=1)
        q = q_ref[r, :].astype(F32) * scale
        k = k_ref[r, :].astype(F32)
        for c in range(per_trip):
            rows = slice(c * cs, (c + 1) * cs)
            out = pl.ds(base + c * cs, cs)
            bs = jnp.dot(tril, parts[rows, :], preferred_element_type=F32)
            b = bs[:, :kw] + bs[:, kw:2 * kw] + bs[:, 2 * kw:]
            b_last = b[cs - 1:cs, :]
            b_mid = b[cs // 2 - 1:cs // 2, :]
            qm_ref[out, :] = (q[rows, :] * jnp.exp(b - b_mid)).astype(BF16)
            km_ref[out, :] = (k[rows, :] * jnp.exp(b_mid - b)).astype(BF16)
            qe_ref[out, :] = (q[rows, :] * jnp.exp(b)).astype(BF16)
            ke_ref[out, :] = (k[rows, :] * jnp.exp(b_last - b)).astype(BF16)
            el_ref[pl.ds(pi * per_trip + c, 1), :] = jnp.exp(b_last)
        return carry

    lax.fori_loop(0, seq // gate_rows, decay, 0)
    st_ref[...] = jnp.zeros_like(st_ref)

    def chunk(ci, carry):
        r = pl.ds(pl.multiple_of(ci * cs, cs), cs)
        e_last = el_ref[pl.ds(ci, 1), :]
        hs = range(heads)
        ks = [slice(h * dk, (h + 1) * dk) for h in hs]
        vs = [slice(h * dv, (h + 1) * dv) for h in hs]
        attn = [lax.dot_general(qm_ref[r, ks[h]], km_ref[r, ks[h]], _NT, preferred_element_type=F32) for h in hs]
        v = [v_ref[r, vs[h]] for h in hs]
        st = [st_ref[h] for h in hs]
        o_inter = [lax.dot_general(qe_ref[r, ks[h]], st[h].astype(BF16), _NT, preferred_element_type=F32)
                   for h in hs]
        kv = [lax.dot_general(v[h], ke_ref[r, ks[h]], _TN, preferred_element_type=F32) for h in hs]
        attn = [jnp.where(causal, attn[h], 0.0).astype(BF16) for h in hs]
        o = [jnp.dot(attn[h], v[h], preferred_element_type=F32) + o_inter[h] for h in hs]
        for h in hs:
            st_ref[h] = st[h] * e_last[:, ks[h]] + kv[h]
        for h in hs:
            ms = jnp.mean(o[h] * o[h], axis=-1, keepdims=True)
            y = o[h] * lax.rsqrt(ms + EPS) * ng_ref[...]
            o_ref[r, vs[h]] = (y * jax.nn.silu(gr_ref[r, vs[h]].astype(F32))).astype(o_ref.dtype)
        return carry

    lax.fori_loop(0, seq // cs, chunk, 0, unroll=2)


def _gla(proj, lr, w2p, gate_b, norm_g, *, batch, seq, key_w, val_w, gate_rows=_Tiles.gla_decay_rows):
    n = proj.shape[0]
    dk = key_w // GLA_HEADS
    dv = val_w // GLA_HEADS
    assert seq % gate_rows == 0 and gate_rows % GLA_CHUNK == 0 and (2 * key_w) % val_w == 0 and dk == LANES
    vb = 2 * key_w // val_w
    rb = vb + 1
    kern = functools.partial(_gla_kernel, heads=GLA_HEADS, dk=dk, dv=dv, gate_rows=gate_rows)
    return pl.pallas_call(
        kern,
        grid=(batch,),
        in_specs=[
            pl.BlockSpec((seq, key_w), lambda b: (b, 0)),
            pl.BlockSpec((seq, key_w), lambda b: (b, 1)),
            pl.BlockSpec((seq, val_w), lambda b: (b, vb)),
            pl.BlockSpec((seq, val_w), lambda b: (b, rb)),
            pl.BlockSpec((seq, LANES), lambda b: (b, 0)),
            pl.BlockSpec((LANES, key_w), lambda b: (0, 0)),
            pl.BlockSpec((1, key_w), lambda b: (0, 0)),
            pl.BlockSpec((1, dv), lambda b: (0, 0)),
        ],
        out_specs=pl.BlockSpec((seq, val_w), lambda b: (b, 0)),
        out_shape=jax.ShapeDtypeStruct((n, val_w), BF16),
        scratch_shapes=[
            pltpu.VMEM((seq, key_w), BF16),
            pltpu.VMEM((seq, key_w), BF16),
            pltpu.VMEM((seq, key_w), BF16),
            pltpu.VMEM((seq, key_w), BF16),
            pltpu.VMEM((seq // GLA_CHUNK, key_w), F32),
            pltpu.VMEM((GLA_HEADS, dv, dk), F32),
        ],
        compiler_params=_params(("arbitrary",)),
        name="gla",
    )(proj, proj, proj, proj, lr, w2p, gate_b, norm_g)


def _moba_kernel(q_ref, k_ref, v_ref, o_ref, vt_ref, s_ref, pt_ref, *, heads):
    bs = MOBA_BLOCK
    hd = MOBA_HEAD_DIM
    seq = k_ref.shape[0]
    nb = seq // bs
    scale = hd ** -0.5
    hs = range(heads)
    cols = [slice(h * hd, (h + 1) * hd) for h in hs]

    krow = lax.broadcasted_iota(jnp.int32, (bs, bs), 0)
    qcol = lax.broadcasted_iota(jnp.int32, (bs, bs), 1)
    own_bias = jnp.where(krow <= qcol, 0.0, -jnp.inf)
    ones_row = (lax.broadcasted_iota(jnp.int32, (BF16_SUBLANES, seq), 0) == 0).astype(F32).astype(BF16)

    def select(h):
        km = jnp.mean(k_ref[:, cols[h]].astype(F32).reshape(nb, bs, hd), axis=1)
        km = jnp.concatenate([km, jnp.zeros((BF16_SUBLANES - nb, hd), F32)], axis=0).astype(BF16)
        sc = lax.dot_general(km, q_ref[:, cols[h]], _NT, preferred_element_type=F32)[0:nb, :]
        blk = lax.broadcasted_iota(jnp.int32, sc.shape, 0)
        qpos = lax.broadcasted_iota(jnp.int32, sc.shape, 1)
        past = (blk + 1) * bs <= qpos
        sc = jnp.where(past, sc, -jnp.inf)
        beaten = jnp.zeros(sc.shape, F32)
        for jp in range(nb):
            c = sc[jp:jp + 1, :]
            ahead = (c > sc) | ((c == sc) & (blk > jp))
            beaten = beaten + ahead.astype(F32)
        vt_ref[h, 0:hd, :] = v_ref[:, cols[h]].astype(F32).T.astype(BF16)
        vt_ref[h, hd:hd + BF16_SUBLANES, :] = ones_row
        return jnp.where(past & (beaten < MOBA_TOPK), 0.0, -jnp.inf)

    sel_bias = [select(h) for h in hs]

    def scores(h, i):
        nk = (i + 1) * bs
        s_ref[h, i % 2, 0:nk, :] = lax.dot_general(
            k_ref[0:nk, cols[h]], q_ref[i * bs:nk, cols[h]], _NT, preferred_element_type=F32)

    def softmax(h, i):
        slot = i % 2
        qs = slice(i * bs, (i + 1) * bs)
        nk = (i + 1) * bs
        biases = [sel_bias[h][j:j + 1, qs] for j in range(i)]
        m = (s_ref[h, slot, i * bs:nk, :] + own_bias).max(axis=0, keepdims=True)
        for j in range(i):
            m = jnp.maximum(m, s_ref[h, slot, j * bs:(j + 1) * bs, :].max(axis=0, keepdims=True) + biases[j])
        own = s_ref[h, slot, i * bs:nk, :] + own_bias
        pt_ref[h, slot, i * bs:nk, :] = jnp.exp2((own - m) * (scale * LOG2E)).astype(BF16)
        for j in range(i):
            p = jnp.exp2((s_ref[h, slot, j * bs:(j + 1) * bs, :] - (m - biases[j])) * (scale * LOG2E))
            pt_ref[h, slot, j * bs:(j + 1) * bs, :] = p.astype(BF16)

    def attend(h, i):
        nk = (i + 1) * bs
        ot = jnp.dot(vt_ref[h, :, 0:nk], pt_ref[h, i % 2, 0:nk, :], preferred_element_type=F32)
        o_ref[i * bs:nk, cols[h]] = (ot[0:hd, :] / ot[hd:hd + 1, :]).T.astype(o_ref.dtype)

    for h in hs:
        scores(h, 0)
    for i in range(nb):
        for h in hs:
            if i + 1 < nb:
                scores(h, i + 1)
        for h in hs:
            if i > 0:
                attend(h, i - 1)
        for h in hs:
            softmax(h, i)
    for h in hs:
        attend(h, nb - 1)


def _moba(proj, *, batch, seq, col0, width, heads_per_step=_Tiles.moba_heads):
    n = proj.shape[0]
    hd = MOBA_HEAD_DIM
    wb = heads_per_step * hd
    assert seq % MOBA_BLOCK == 0 and seq // MOBA_BLOCK <= BF16_SUBLANES and width % wb == 0 and col0 % wb == 0
    groups = width // wb
    qb = col0 // wb
    kb = qb + groups
    vb = kb + groups
    kern = functools.partial(_moba_kernel, heads=heads_per_step)
    return pl.pallas_call(
        kern,
        grid=(batch, groups),
        in_specs=[
            pl.BlockSpec((seq, wb), lambda b, h: (b, qb + h)),
            pl.BlockSpec((seq, wb), lambda b, h: (b, kb + h)),
            pl.BlockSpec((seq, wb), lambda b, h: (b, vb + h)),
        ],
        out_specs=pl.BlockSpec((seq, wb), lambda b, h: (b, h)),
        out_shape=jax.ShapeDtypeStruct((n, width), BF16),
        scratch_shapes=[
            pltpu.VMEM((heads_per_step, hd + BF16_SUBLANES, seq), BF16),
            pltpu.VMEM((heads_per_step, 2, seq, MOBA_BLOCK), F32),
            pltpu.VMEM((heads_per_step, 2, seq, MOBA_BLOCK), BF16),
        ],
        compiler_params=_params(("arbitrary", "arbitrary")),
        name="moba",
    )(proj, proj, proj)


def _merge_kernel(og_ref, om_ref, wg_ref, wm_ref, g1_ref, g2_ref, wo_ref, x_ref, gt_ref, o_ref):
    yg = jnp.dot(og_ref[...], wg_ref[...], preferred_element_type=F32)
    ym = jnp.dot(om_ref[...], wm_ref[...], preferred_element_type=F32)
    z = jax.nn.sigmoid(g1_ref[...].astype(F32)) * yg + jax.nn.sigmoid(g2_ref[...].astype(F32)) * ym
    y = jnp.dot(z.astype(BF16), wo_ref[...], preferred_element_type=F32)
    o_ref[...] = x_ref[...] + gt_ref[...] * y


def _merge(o_gla, o_moba, w_up_gla, w_up_moba, proj, gate_col0, w_out, x2, mod, layer, *, seq,
           tm=_Tiles.merge_rows):
    n, d = x2.shape
    assert seq % tm == 0 and gate_col0 % d == 0
    per_b = seq // tm
    g1b = gate_col0 // d
    kg = o_gla.shape[1]
    km = o_moba.shape[1]
    resident = dict(pipeline_mode=pl.Buffered(1))
    return pl.pallas_call(
        _merge_kernel,
        grid=(n // tm,),
        in_specs=[
            pl.BlockSpec((tm, kg), lambda i: (i, 0)),
            pl.BlockSpec((tm, km), lambda i: (i, 0)),
            pl.BlockSpec((None, kg, d), lambda i: (layer, 0, 0), **resident),
            pl.BlockSpec((None, km, d), lambda i: (layer, 0, 0), **resident),
            pl.BlockSpec((tm, d), lambda i: (i, g1b)),
            pl.BlockSpec((tm, d), lambda i: (i, g1b + 1)),
            pl.BlockSpec((None, d, d), lambda i: (layer, 0, 0), **resident),
            pl.BlockSpec((tm, d), lambda i: (i, 0)),
            pl.BlockSpec((None, None, 1, d), lambda i: (i // per_b, 2, 0, 0)),
        ],
        out_specs=pl.BlockSpec((tm, d), lambda i: (i, 0)),
        out_shape=jax.ShapeDtypeStruct((n, d), F32),
        compiler_params=_params(("arbitrary",)),
        name="merge",
    )(o_gla, o_moba, w_up_gla, w_up_moba, proj, proj, w_out, x2, mod)


def _ffn_kernel(x_ref, g_ref, sc_ref, sh_ref, gt_ref, wa_ref, wu_ref, wo_ref, fg_ref, o_ref, h_ref, *, final_norm):
    j = pl.program_id(1)

    def part(h):
        a = jnp.dot(h, wa_ref[...], preferred_element_type=F32)
        u = jnp.dot(h, wu_ref[...], preferred_element_type=F32)
        return jnp.dot((jax.nn.silu(a) * u).astype(BF16), wo_ref[...], preferred_element_type=F32)

    @pl.when(j == 0)
    def _():
        h = _mod_norm(x_ref[...], g_ref[...], sc_ref[...], sh_ref[...]).astype(BF16)
        h_ref[...] = h
        o_ref[...] = part(h)

    last = pl.num_programs(1) - 1

    @pl.when((j > 0) & (j < last))
    def _():
        o_ref[...] += part(h_ref[...])

    @pl.when(j == last)
    def _():
        y = x_ref[...] + gt_ref[...] * (o_ref[...] + part(h_ref[...]))
        if final_norm:
            ms = jnp.mean(y * y, axis=-1, keepdims=True)
            y = y * lax.rsqrt(ms + EPS) * fg_ref[...]
        o_ref[...] = y


def _ffn(x2, g, mod, w_in, w_out, final_g, layer, *, seq, final_norm,
         tm=_Tiles.ffn_rows, tf=_Tiles.ffn_cols):
    n, d = x2.shape
    dff = w_out.shape[1]
    per_b = seq // tm
    nf = dff // tf
    assert seq % tm == 0 and dff % tf == 0
    assert nf >= 2
    kern = functools.partial(_ffn_kernel, final_norm=final_norm)
    return pl.pallas_call(
        kern,
        grid=(n // tm, nf),
        in_specs=[
            pl.BlockSpec((tm, d), lambda i, j: (i, 0)),
            pl.BlockSpec((1, d), lambda i, j: (0, 0)),
            pl.BlockSpec((None, None, 1, d), lambda i, j: (i // per_b, 4, 0, 0)),
            pl.BlockSpec((None, None, 1, d), lambda i, j: (i // per_b, 3, 0, 0)),
            pl.BlockSpec((None, None, 1, d), lambda i, j: (i // per_b, 5, 0, 0)),
            pl.BlockSpec((None, d, tf), lambda i, j: (layer, 0, j)),
            pl.BlockSpec((None, d, tf), lambda i, j: (layer, 0, nf + j)),
            pl.BlockSpec((None, tf, d), lambda i, j: (layer, j, 0)),
            pl.BlockSpec((1, d), lambda i, j: (0, 0)),
        ],
        out_specs=pl.BlockSpec((tm, d), lambda i, j: (i, 0)),
        out_shape=jax.ShapeDtypeStruct((n, d), F32),
        scratch_shapes=[pltpu.VMEM((tm, d), BF16)],
        compiler_params=_params(("arbitrary", "arbitrary"), BIG_VMEM_LIMIT_BYTES),
        name="ffn",
    )(x2, g, mod, mod, mod, w_in, w_in, w_out, final_g)


def kernel(x, c, ada_w, ada_b, norm1_g, w_in, gla_gate_w2, gla_gate_b, gla_norm_g, w_up_gla, w_up_moba,
           w_out, norm2_g, w_ffn_in, w_ffn_out, final_g):
    batch, seq, d = x.shape
    depth = ada_w.shape[0]
    rank, key_w = gla_gate_w2.shape[1:]
    val_w = w_up_gla.shape[1]
    moba_w = w_up_moba.shape[1]
    n = batch * seq
    lr0 = 2 * key_w + 2 * val_w
    moba0 = lr0
    gate0 = moba0 + 3 * moba_w

    mod = _adaln(c, ada_w, ada_b).reshape(depth, batch, 6, 1, d)
    w_main, w_lr = _repack_w_in(jnp.swapaxes(w_in, 1, 2), lr0, rank)
    x2 = x.reshape(n, d)
    for l in range(depth):
        w2p = jnp.pad(gla_gate_w2[l], ((0, LANES - rank), (0, 0))).astype(BF16)
        proj, lr, (w_up_gla_b, w_up_moba_b, w_out_b, w_ffn_in_b, w_ffn_out_b) = _in_proj(
            x2, norm1_g[l][None], mod[l], w_main, w_lr, l, [w_up_gla, w_up_moba, w_out, w_ffn_in, w_ffn_out],
            seq=seq)
        o_gla = _gla(proj, lr, w2p, gla_gate_b[l][None], gla_norm_g[l][None],
                     batch=batch, seq=seq, key_w=key_w, val_w=val_w)
        o_moba = _moba(proj, batch=batch, seq=seq, col0=moba0, width=moba_w)
        x2 = _merge(o_gla, o_moba, w_up_gla_b, w_up_moba_b, proj, gate0, w_out_b, x2, mod[l], 0, seq=seq)
        x2 = _ffn(x2, norm2_g[l][None], mod[l], w_ffn_in_b, w_ffn_out_b, final_g[None], 0,
                  seq=seq, final_norm=(l == depth - 1))
    return x2.reshape(batch, seq, d)
```

```python
import functools

import jax
import jax.numpy as jnp
from jax import lax
from jax.experimental import pallas as pl
from jax.experimental.pallas import tpu as pltpu

F32 = jnp.float32
BF16 = jnp.bfloat16

GLA_HEADS = 4
GLA_GATE_NORM = 16.0
GLA_CHUNK = 64
MOBA_HEAD_DIM = 128
MOBA_BLOCK = 256
MOBA_TOPK = 3
EPS = 1e-6
LOG2E = 1.4426950408889634

V7X_VMEM_BYTES = 64 * 1024 * 1024
VMEM_LIMIT_BYTES = V7X_VMEM_BYTES - 8 * 1024 * 1024
BIG_VMEM_LIMIT_BYTES = V7X_VMEM_BYTES - 4 * 1024 * 1024
LANES = 128
BF16_SUBLANES = 16


class _Tiles:
    adaln_cols = 1024
    repack_rows = 1024
    in_proj_rows, in_proj_cols = 1024, 2048
    in_proj_norm_chunks = 4
    gla_decay_rows = 512
    moba_heads = 2
    merge_rows = 512
    ffn_rows, ffn_cols = 1024, 512

_NT = (((1,), (1,)), ((), ()))
_TN = (((0,), (0,)), ((), ()))


def _params(semantics, vmem_limit_bytes=VMEM_LIMIT_BYTES):
    return pltpu.CompilerParams(dimension_semantics=semantics, vmem_limit_bytes=vmem_limit_bytes)


def _mod_norm(x, g, sc, sh):
    ms = jnp.mean(x * x, axis=-1, keepdims=True)
    y = x * lax.rsqrt(ms + EPS) * g
    return y * (1.0 + sc) + sh


def _adaln_kernel(c_ref, w_ref, b_ref, o_ref):
    c_act = jax.nn.silu(c_ref[...])
    o_ref[...] = (
        jnp.dot(c_act.astype(BF16), w_ref[...].astype(BF16), preferred_element_type=F32) + b_ref[...]
    )


def _adaln(c, ada_w, ada_b, *, tn=_Tiles.adaln_cols):
    depth, d, n6 = ada_w.shape
    b = c.shape[0]
    return pl.pallas_call(
        _adaln_kernel,
        grid=(depth, n6 // tn),
        in_specs=[
            pl.BlockSpec((b, d), lambda l, j: (0, 0)),
            pl.BlockSpec((None, d, tn), lambda l, j: (l, 0, j)),
            pl.BlockSpec((None, 1, tn), lambda l, j: (l, 0, j)),
        ],
        out_specs=pl.BlockSpec((None, b, tn), lambda l, j: (l, 0, j)),
        out_shape=jax.ShapeDtypeStruct((depth, b, n6), F32),
        compiler_params=_params(("arbitrary", "arbitrary")),
        name="adaln",
    )(c, ada_w, ada_b.reshape(depth, 1, n6))


def _repack_kernel(a_ref, b_ref, o_ref, ol_ref, *, lr_tile, rank):
    t = pl.program_id(1)

    @pl.when(t < lr_tile)
    def _():
        o_ref[...] = a_ref[...].astype(BF16)

    @pl.when(t >= lr_tile)
    def _():
        o_ref[...] = jnp.concatenate([a_ref[rank:, :], b_ref[...]], axis=0).astype(BF16)

    @pl.when(t == lr_tile)
    def _():
        pad = jnp.zeros((ol_ref.shape[0] - rank, ol_ref.shape[1]), F32)
        ol_ref[...] = jnp.concatenate([a_ref[0:rank, :], pad], axis=0).astype(BF16)


def _repack_w_in(w_in_t, lr0, rank, *, tw=_Tiles.repack_rows):
    depth, c, d = w_in_t.shape
    assert lr0 % tw == 0 and (c - rank) % tw == 0 and tw % rank == 0 and rank % BF16_SUBLANES == 0
    kern = functools.partial(_repack_kernel, lr_tile=lr0 // tw, rank=rank)
    return pl.pallas_call(
        kern,
        grid=(depth, (c - rank) // tw),
        in_specs=[
            pl.BlockSpec((None, tw, d), lambda l, t: (l, t, 0)),
            pl.BlockSpec((None, rank, d), lambda l, t: (l, (t + 1) * (tw // rank), 0)),
        ],
        out_specs=[
            pl.BlockSpec((None, tw, d), lambda l, t: (l, t, 0)),
            pl.BlockSpec((None, LANES, d), lambda l, t: (l, 0, 0)),
        ],
        out_shape=[
            jax.ShapeDtypeStruct((depth, c - rank, d), BF16),
            jax.ShapeDtypeStruct((depth, LANES, d), BF16),
        ],
        compiler_params=_params(("arbitrary", "arbitrary")),
        name="repack_w_in",
    )(w_in_t, w_in_t)


def _cast_rows(rows, steps):
    for r in range(BF16_SUBLANES, rows + 1, BF16_SUBLANES):
        if rows % r == 0 and rows // r <= steps:
            return r
    raise ValueError(f"no bf16-aligned row block covers {rows} rows in {steps} steps")


def _in_proj_kernel(x_ref, g_ref, sc_ref, sh_ref, w_ref, wl_ref, *rest, n_cast, norm_chunks):
    cast_in = rest[:n_cast]
    o_ref, ol_ref = rest[n_cast:n_cast + 2]
    cast_out = rest[n_cast + 2:2 * n_cast + 2]
    h_ref = rest[-1]
    j = pl.program_id(1)

    def side_jobs():
        for src, dst in zip(cast_in, cast_out):
            dst[...] = src[...].astype(BF16)

    @pl.when(j == 0)
    def _():
        rows = x_ref.shape[0] // norm_chunks
        for c in range(norm_chunks):
            r = slice(c * rows, (c + 1) * rows)
            h = _mod_norm(x_ref[r, :], g_ref[...], sc_ref[...], sh_ref[...]).astype(BF16)
            h_ref[r, :] = h
            ol_ref[r, :] = lax.dot_general(h, wl_ref[...], _NT, preferred_element_type=F32).astype(ol_ref.dtype)
            o_ref[r, :] = lax.dot_general(h, w_ref[...], _NT, preferred_element_type=F32).astype(o_ref.dtype)
        side_jobs()

    @pl.when(j > 0)
    def _():
        o_ref[...] = lax.dot_general(h_ref[...], w_ref[...], _NT, preferred_element_type=F32).astype(o_ref.dtype)
        side_jobs()


def _in_proj(x2, g, mod, w_main, w_lr, layer, cast_weights, *, seq,
             tm=_Tiles.in_proj_rows, tn=_Tiles.in_proj_cols):
    n, d = x2.shape
    c = w_main.shape[1]
    assert seq % tm == 0 and c % tn == 0 and tm % (_Tiles.in_proj_norm_chunks * BF16_SUBLANES) == 0
    per_b = seq // tm
    ncol = c // tn
    steps = (n // tm) * ncol
    cast_in_specs, cast_out_specs, cast_shapes = [], [], []
    casts = [(w, _cast_rows(w.shape[1], steps)) for w in cast_weights]
    for w, rows in casts:
        nblk = w.shape[1] // rows

        def block(i, j, nblk=nblk):
            return jnp.minimum(i * ncol + j, nblk - 1)

        cast_in_specs.append(pl.BlockSpec((None, rows, w.shape[2]), lambda i, j, b=block: (layer, b(i, j), 0)))
        cast_out_specs.append(pl.BlockSpec((None, rows, w.shape[2]), lambda i, j, b=block: (0, b(i, j), 0)))
        cast_shapes.append(jax.ShapeDtypeStruct((1,) + w.shape[1:], BF16))
    kern = functools.partial(_in_proj_kernel, n_cast=len(casts), norm_chunks=_Tiles.in_proj_norm_chunks)
    outs = pl.pallas_call(
        kern,
        grid=(n // tm, ncol),
        in_specs=[
            pl.BlockSpec((tm, d), lambda i, j: (i, 0)),
            pl.BlockSpec((1, d), lambda i, j: (0, 0)),
            pl.BlockSpec((None, None, 1, d), lambda i, j: (i // per_b, 1, 0, 0)),
            pl.BlockSpec((None, None, 1, d), lambda i, j: (i // per_b, 0, 0, 0)),
            pl.BlockSpec((None, tn, d), lambda i, j: (layer, j, 0)),
            pl.BlockSpec((None, LANES, d), lambda i, j: (layer, 0, 0)),
        ] + cast_in_specs,
        out_specs=[
            pl.BlockSpec((tm, tn), lambda i, j: (i, j)),
            pl.BlockSpec((tm, LANES), lambda i, j: (i, 0)),
        ] + cast_out_specs,
        out_shape=[
            jax.ShapeDtypeStruct((n, c), BF16),
            jax.ShapeDtypeStruct((n, LANES), BF16),
        ] + cast_shapes,
        scratch_shapes=[pltpu.VMEM((tm, d), BF16)],
        compiler_params=_params(("arbitrary", "arbitrary"), BIG_VMEM_LIMIT_BYTES),
        name="in_proj",
    )(x2, g, mod, mod, w_main, w_lr, *[w for w, _ in casts])
    return outs[0], outs[1], outs[2:]


def _split3(x):
    hi = x.astype(BF16)
    r1 = x - hi.astype(F32)
    mid = r1.astype(BF16)
    lo = (r1 - mid.astype(F32)).astype(BF16)
    return hi, mid, lo


def _gla_kernel(q_ref, k_ref, v_ref, gr_ref, lr_ref, w2_ref, gb_ref, ng_ref, o_ref,
                qm_ref, km_ref, qe_ref, ke_ref, el_ref, st_ref, *, heads, dk, dv, gate_rows):
    seq = q_ref.shape[0]
    cs = GLA_CHUNK
    kw = heads * dk
    per_trip = gate_rows // cs
    row = lax.broadcasted_iota(jnp.int32, (cs, cs), 0)
    col = lax.broadcasted_iota(jnp.int32, (cs, cs), 1)
    causal = col <= row
    tril = causal.astype(BF16)
    scale = dk ** -0.5

    def decay(pi, carry):
        base = pl.multiple_of(pi * gate_rows, gate_rows)
        r = pl.ds(base, gate_rows)
        xg = jnp.dot(lr_ref[r, :], w2_ref[...], preferred_element_type=F32) + gb_ref[...]
        g = (jnp.minimum(xg, 0.0) - jnp.log(1.0 + jnp.exp(-jnp.abs(xg)))) * (1.0 / GLA_GATE_NORM)
        parts = jnp.concatenate(_split3(g), axis=1)
        q = q_ref[r, :].astype(F32) * scale
        k = k_ref[r, :].astype(F32)
        for c in range(per_trip):
            rows = slice(c * cs, (c + 1) * cs)
            out = pl.ds(base + c * cs, cs)
            bs = jnp.dot(tril, parts[rows, :], preferred_element_type=F32)
            b = bs[:, :kw] + bs[:, kw:2 * kw] + bs[:, 2 * kw:]
            b_last = b[cs - 1:cs, :]
            b_mid = b[cs // 2 - 1:cs // 2, :]
            qm_ref[out, :] = (q[rows, :] * jnp.exp(b - b_mid)).astype(BF16)
            km_ref[out, :] = (k[rows, :] * jnp.exp(b_mid - b)).astype(BF16)
            qe_ref[out, :] = (q[rows, :] * jnp.exp(b)).astype(BF16)
            ke_ref[out, :] = (k[rows, :] * jnp.exp(b_last - b)).astype(BF16)
            el_ref[pl.ds(pi * per_trip + c, 1), :] = jnp.exp(b_last)
        return carry

    lax.fori_loop(0, seq // gate_rows, decay, 0)
    st_ref[...] = jnp.zeros_like(st_ref)

    def chunk(ci, carry):
        r = pl.ds(pl.multiple_of(ci * cs, cs), cs)
        e_last = el_ref[pl.ds(ci, 1), :]
        hs = range(heads)
        ks = [slice(h * dk, (h + 1) * dk) for h in hs]
        vs = [slice(h * dv, (h + 1) * dv) for h in hs]
        attn = [lax.dot_general(qm_ref[r, ks[h]], km_ref[r, ks[h]], _NT, preferred_element_type=F32) for h in hs]
        v = [v_ref[r, vs[h]] for h in hs]
        st = [st_ref[h] for h in hs]
        o_inter = [lax.dot_general(qe_ref[r, ks[h]], st[h].astype(BF16), _NT, preferred_element_type=F32)
                   for h in hs]
        kv = [lax.dot_general(v[h], ke_ref[r, ks[h]], _TN, preferred_element_type=F32) for h in hs]
        attn = [jnp.where(causal, attn[h], 0.0).astype(BF16) for h in hs]
        o = [jnp.dot(attn[h], v[h], preferred_element_type=F32) + o_inter[h] for h in hs]
        for h in hs:
            st_ref[h] = st[h] * e_last[:, ks[h]] + kv[h]
        for h in hs:
            ms = jnp.mean(o[h] * o[h], axis=-1, keepdims=True)
            y = o[h] * lax.rsqrt(ms + EPS) * ng_ref[...]
            o_ref[r, vs[h]] = (y * jax.nn.silu(gr_ref[r, vs[h]].astype(F32))).astype(o_ref.dtype)
        return carry

    lax.fori_loop(0, seq // cs, chunk, 0, unroll=2)


def _gla(proj, lr, w2p, gate_b, norm_g, *, batch, seq, key_w, val_w, gate_rows=_Tiles.gla_decay_rows):
    n = proj.shape[0]
    dk = key_w // GLA_HEADS
    dv = val_w // GLA_HEADS
    assert seq % gate_rows == 0 and gate_rows % GLA_CHUNK == 0 and (2 * key_w) % val_w == 0 and dk == LANES
    vb = 2 * key_w // val_w
    rb = vb + 1
    kern = functools.partial(_gla_kernel, heads=GLA_HEADS, dk=dk, dv=dv, gate_rows=gate_rows)
    return pl.pallas_call(
        kern,
        grid=(batch,),
        in_specs=[
            pl.BlockSpec((seq, key_w), lambda b: (b, 0)),
            pl.BlockSpec((seq, key_w), lambda b: (b, 1)),
            pl.BlockSpec((seq, val_w), lambda b: (b, vb)),
            pl.BlockSpec((seq, val_w), lambda b: (b, rb)),
            pl.BlockSpec((seq, LANES), lambda b: (b, 0)),
            pl.BlockSpec((LANES, key_w), lambda b: (0, 0)),
            pl.BlockSpec((1, key_w), lambda b: (0, 0)),
            pl.BlockSpec((1, dv), lambda b: (0, 0)),
        ],
        out_specs=pl.BlockSpec((seq, val_w), lambda b: (b, 0)),
        out_shape=jax.ShapeDtypeStruct((n, val_w), BF16),
        scratch_shapes=[
            pltpu.VMEM((seq, key_w), BF16),
            pltpu.VMEM((seq, key_w), BF16),
            pltpu.VMEM((seq, key_w), BF16),
            pltpu.VMEM((seq, key_w), BF16),
            pltpu.VMEM((seq // GLA_CHUNK, key_w), F32),
            pltpu.VMEM((GLA_HEADS, dv, dk), F32),
        ],
        compiler_params=_params(("arbitrary",)),
        name="gla",
    )(proj, proj, proj, proj, lr, w2p, gate_b, norm_g)


def _moba_kernel(q_ref, k_ref, v_ref, o_ref, vt_ref, s_ref, pt_ref, *, heads):
    bs = MOBA_BLOCK
    hd = MOBA_HEAD_DIM
    seq = k_ref.shape[0]
    nb = seq // bs
    scale = hd ** -0.5
    hs = range(heads)
    cols = [slice(h * hd, (h + 1) * hd) for h in hs]

    krow = lax.broadcasted_iota(jnp.int32, (bs, bs), 0)
    qcol = lax.broadcasted_iota(jnp.int32, (bs, bs), 1)
    own_bias = jnp.where(krow <= qcol, 0.0, -jnp.inf)
    ones_row = (lax.broadcasted_iota(jnp.int32, (BF16_SUBLANES, seq), 0) == 0).astype(F32).astype(BF16)

    def select(h):
        km = jnp.mean(k_ref[:, cols[h]].astype(F32).reshape(nb, bs, hd), axis=1)
        km = jnp.concatenate([km, jnp.zeros((BF16_SUBLANES - nb, hd), F32)], axis=0).astype(BF16)
        sc = lax.dot_general(km, q_ref[:, cols[h]], _NT, preferred_element_type=F32)[0:nb, :]
        blk = lax.broadcasted_iota(jnp.int32, sc.shape, 0)
        qpos = lax.broadcasted_iota(jnp.int32, sc.shape, 1)
        past = (blk + 1) * bs <= qpos
        sc = jnp.where(past, sc, -jnp.inf)
        beaten = jnp.zeros(sc.shape, F32)
        for jp in range(nb):
            c = sc[jp:jp + 1, :]
            ahead = (c > sc) | ((c == sc) & (blk > jp))
            beaten = beaten + ahead.astype(F32)
        vt_ref[h, 0:hd, :] = v_ref[:, cols[h]].astype(F32).T.astype(BF16)
        vt_ref[h, hd:hd + BF16_SUBLANES, :] = ones_row
        return jnp.where(past & (beaten < MOBA_TOPK), 0.0, -jnp.inf)

    sel_bias = [select(h) for h in hs]

    def scores(h, i):
        nk = (i + 1) * bs
        s_ref[h, i % 2, 0:nk, :] = lax.dot_general(
            k_ref[0:nk, cols[h]], q_ref[i * bs:nk, cols[h]], _NT, preferred_element_type=F32)

    def softmax(h, i):
        slot = i % 2
        qs = slice(i * bs, (i + 1) * bs)
        nk = (i + 1) * bs
        biases = [sel_bias[h][j:j + 1, qs] for j in range(i)]
        m = (s_ref[h, slot, i * bs:nk, :] + own_bias).max(axis=0, keepdims=True)
        for j in range(i):
            m = jnp.maximum(m, s_ref[h, slot, j * bs:(j + 1) * bs, :].max(axis=0, keepdims=True) + biases[j])
        own = s_ref[h, slot, i * bs:nk, :] + own_bias
        pt_ref[h, slot, i * bs:nk, :] = jnp.exp2((own - m) * (scale * LOG2E)).astype(BF16)
        for j in range(i):
            p = jnp.exp2((s_ref[h, slot, j * bs:(j + 1) * bs, :] - (m - biases[j])) * (scale * LOG2E))
            pt_ref[h, slot, j * bs:(j + 1) * bs, :] = p.astype(BF16)

    def attend(h, i):
        nk = (i + 1) * bs
        ot = jnp.dot(vt_ref[h, :, 0:nk], pt_ref[h, i % 2, 0:nk, :], preferred_element_type=F32)
        o_ref[i * bs:nk, cols[h]] = (ot[0:hd, :] / ot[hd:hd + 1, :]).T.astype(o_ref.dtype)

    for h in hs:
        scores(h, 0)
    for i in range(nb):
        for h in hs:
            if i + 1 < nb:
                scores(h, i + 1)
        for h in hs:
            if i > 0:
                attend(h, i - 1)
        for h in hs:
            softmax(h, i)
    for h in hs:
        attend(h, nb - 1)


def _moba(proj, *, batch, seq, col0, width, heads_per_step=_Tiles.moba_heads):
    n = proj.shape[0]
    hd = MOBA_HEAD_DIM
    wb = heads_per_step * hd
    assert seq % MOBA_BLOCK == 0 and seq // MOBA_BLOCK <= BF16_SUBLANES and width % wb == 0 and col0 % wb == 0
    groups = width // wb
    qb = col0 // wb
    kb = qb + groups
    vb = kb + groups
    kern = functools.partial(_moba_kernel, heads=heads_per_step)
    return pl.pallas_call(
        kern,
        grid=(batch, groups),
        in_specs=[
            pl.BlockSpec((seq, wb), lambda b, h: (b, qb + h)),
            pl.BlockSpec((seq, wb), lambda b, h: (b, kb + h)),
            pl.BlockSpec((seq, wb), lambda b, h: (b, vb + h)),
        ],
        out_specs=pl.BlockSpec((seq, wb), lambda b, h: (b, h)),
        out_shape=jax.ShapeDtypeStruct((n, width), BF16),
        scratch_shapes=[
            pltpu.VMEM((heads_per_step, hd + BF16_SUBLANES, seq), BF16),
            pltpu.VMEM((heads_per_step, 2, seq, MOBA_BLOCK), F32),
            pltpu.VMEM((heads_per_step, 2, seq, MOBA_BLOCK), BF16),
        ],
        compiler_params=_params(("arbitrary", "arbitrary")),
        name="moba",
    )(proj, proj, proj)


def _merge_kernel(og_ref, om_ref, wg_ref, wm_ref, g1_ref, g2_ref, wo_ref, x_ref, gt_ref, o_ref):
    yg = jnp.dot(og_ref[...], wg_ref[...], preferred_element_type=F32)
    ym = jnp.dot(om_ref[...], wm_ref[...], preferred_element_type=F32)
    z = jax.nn.sigmoid(g1_ref[...].astype(F32)) * yg + jax.nn.sigmoid(g2_ref[...].astype(F32)) * ym
    y = jnp.dot(z.astype(BF16), wo_ref[...], preferred_element_type=F32)
    o_ref[...] = x_ref[...] + gt_ref[...] * y


def _merge(o_gla, o_moba, w_up_gla, w_up_moba, proj, gate_col0, w_out, x2, mod, layer, *, seq,
           tm=_Tiles.merge_rows):
    n, d = x2.shape
    assert seq % tm == 0 and gate_col0 % d == 0
    per_b = seq // tm
    g1b = gate_col0 // d
    kg = o_gla.shape[1]
    km = o_moba.shape[1]
    resident = dict(pipeline_mode=pl.Buffered(1))
    return pl.pallas_call(
        _merge_kernel,
        grid=(n // tm,),
        in_specs=[
            pl.BlockSpec((tm, kg), lambda i: (i, 0)),
            pl.BlockSpec((tm, km), lambda i: (i, 0)),
            pl.BlockSpec((None, kg, d), lambda i: (layer, 0, 0), **resident),
            pl.BlockSpec((None, km, d), lambda i: (layer, 0, 0), **resident),
            pl.BlockSpec((tm, d), lambda i: (i, g1b)),
            pl.BlockSpec((tm, d), lambda i: (i, g1b + 1)),
            pl.BlockSpec((None, d, d), lambda i: (layer, 0, 0), **resident),
            pl.BlockSpec((tm, d), lambda i: (i, 0)),
            pl.BlockSpec((None, None, 1, d), lambda i: (i // per_b, 2, 0, 0)),
        ],
        out_specs=pl.BlockSpec((tm, d), lambda i: (i, 0)),
        out_shape=jax.ShapeDtypeStruct((n, d), F32),
        compiler_params=_params(("arbitrary",)),
        name="merge",
    )(o_gla, o_moba, w_up_gla, w_up_moba, proj, proj, w_out, x2, mod)


def _ffn_kernel(x_ref, g_ref, sc_ref, sh_ref, gt_ref, wa_ref, wu_ref, wo_ref, fg_ref, o_ref, h_ref, *, final_norm):
    j = pl.program_id(1)

    def part(h):
        a = jnp.dot(h, wa_ref[...], preferred_element_type=F32)
        u = jnp.dot(h, wu_ref[...], preferred_element_type=F32)
        return jnp.dot((jax.nn.silu(a) * u).astype(BF16), wo_ref[...], preferred_element_type=F32)

    @pl.when(j == 0)
    def _():
        h = _mod_norm(x_ref[...], g_ref[...], sc_ref[...], sh_ref[...]).astype(BF16)
        h_ref[...] = h
        o_ref[...] = part(h)

    last = pl.num_programs(1) - 1

    @pl.when((j > 0) & (j < last))
    def _():
        o_ref[...] += part(h_ref[...])

    @pl.when(j == last)
    def _():
        y = x_ref[...] + gt_ref[...] * (o_ref[...] + part(h_ref[...]))
        if final_norm:
            ms = jnp.mean(y * y, axis=-1, keepdims=True)
            y = y * lax.rsqrt(ms + EPS) * fg_ref[...]
        o_ref[...] = y


def _ffn(x2, g, mod, w_in, w_out, final_g, layer, *, seq, final_norm,
         tm=_Tiles.ffn_rows, tf=_Tiles.ffn_cols):
    n, d = x2.shape
    dff = w_out.shape[1]
    per_b = seq // tm
    nf = dff // tf
    assert seq % tm == 0 and dff % tf == 0
    assert nf >= 2
    kern = functools.partial(_ffn_kernel, final_norm=final_norm)
    return pl.pallas_call(
        kern,
        grid=(n // tm, nf),
        in_specs=[
            pl.BlockSpec((tm, d), lambda i, j: (i, 0)),
            pl.BlockSpec((1, d), lambda i, j: (0, 0)),
            pl.BlockSpec((None, None, 1, d), lambda i, j: (i // per_b, 4, 0, 0)),
            pl.BlockSpec((None, None, 1, d), lambda i, j: (i // per_b, 3, 0, 0)),
            pl.BlockSpec((None, None, 1, d), lambda i, j: (i // per_b, 5, 0, 0)),
            pl.BlockSpec((None, d, tf), lambda i, j: (layer, 0, j)),
            pl.BlockSpec((None, d, tf), lambda i, j: (layer, 0, nf + j)),
            pl.BlockSpec((None, tf, d), lambda i, j: (layer, j, 0)),
            pl.BlockSpec((1, d), lambda i, j: (0, 0)),
        ],
        out_specs=pl.BlockSpec((tm, d), lambda i, j: (i, 0)),
        out_shape=jax.ShapeDtypeStruct((n, d), F32),
        scratch_shapes=[pltpu.VMEM((tm, d), BF16)],
        compiler_params=_params(("arbitrary", "arbitrary"), BIG_VMEM_LIMIT_BYTES),
        name="ffn",
    )(x2, g, mod, mod, mod, w_in, w_in, w_out, final_g)


def kernel(x, c, ada_w, ada_b, norm1_g, w_in, gla_gate_w2, gla_gate_b, gla_norm_g, w_up_gla, w_up_moba,
           w_out, norm2_g, w_ffn_in, w_ffn_out, final_g):
    batch, seq, d = x.shape
    depth = ada_w.shape[0]
    rank, key_w = gla_gate_w2.shape[1:]
    val_w = w_up_gla.shape[1]
    moba_w = w_up_moba.shape[1]
    n = batch * seq
    lr0 = 2 * key_w + 2 * val_w
    moba0 = lr0
    gate0 = moba0 + 3 * moba_w

    mod = _adaln(c, ada_w, ada_b).reshape(depth, batch, 6, 1, d)
    w_main, w_lr = _repack_w_in(jnp.swapaxes(w_in, 1, 2), lr0, rank)
    x2 = x.reshape(n, d)
    for l in range(depth):
        w2p = jnp.pad(gla_gate_w2[l], ((0, LANES - rank), (0, 0))).astype(BF16)
        proj, lr, (w_up_gla_b, w_up_moba_b, w_out_b, w_ffn_in_b, w_ffn_out_b) = _in_proj(
            x2, norm1_g[l][None], mod[l], w_main, w_lr, l, [w_up_gla, w_up_moba, w_out, w_ffn_in, w_ffn_out],
            seq=seq)
        o_gla = _gla(proj, lr, w2p, gla_gate_b[l][None], gla_norm_g[l][None],
                     batch=batch, seq=seq, key_w=key_w, val_w=val_w)
        o_moba = _moba(proj, batch=batch, seq=seq, col0=moba0, width=moba_w)
        x2 = _merge(o_gla, o_moba, w_up_gla_b, w_up_moba_b, proj, gate0, w_out_b, x2, mod[l], 0, seq=seq)
        x2 = _ffn(x2, norm2_g[l][None], mod[l], w_ffn_in_b, w_ffn_out_b, final_g[None], 0,
                  seq=seq, final_norm=(l == depth - 1))
    return x2.reshape(batch, seq, d)
```

```python
import functools

import jax
import jax.numpy as jnp
from jax import lax
from jax.experimental import pallas as pl
from jax.experimental.pallas import tpu as pltpu

F32 = jnp.float32
BF16 = jnp.bfloat16

GLA_HEADS = 4
GLA_GATE_NORM = 16.0
GLA_CHUNK = 64
MOBA_HEAD_DIM = 128
MOBA_BLOCK = 256
MOBA_TOPK = 3
EPS = 1e-6
LOG2E = 1.4426950408889634

V7X_VMEM_BYTES = 64 * 1024 * 1024
VMEM_LIMIT_BYTES = V7X_VMEM_BYTES - 8 * 1024 * 1024
BIG_VMEM_LIMIT_BYTES = V7X_VMEM_BYTES - 4 * 1024 * 1024
LANES = 128
BF16_SUBLANES = 16


class _Tiles:
    adaln_cols = 1024
    repack_rows = 1024
    in_proj_rows, in_proj_cols = 1024, 2048
    in_proj_norm_chunks = 4
    gla_decay_rows = 512
    moba_heads = 2
    merge_rows = 512
    ffn_rows, ffn_cols = 1024, 512

_NT = (((1,), (1,)), ((), ()))
_TN = (((0,), (0,)), ((), ()))


def _params(semantics, vmem_limit_bytes=VMEM_LIMIT_BYTES):
    return pltpu.CompilerParams(dimension_semantics=semantics, vmem_limit_bytes=vmem_limit_bytes)


def _mod_norm(x, g, sc, sh):
    ms = jnp.mean(x * x, axis=-1, keepdims=True)
    y = x * lax.rsqrt(ms + EPS) * g
    return y * (1.0 + sc) + sh


def _adaln_kernel(c_ref, w_ref, b_ref, o_ref):
    c_act = jax.nn.silu(c_ref[...])
    o_ref[...] = (
        jnp.dot(c_act.astype(BF16), w_ref[...].astype(BF16), preferred_element_type=F32) + b_ref[...]
    )


def _adaln(c, ada_w, ada_b, *, tn=_Tiles.adaln_cols):
    depth, d, n6 = ada_w.shape
    b = c.shape[0]
    return pl.pallas_call(
        _adaln_kernel,
        grid=(depth, n6 // tn),
        in_specs=[
            pl.BlockSpec((b, d), lambda l, j: (0, 0)),
            pl.BlockSpec((None, d, tn), lambda l, j: (l, 0, j)),
            pl.BlockSpec((None, 1, tn), lambda l, j: (l, 0, j)),
        ],
        out_specs=pl.BlockSpec((None, b, tn), lambda l, j: (l, 0, j)),
        out_shape=jax.ShapeDtypeStruct((depth, b, n6), F32),
        compiler_params=_params(("arbitrary", "arbitrary")),
        name="adaln",
    )(c, ada_w, ada_b.reshape(depth, 1, n6))


def _repack_kernel(a_ref, b_ref, o_ref, ol_ref, *, lr_tile, rank):
    t = pl.program_id(1)

    @pl.when(t < lr_tile)
    def _():
        o_ref[...] = a_ref[...].astype(BF16)

    @pl.when(t >= lr_tile)
    def _():
        o_ref[...] = jnp.concatenate([a_ref[rank:, :], b_ref[...]], axis=0).astype(BF16)

    @pl.when(t == lr_tile)
    def _():
        pad = jnp.zeros((ol_ref.shape[0] - rank, ol_ref.shape[1]), F32)
        ol_ref[...] = jnp.concatenate([a_ref[0:rank, :], pad], axis=0).astype(BF16)


def _repack_w_in(w_in_t, lr0, rank, *, tw=_Tiles.repack_rows):
    depth, c, d = w_in_t.shape
    assert lr0 % tw == 0 and (c - rank) % tw == 0 and tw % rank == 0 and rank % BF16_SUBLANES == 0
    kern = functools.partial(_repack_kernel, lr_tile=lr0 // tw, rank=rank)
    return pl.pallas_call(
        kern,
        grid=(depth, (c - rank) // tw),
        in_specs=[
            pl.BlockSpec((None, tw, d), lambda l, t: (l, t, 0)),
            pl.BlockSpec((None, rank, d), lambda l, t: (l, (t + 1) * (tw // rank), 0)),
        ],
        out_specs=[
            pl.BlockSpec((None, tw, d), lambda l, t: (l, t, 0)),
            pl.BlockSpec((None, LANES, d), lambda l, t: (l, 0, 0)),
        ],
        out_shape=[
            jax.ShapeDtypeStruct((depth, c - rank, d), BF16),
            jax.ShapeDtypeStruct((depth, LANES, d), BF16),
        ],
        compiler_params=_params(("arbitrary", "arbitrary")),
        name="repack_w_in",
    )(w_in_t, w_in_t)


def _cast_rows(rows, steps):
    for r in range(BF16_SUBLANES, rows + 1, BF16_SUBLANES):
        if rows % r == 0 and rows // r <= steps:
            return r
    raise ValueError(f"no bf16-aligned row block covers {rows} rows in {steps} steps")


def _in_proj_kernel(x_ref, g_ref, sc_ref, sh_ref, w_ref, wl_ref, *rest, n_cast, norm_chunks):
    cast_in = rest[:n_cast]
    o_ref, ol_ref = rest[n_cast:n_cast + 2]
    cast_out = rest[n_cast + 2:2 * n_cast + 2]
    h_ref = rest[-1]
    j = pl.program_id(1)

    def side_jobs():
        for src, dst in zip(cast_in, cast_out):
            dst[...] = src[...].astype(BF16)

    @pl.when(j == 0)
    def _():
        rows = x_ref.shape[0] // norm_chunks
        for c in range(norm_chunks):
            r = slice(c * rows, (c + 1) * rows)
            h = _mod_norm(x_ref[r, :], g_ref[...], sc_ref[...], sh_ref[...]).astype(BF16)
            h_ref[r, :] = h
            ol_ref[r, :] = lax.dot_general(h, wl_ref[...], _NT, preferred_element_type=F32).astype(ol_ref.dtype)
            o_ref[r, :] = lax.dot_general(h, w_ref[...], _NT, preferred_element_type=F32).astype(o_ref.dtype)
        side_jobs()

    @pl.when(j > 0)
    def _():
        o_ref[...] = lax.dot_general(h_ref[...], w_ref[...], _NT, preferred_element_type=F32).astype(o_ref.dtype)
        side_jobs()


def _in_proj(x2, g, mod, w_main, w_lr, layer, cast_weights, *, seq,
             tm=_Tiles.in_proj_rows, tn=_Tiles.in_proj_cols):
    n, d = x2.shape
    c = w_main.shape[1]
    per_b = seq // tm
    ncol = c // tn
    steps = (n // tm) * ncol
    cast_in_specs, cast_out_specs, cast_shapes = [], [], []
    casts = [(w, _cast_rows(w.shape[1], steps)) for w in cast_weights]
    for w, rows in casts:
        nblk = w.shape[1] // rows

        def block(i, j, nblk=nblk):
            return jnp.minimum(i * ncol + j, nblk - 1)

        cast_in_specs.append(pl.BlockSpec((None, rows, w.shape[2]), lambda i, j, b=block: (layer, b(i, j), 0)))
        cast_out_specs.append(pl.BlockSpec((None, rows, w.shape[2]), lambda i, j, b=block: (0, b(i, j), 0)))
        cast_shapes.append(jax.ShapeDtypeStruct((1,) + w.shape[1:], BF16))
    kern = functools.partial(_in_proj_kernel, n_cast=len(casts), norm_chunks=_Tiles.in_proj_norm_chunks)
    outs = pl.pallas_call(
        kern,
        grid=(n // tm, ncol),
        in_specs=[
            pl.BlockSpec((tm, d), lambda i, j: (i, 0)),
            pl.BlockSpec((1, d), lambda i, j: (0, 0)),
            pl.BlockSpec((None, None, 1, d), lambda i, j: (i // per_b, 1, 0, 0)),
            pl.BlockSpec((None, None, 1, d), lambda i, j: (i // per_b, 0, 0, 0)),
            pl.BlockSpec((None, tn, d), lambda i, j: (layer, j, 0)),
            pl.BlockSpec((None, LANES, d), lambda i, j: (layer, 0, 0)),
        ] + cast_in_specs,
        out_specs=[
            pl.BlockSpec((tm, tn), lambda i, j: (i, j)),
            pl.BlockSpec((tm, LANES), lambda i, j: (i, 0)),
        ] + cast_out_specs,
        out_shape=[
            jax.ShapeDtypeStruct((n, c), BF16),
            jax.ShapeDtypeStruct((n, LANES), BF16),
        ] + cast_shapes,
        scratch_shapes=[pltpu.VMEM((tm, d), BF16)],
        compiler_params=_params(("arbitrary", "arbitrary"), BIG_VMEM_LIMIT_BYTES),
        name="in_proj",
    )(x2, g, mod, mod, w_main, w_lr, *[w for w, _ in casts])
    return outs[0], outs[1], outs[2:]


def _split3(x):
    hi = x.astype(BF16)
    r1 = x - hi.astype(F32)
    mid = r1.astype(BF16)
    lo = (r1 - mid.astype(F32)).astype(BF16)
    return hi, mid, lo


def _gla_kernel(q_ref, k_ref, v_ref, gr_ref, lr_ref, w2_ref, gb_ref, ng_ref, o_ref,
                qm_ref, km_ref, qe_ref, ke_ref, el_ref, st_ref, *, heads, dk, dv, gate_rows):
    seq = q_ref.shape[0]
    cs = GLA_CHUNK
    kw = heads * dk
    per_trip = gate_rows // cs
    row = lax.broadcasted_iota(jnp.int32, (cs, cs), 0)
    col = lax.broadcasted_iota(jnp.int32, (cs, cs), 1)
    causal = col <= row
    tril = causal.astype(BF16)
    scale = dk ** -0.5

    def decay(pi, carry):
        base = pl.multiple_of(pi * gate_rows, gate_rows)
        r = pl.ds(base, gate_rows)
        xg = jnp.dot(lr_ref[r, :], w2_ref[...], preferred_element_type=F32) + gb_ref[...]
        g = (jnp.minimum(xg, 0.0) - jnp.log(1.0 + jnp.exp(-jnp.abs(xg)))) * (1.0 / GLA_GATE_NORM)
        parts = jnp.concatenate(_split3(g), axis=1)
        q = q_ref[r, :].astype(F32) * scale
        k = k_ref[r, :].astype(F32)
        for c in range(per_trip):
            rows = slice(c * cs, (c + 1) * cs)
            out = pl.ds(base + c * cs, cs)
            bs = jnp.dot(tril, parts[rows, :], preferred_element_type=F32)
            b = bs[:, :kw] + bs[:, kw:2 * kw] + bs[:, 2 * kw:]
            b_last = b[cs - 1:cs, :]
            b_mid = b[cs // 2 - 1:cs // 2, :]
            qm_ref[out, :] = (q[rows, :] * jnp.exp(b - b_mid)).astype(BF16)
            km_ref[out, :] = (k[rows, :] * jnp.exp(b_mid - b)).astype(BF16)
            qe_ref[out, :] = (q[rows, :] * jnp.exp(b)).astype(BF16)
            ke_ref[out, :] = (k[rows, :] * jnp.exp(b_last - b)).astype(BF16)
            el_ref[pl.ds(pi * per_trip + c, 1), :] = jnp.exp(b_last)
        return carry

    lax.fori_loop(0, seq // gate_rows, decay, 0)
    st_ref[...] = jnp.zeros_like(st_ref)

    def chunk(ci, carry):
        r = pl.ds(pl.multiple_of(ci * cs, cs), cs)
        e_last = el_ref[pl.ds(ci, 1), :]
        hs = range(heads)
        ks = [slice(h * dk, (h + 1) * dk) for h in hs]
        vs = [slice(h * dv, (h + 1) * dv) for h in hs]
        attn = [lax.dot_general(qm_ref[r, ks[h]], km_ref[r, ks[h]], _NT, preferred_element_type=F32) for h in hs]
        v = [v_ref[r, vs[h]] for h in hs]
        st = [st_ref[h] for h in hs]
        o_inter = [lax.dot_general(qe_ref[r, ks[h]], st[h].astype(BF16), _NT, preferred_element_type=F32)
                   for h in hs]
        kv = [lax.dot_general(v[h], ke_ref[r, ks[h]], _TN, preferred_element_type=F32) for h in hs]
        attn = [jnp.where(causal, attn[h], 0.0).astype(BF16) for h in hs]
        o = [jnp.dot(attn[h], v[h], preferred_element_type=F32) + o_inter[h] for h in hs]
        for h in hs:
            st_ref[h] = st[h] * e_last[:, ks[h]] + kv[h]
        for h in hs:
            ms = jnp.mean(o[h] * o[h], axis=-1, keepdims=True)
            y = o[h] * lax.rsqrt(ms + EPS) * ng_ref[...]
            o_ref[r, vs[h]] = (y * jax.nn.silu(gr_ref[r, vs[h]].astype(F32))).astype(o_ref.dtype)
        return carry

    lax.fori_loop(0, seq // cs, chunk, 0, unroll=2)


def _gla(proj, lr, w2p, gate_b, norm_g, *, batch, seq, key_w, val_w, gate_rows=_Tiles.gla_decay_rows):
    n = proj.shape[0]
    dk = key_w // GLA_HEADS
    dv = val_w // GLA_HEADS
    vb = 2 * key_w // val_w
    rb = vb + 1
    kern = functools.partial(_gla_kernel, heads=GLA_HEADS, dk=dk, dv=dv, gate_rows=gate_rows)
    return pl.pallas_call(
        kern,
        grid=(batch,),
        in_specs=[
            pl.BlockSpec((seq, key_w), lambda b: (b, 0)),
            pl.BlockSpec((seq, key_w), lambda b: (b, 1)),
            pl.BlockSpec((seq, val_w), lambda b: (b, vb)),
            pl.BlockSpec((seq, val_w), lambda b: (b, rb)),
            pl.BlockSpec((seq, LANES), lambda b: (b, 0)),
            pl.BlockSpec((LANES, key_w), lambda b: (0, 0)),
            pl.BlockSpec((1, key_w), lambda b: (0, 0)),
            pl.BlockSpec((1, dv), lambda b: (0, 0)),
        ],
        out_specs=pl.BlockSpec((seq, val_w), lambda b: (b, 0)),
        out_shape=jax.ShapeDtypeStruct((n, val_w), BF16),
        scratch_shapes=[
            pltpu.VMEM((seq, key_w), BF16),
            pltpu.VMEM((seq, key_w), BF16),
            pltpu.VMEM((seq, key_w), BF16),
            pltpu.VMEM((seq, key_w), BF16),
            pltpu.VMEM((seq // GLA_CHUNK, key_w), F32),
            pltpu.VMEM((GLA_HEADS, dv, dk), F32),
        ],
        compiler_params=_params(("arbitrary",)),
        name="gla",
    )(proj, proj, proj, proj, lr, w2p, gate_b, norm_g)


def _moba_kernel(q_ref, k_ref, v_ref, o_ref, vt_ref, s_ref, pt_ref, *, heads):
    bs = MOBA_BLOCK
    hd = MOBA_HEAD_DIM
    seq = k_ref.shape[0]
    nb = seq // bs
    scale = hd ** -0.5
    hs = range(heads)
    cols = [slice(h * hd, (h + 1) * hd) for h in hs]

    krow = lax.broadcasted_iota(jnp.int32, (bs, bs), 0)
    qcol = lax.broadcasted_iota(jnp.int32, (bs, bs), 1)
    own_bias = jnp.where(krow <= qcol, 0.0, -jnp.inf)
    ones_row = (lax.broadcasted_iota(jnp.int32, (BF16_SUBLANES, seq), 0) == 0).astype(F32).astype(BF16)

    def select(h):
        km = jnp.mean(k_ref[:, cols[h]].astype(F32).reshape(nb, bs, hd), axis=1)
        km = jnp.concatenate([km, jnp.zeros((BF16_SUBLANES - nb, hd), F32)], axis=0).astype(BF16)
        sc = lax.dot_general(km, q_ref[:, cols[h]], _NT, preferred_element_type=F32)[0:nb, :]
        blk = lax.broadcasted_iota(jnp.int32, sc.shape, 0)
        qpos = lax.broadcasted_iota(jnp.int32, sc.shape, 1)
        past = (blk + 1) * bs <= qpos
        sc = jnp.where(past, sc, -jnp.inf)
        beaten = jnp.zeros(sc.shape, F32)
        for jp in range(nb):
            c = sc[jp:jp + 1, :]
            ahead = (c > sc) | ((c == sc) & (blk > jp))
            beaten = beaten + ahead.astype(F32)
        vt_ref[h, 0:hd, :] = v_ref[:, cols[h]].astype(F32).T.astype(BF16)
        vt_ref[h, hd:hd + BF16_SUBLANES, :] = ones_row
        return jnp.where(past & (beaten < MOBA_TOPK), 0.0, -jnp.inf)

    sel_bias = [select(h) for h in hs]

    def scores(h, i):
        nk = (i + 1) * bs
        s_ref[h, i % 2, 0:nk, :] = lax.dot_general(
            k_ref[0:nk, cols[h]], q_ref[i * bs:nk, cols[h]], _NT, preferred_element_type=F32)

    def softmax(h, i):
        slot = i % 2
        qs = slice(i * bs, (i + 1) * bs)
        nk = (i + 1) * bs
        biases = [sel_bias[h][j:j + 1, qs] for j in range(i)]
        m = (s_ref[h, slot, i * bs:nk, :] + own_bias).max(axis=0, keepdims=True)
        for j in range(i):
            m = jnp.maximum(m, s_ref[h, slot, j * bs:(j + 1) * bs, :].max(axis=0, keepdims=True) + biases[j])
        own = s_ref[h, slot, i * bs:nk, :] + own_bias
        pt_ref[h, slot, i * bs:nk, :] = jnp.exp2((own - m) * (scale * LOG2E)).astype(BF16)
        for j in range(i):
            p = jnp.exp2((s_ref[h, slot, j * bs:(j + 1) * bs, :] - (m - biases[j])) * (scale * LOG2E))
            pt_ref[h, slot, j * bs:(j + 1) * bs, :] = p.astype(BF16)

    def attend(h, i):
        nk = (i + 1) * bs
        ot = jnp.dot(vt_ref[h, :, 0:nk], pt_ref[h, i % 2, 0:nk, :], preferred_element_type=F32)
        o_ref[i * bs:nk, cols[h]] = (ot[0:hd, :] / ot[hd:hd + 1, :]).T.astype(o_ref.dtype)

    for h in hs:
        scores(h, 0)
    for i in range(nb):
        for h in hs:
            if i + 1 < nb:
                scores(h, i + 1)
        for h in hs:
            if i > 0:
                attend(h, i - 1)
        for h in hs:
            softmax(h, i)
    for h in hs:
        attend(h, nb - 1)


def _moba(proj, *, batch, seq, col0, width, heads_per_step=_Tiles.moba_heads):
    n = proj.shape[0]
    hd = MOBA_HEAD_DIM
    wb = heads_per_step * hd
    groups = width // wb
    qb = col0 // wb
    kb = qb + groups
    vb = kb + groups
    kern = functools.partial(_moba_kernel, heads=heads_per_step)
    return pl.pallas_call(
        kern,
        grid=(batch, groups),
        in_specs=[
            pl.BlockSpec((seq, wb), lambda b, h: (b, qb + h)),
            pl.BlockSpec((seq, wb), lambda b, h: (b, kb + h)),
            pl.BlockSpec((seq, wb), lambda b, h: (b, vb + h)),
        ],
        out_specs=pl.BlockSpec((seq, wb), lambda b, h: (b, h)),
        out_shape=jax.ShapeDtypeStruct((n, width), BF16),
        scratch_shapes=[
            pltpu.VMEM((heads_per_step, hd + BF16_SUBLANES, seq), BF16),
            pltpu.VMEM((heads_per_step, 2, seq, MOBA_BLOCK), F32),
            pltpu.VMEM((heads_per_step, 2, seq, MOBA_BLOCK), BF16),
        ],
        compiler_params=_params(("arbitrary", "arbitrary")),
        name="moba",
    )(proj, proj, proj)


def _merge_kernel(og_ref, om_ref, wg_ref, wm_ref, g1_ref, g2_ref, wo_ref, x_ref, gt_ref, o_ref):
    yg = jnp.dot(og_ref[...], wg_ref[...], preferred_element_type=F32)
    ym = jnp.dot(om_ref[...], wm_ref[...], preferred_element_type=F32)
    z = jax.nn.sigmoid(g1_ref[...].astype(F32)) * yg + jax.nn.sigmoid(g2_ref[...].astype(F32)) * ym
    y = jnp.dot(z.astype(BF16), wo_ref[...], preferred_element_type=F32)
    o_ref[...] = x_ref[...] + gt_ref[...] * y


def _merge(o_gla, o_moba, w_up_gla, w_up_moba, proj, gate_col0, w_out, x2, mod, layer, *, seq,
           tm=_Tiles.merge_rows):
    n, d = x2.shape
    per_b = seq // tm
    g1b = gate_col0 // d
    kg = o_gla.shape[1]
    km = o_moba.shape[1]
    resident = dict(pipeline_mode=pl.Buffered(1))
    return pl.pallas_call(
        _merge_kernel,
        grid=(n // tm,),
        in_specs=[
            pl.BlockSpec((tm, kg), lambda i: (i, 0)),
            pl.BlockSpec((tm, km), lambda i: (i, 0)),
            pl.BlockSpec((None, kg, d), lambda i: (layer, 0, 0), **resident),
            pl.BlockSpec((None, km, d), lambda i: (layer, 0, 0), **resident),
            pl.BlockSpec((tm, d), lambda i: (i, g1b)),
            pl.BlockSpec((tm, d), lambda i: (i, g1b + 1)),
            pl.BlockSpec((None, d, d), lambda i: (layer, 0, 0), **resident),
            pl.BlockSpec((tm, d), lambda i: (i, 0)),
            pl.BlockSpec((None, None, 1, d), lambda i: (i // per_b, 2, 0, 0)),
        ],
        out_specs=pl.BlockSpec((tm, d), lambda i: (i, 0)),
        out_shape=jax.ShapeDtypeStruct((n, d), F32),
        compiler_params=_params(("arbitrary",)),
        name="merge",
    )(o_gla, o_moba, w_up_gla, w_up_moba, proj, proj, w_out, x2, mod)


def _ffn_kernel(x_ref, g_ref, sc_ref, sh_ref, gt_ref, win_hbm, wout_hbm, fg_ref, o_ref,
                h_ref, wa_buf, wu_buf, wo_buf, sem, *, final_norm, layer, nf, tf):
    def copies(j, slot):
        return (
            pltpu.make_async_copy(win_hbm.at[layer, :, pl.ds(j * tf, tf)], wa_buf.at[slot], sem.at[slot, 0]),
            pltpu.make_async_copy(win_hbm.at[layer, :, pl.ds((nf + j) * tf, tf)], wu_buf.at[slot], sem.at[slot, 1]),
            pltpu.make_async_copy(wout_hbm.at[layer, pl.ds(j * tf, tf), :], wo_buf.at[slot], sem.at[slot, 2]),
        )

    for cp in copies(0, 0):
        cp.start()
    h_ref[...] = _mod_norm(x_ref[...], g_ref[...], sc_ref[...], sh_ref[...]).astype(BF16)
    for j in range(nf):
        slot = j % 2
        if j + 1 < nf:
            for cp in copies(j + 1, 1 - slot):
                cp.start()
        for cp in copies(j, slot):
            cp.wait()
        h = h_ref[...]
        a = jnp.dot(h, wa_buf[slot], preferred_element_type=F32)
        u = jnp.dot(h, wu_buf[slot], preferred_element_type=F32)
        part = jnp.dot((jax.nn.silu(a) * u).astype(BF16), wo_buf[slot], preferred_element_type=F32)
        if j == 0:
            o_ref[...] = part
        elif j < nf - 1:
            o_ref[...] += part
        else:
            y = x_ref[...] + gt_ref[...] * (o_ref[...] + part)
            if final_norm:
                ms = jnp.mean(y * y, axis=-1, keepdims=True)
                y = y * lax.rsqrt(ms + EPS) * fg_ref[...]
            o_ref[...] = y


def _ffn(x2, g, mod, w_in, w_out, final_g, layer, *, seq, final_norm,
         tm=_Tiles.ffn_rows, tf=_Tiles.ffn_cols):
    n, d = x2.shape
    dff = w_out.shape[1]
    per_b = seq // tm
    nf = dff // tf
    kern = functools.partial(_ffn_kernel, final_norm=final_norm, layer=layer, nf=nf, tf=tf)
    return pl.pallas_call(
        kern,
        grid=(n // tm,),
        in_specs=[
            pl.BlockSpec((tm, d), lambda i: (i, 0)),
            pl.BlockSpec((1, d), lambda i: (0, 0)),
            pl.BlockSpec((None, None, 1, d), lambda i: (i // per_b, 4, 0, 0)),
            pl.BlockSpec((None, None, 1, d), lambda i: (i // per_b, 3, 0, 0)),
            pl.BlockSpec((None, None, 1, d), lambda i: (i // per_b, 5, 0, 0)),
            pl.BlockSpec(memory_space=pl.ANY),
            pl.BlockSpec(memory_space=pl.ANY),
            pl.BlockSpec((1, d), lambda i: (0, 0)),
        ],
        out_specs=pl.BlockSpec((tm, d), lambda i: (i, 0)),
        out_shape=jax.ShapeDtypeStruct((n, d), F32),
        scratch_shapes=[
            pltpu.VMEM((tm, d), BF16),
            pltpu.VMEM((2, d, tf), BF16),
            pltpu.VMEM((2, d, tf), BF16),
            pltpu.VMEM((2, tf, d), BF16),
            pltpu.SemaphoreType.DMA((2, 3)),
        ],
        compiler_params=_params(("arbitrary",), BIG_VMEM_LIMIT_BYTES),
        name="ffn",
    )(x2, g, mod, mod, mod, w_in, w_out, final_g)


def kernel(x, c, ada_w, ada_b, norm1_g, w_in, gla_gate_w2, gla_gate_b, gla_norm_g, w_up_gla, w_up_moba,
           w_out, norm2_g, w_ffn_in, w_ffn_out, final_g):
    batch, seq, d = x.shape
    depth = ada_w.shape[0]
    rank, key_w = gla_gate_w2.shape[1:]
    val_w = w_up_gla.shape[1]
    moba_w = w_up_moba.shape[1]
    n = batch * seq
    lr0 = 2 * key_w + 2 * val_w
    moba0 = lr0
    gate0 = moba0 + 3 * moba_w

    mod = _adaln(c, ada_w, ada_b).reshape(depth, batch, 6, 1, d)
    w_main, w_lr = _repack_w_in(jnp.swapaxes(w_in, 1, 2), lr0, rank)
    x2 = x.reshape(n, d)
    for l in range(depth):
        w2p = jnp.pad(gla_gate_w2[l], ((0, LANES - rank), (0, 0))).astype(BF16)
        proj, lr, (w_up_gla_b, w_up_moba_b, w_out_b, w_ffn_in_b, w_ffn_out_b) = _in_proj(
            x2, norm1_g[l][None], mod[l], w_main, w_lr, l, [w_up_gla, w_up_moba, w_out, w_ffn_in, w_ffn_out],
            seq=seq)
        o_gla = _gla(proj, lr, w2p, gla_gate_b[l][None], gla_norm_g[l][None],
                     batch=batch, seq=seq, key_w=key_w, val_w=val_w)
        o_moba = _moba(proj, batch=batch, seq=seq, col0=moba0, width=moba_w)
        x2 = _merge(o_gla, o_moba, w_up_gla_b, w_up_moba_b, proj, gate0, w_out_b, x2, mod[l], 0, seq=seq)
        x2 = _ffn(x2, norm2_g[l][None], mod[l], w_ffn_in_b, w_ffn_out_b, final_g[None], 0,
                  seq=seq, final_norm=(l == depth - 1))
    return x2.reshape(batch, seq, d)
```

```python
import functools

import jax
import jax.numpy as jnp
from jax import lax
from jax.experimental import pallas as pl
from jax.experimental.pallas import tpu as pltpu

F32 = jnp.float32
BF16 = jnp.bfloat16

GLA_HEADS = 4
GLA_GATE_NORM = 16.0
GLA_CHUNK = 128
MOBA_HEAD_DIM = 128
MOBA_BLOCK = 256
MOBA_TOPK = 3
EPS = 1e-6
LOG2E = 1.4426950408889634

V7X_VMEM_BYTES = 64 * 1024 * 1024
VMEM_LIMIT_BYTES = V7X_VMEM_BYTES - 8 * 1024 * 1024
BIG_VMEM_LIMIT_BYTES = V7X_VMEM_BYTES - 4 * 1024 * 1024
LANES = 128
BF16_SUBLANES = 16


class _Tiles:
    adaln_cols = 1024
    repack_rows = 1024
    in_proj_rows, in_proj_cols = 1024, 2048
    in_proj_norm_chunks = 4
    gla_decay_rows = 512
    moba_heads = 2
    merge_rows = 512
    ffn_rows, ffn_cols = 1024, 512

_NT = (((1,), (1,)), ((), ()))
_TN = (((0,), (0,)), ((), ()))


def _params(semantics, vmem_limit_bytes=VMEM_LIMIT_BYTES):
    return pltpu.CompilerParams(dimension_semantics=semantics, vmem_limit_bytes=vmem_limit_bytes)


def _mod_norm(x, g, sc, sh):
    ms = jnp.mean(x * x, axis=-1, keepdims=True)
    y = x * lax.rsqrt(ms + EPS) * g
    return y * (1.0 + sc) + sh


def _adaln_kernel(c_ref, w_ref, b_ref, o_ref):
    c_act = jax.nn.silu(c_ref[...])
    o_ref[...] = (
        jnp.dot(c_act.astype(BF16), w_ref[...].astype(BF16), preferred_element_type=F32) + b_ref[...]
    )


def _adaln(c, ada_w, ada_b, *, tn=_Tiles.adaln_cols):
    depth, d, n6 = ada_w.shape
    b = c.shape[0]
    return pl.pallas_call(
        _adaln_kernel,
        grid=(depth, n6 // tn),
        in_specs=[
            pl.BlockSpec((b, d), lambda l, j: (0, 0)),
            pl.BlockSpec((None, d, tn), lambda l, j: (l, 0, j)),
            pl.BlockSpec((None, 1, tn), lambda l, j: (l, 0, j)),
        ],
        out_specs=pl.BlockSpec((None, b, tn), lambda l, j: (l, 0, j)),
        out_shape=jax.ShapeDtypeStruct((depth, b, n6), F32),
        compiler_params=_params(("arbitrary", "arbitrary")),
        name="adaln",
    )(c, ada_w, ada_b.reshape(depth, 1, n6))


def _repack_kernel(a_ref, b_ref, o_ref, ol_ref, *, lr_tile, rank):
    t = pl.program_id(1)

    @pl.when(t < lr_tile)
    def _():
        o_ref[...] = a_ref[...].astype(BF16)

    @pl.when(t >= lr_tile)
    def _():
        o_ref[...] = jnp.concatenate([a_ref[rank:, :], b_ref[...]], axis=0).astype(BF16)

    @pl.when(t == lr_tile)
    def _():
        pad = jnp.zeros((ol_ref.shape[0] - rank, ol_ref.shape[1]), F32)
        ol_ref[...] = jnp.concatenate([a_ref[0:rank, :], pad], axis=0).astype(BF16)


def _repack_w_in(w_in_t, lr0, rank, *, tw=_Tiles.repack_rows):
    depth, c, d = w_in_t.shape
    assert lr0 % tw == 0 and (c - rank) % tw == 0 and tw % rank == 0 and rank % BF16_SUBLANES == 0
    kern = functools.partial(_repack_kernel, lr_tile=lr0 // tw, rank=rank)
    return pl.pallas_call(
        kern,
        grid=(depth, (c - rank) // tw),
        in_specs=[
            pl.BlockSpec((None, tw, d), lambda l, t: (l, t, 0)),
            pl.BlockSpec((None, rank, d), lambda l, t: (l, (t + 1) * (tw // rank), 0)),
        ],
        out_specs=[
            pl.BlockSpec((None, tw, d), lambda l, t: (l, t, 0)),
            pl.BlockSpec((None, LANES, d), lambda l, t: (l, 0, 0)),
        ],
        out_shape=[
            jax.ShapeDtypeStruct((depth, c - rank, d), BF16),
            jax.ShapeDtypeStruct((depth, LANES, d), BF16),
        ],
        compiler_params=_params(("arbitrary", "arbitrary")),
        name="repack_w_in",
    )(w_in_t, w_in_t)


def _cast_rows(rows, steps):
    for r in range(BF16_SUBLANES, rows + 1, BF16_SUBLANES):
        if rows % r == 0 and rows // r <= steps:
            return r
    raise ValueError(f"no bf16-aligned row block covers {rows} rows in {steps} steps")


def _in_proj_kernel(x_ref, g_ref, sc_ref, sh_ref, w_ref, wl_ref, *rest, n_cast, norm_chunks):
    cast_in = rest[:n_cast]
    o_ref, ol_ref = rest[n_cast:n_cast + 2]
    cast_out = rest[n_cast + 2:2 * n_cast + 2]
    h_ref = rest[-1]
    j = pl.program_id(1)

    def side_jobs():
        for src, dst in zip(cast_in, cast_out):
            dst[...] = src[...].astype(BF16)

    @pl.when(j == 0)
    def _():
        rows = x_ref.shape[0] // norm_chunks
        for c in range(norm_chunks):
            r = slice(c * rows, (c + 1) * rows)
            h = _mod_norm(x_ref[r, :], g_ref[...], sc_ref[...], sh_ref[...]).astype(BF16)
            h_ref[r, :] = h
            ol_ref[r, :] = lax.dot_general(h, wl_ref[...], _NT, preferred_element_type=F32).astype(ol_ref.dtype)
            o_ref[r, :] = lax.dot_general(h, w_ref[...], _NT, preferred_element_type=F32).astype(o_ref.dtype)
        side_jobs()

    @pl.when(j > 0)
    def _():
        o_ref[...] = lax.dot_general(h_ref[...], w_ref[...], _NT, preferred_element_type=F32).astype(o_ref.dtype)
        side_jobs()


def _in_proj(x2, g, mod, w_main, w_lr, layer, cast_weights, *, seq,
             tm=_Tiles.in_proj_rows, tn=_Tiles.in_proj_cols):
    n, d = x2.shape
    c = w_main.shape[1]
    per_b = seq // tm
    ncol = c // tn
    steps = (n // tm) * ncol
    cast_in_specs, cast_out_specs, cast_shapes = [], [], []
    casts = [(w, _cast_rows(w.shape[1], steps)) for w in cast_weights]
    for w, rows in casts:
        nblk = w.shape[1] // rows

        def block(i, j, nblk=nblk):
            return jnp.minimum(i * ncol + j, nblk - 1)

        cast_in_specs.append(pl.BlockSpec((None, rows, w.shape[2]), lambda i, j, b=block: (layer, b(i, j), 0)))
        cast_out_specs.append(pl.BlockSpec((None, rows, w.shape[2]), lambda i, j, b=block: (0, b(i, j), 0)))
        cast_shapes.append(jax.ShapeDtypeStruct((1,) + w.shape[1:], BF16))
    kern = functools.partial(_in_proj_kernel, n_cast=len(casts), norm_chunks=_Tiles.in_proj_norm_chunks)
    outs = pl.pallas_call(
        kern,
        grid=(n // tm, ncol),
        in_specs=[
            pl.BlockSpec((tm, d), lambda i, j: (i, 0)),
            pl.BlockSpec((1, d), lambda i, j: (0, 0)),
            pl.BlockSpec((None, None, 1, d), lambda i, j: (i // per_b, 1, 0, 0)),
            pl.BlockSpec((None, None, 1, d), lambda i, j: (i // per_b, 0, 0, 0)),
            pl.BlockSpec((None, tn, d), lambda i, j: (layer, j, 0)),
            pl.BlockSpec((None, LANES, d), lambda i, j: (layer, 0, 0)),
        ] + cast_in_specs,
        out_specs=[
            pl.BlockSpec((tm, tn), lambda i, j: (i, j)),
            pl.BlockSpec((tm, LANES), lambda i, j: (i, 0)),
        ] + cast_out_specs,
        out_shape=[
            jax.ShapeDtypeStruct((n, c), BF16),
            jax.ShapeDtypeStruct((n, LANES), BF16),
        ] + cast_shapes,
        scratch_shapes=[pltpu.VMEM((tm, d), BF16)],
        compiler_params=_params(("arbitrary", "arbitrary"), BIG_VMEM_LIMIT_BYTES),
        name="in_proj",
    )(x2, g, mod, mod, w_main, w_lr, *[w for w, _ in casts])
    return outs[0], outs[1], outs[2:]


def _split3(x):
    hi = x.astype(BF16)
    r1 = x - hi.astype(F32)
    mid = r1.astype(BF16)
    lo = (r1 - mid.astype(F32)).astype(BF16)
    return hi, mid, lo


def _gla_kernel(q_ref, k_ref, v_ref, gr_ref, lr_ref, w2_ref, gb_ref, ng_ref, o_ref,
                qm_ref, km_ref, qe_ref, ke_ref, el_ref, st_ref, *, heads, dk, dv, gate_rows):
    seq = q_ref.shape[0]
    cs = GLA_CHUNK
    kw = heads * dk
    per_trip = gate_rows // cs
    row = lax.broadcasted_iota(jnp.int32, (cs, cs), 0)
    col = lax.broadcasted_iota(jnp.int32, (cs, cs), 1)
    causal = col <= row
    tril = causal.astype(BF16)
    scale = dk ** -0.5

    def decay(pi, carry):
        base = pl.multiple_of(pi * gate_rows, gate_rows)
        r = pl.ds(base, gate_rows)
        xg = jnp.dot(lr_ref[r, :], w2_ref[...], preferred_element_type=F32) + gb_ref[...]
        g = (jnp.minimum(xg, 0.0) - jnp.log(1.0 + jnp.exp(-jnp.abs(xg)))) * (1.0 / GLA_GATE_NORM)
        parts = jnp.concatenate(_split3(g), axis=1)
        q = q_ref[r, :].astype(F32) * scale
        k = k_ref[r, :].astype(F32)
        for c in range(per_trip):
            rows = slice(c * cs, (c + 1) * cs)
            out = pl.ds(base + c * cs, cs)
            bs = jnp.dot(tril, parts[rows, :], preferred_element_type=F32)
            b = bs[:, :kw] + bs[:, kw:2 * kw] + bs[:, 2 * kw:]
            b_last = b[cs - 1:cs, :]
            b_mid = b[cs // 2 - 1:cs // 2, :]
            qm_ref[out, :] = (q[rows, :] * jnp.exp(b - b_mid)).astype(BF16)
            km_ref[out, :] = (k[rows, :] * jnp.exp(b_mid - b)).astype(BF16)
            qe_ref[out, :] = (q[rows, :] * jnp.exp(b)).astype(BF16)
            ke_ref[out, :] = (k[rows, :] * jnp.exp(b_last - b)).astype(BF16)
            el_ref[pl.ds(pi * per_trip + c, 1), :] = jnp.exp(b_last)
        return carry

    lax.fori_loop(0, seq // gate_rows, decay, 0)
    st_ref[...] = jnp.zeros_like(st_ref)

    def chunk(ci, carry):
        r = pl.ds(pl.multiple_of(ci * cs, cs), cs)
        e_last = el_ref[pl.ds(ci, 1), :]
        hs = range(heads)
        ks = [slice(h * dk, (h + 1) * dk) for h in hs]
        vs = [slice(h * dv, (h + 1) * dv) for h in hs]
        attn = [lax.dot_general(qm_ref[r, ks[h]], km_ref[r, ks[h]], _NT, preferred_element_type=F32) for h in hs]
        v = [v_ref[r, vs[h]] for h in hs]
        st = [st_ref[h] for h in hs]
        o_inter = [lax.dot_general(qe_ref[r, ks[h]], st[h].astype(BF16), _NT, preferred_element_type=F32)
                   for h in hs]
        kv = [lax.dot_general(v[h], ke_ref[r, ks[h]], _TN, preferred_element_type=F32) for h in hs]
        attn = [jnp.where(causal, attn[h], 0.0).astype(BF16) for h in hs]
        o = [jnp.dot(attn[h], v[h], preferred_element_type=F32) + o_inter[h] for h in hs]
        for h in hs:
            st_ref[h] = st[h] * e_last[:, ks[h]] + kv[h]
        for h in hs:
            ms = jnp.mean(o[h] * o[h], axis=-1, keepdims=True)
            y = o[h] * lax.rsqrt(ms + EPS) * ng_ref[...]
            o_ref[r, vs[h]] = (y * jax.nn.silu(gr_ref[r, vs[h]].astype(F32))).astype(o_ref.dtype)
        return carry

    lax.fori_loop(0, seq // cs, chunk, 0, unroll=2)


def _gla(proj, lr, w2p, gate_b, norm_g, *, batch, seq, key_w, val_w, gate_rows=_Tiles.gla_decay_rows):
    n = proj.shape[0]
    dk = key_w // GLA_HEADS
    dv = val_w // GLA_HEADS
    vb = 2 * key_w // val_w
    rb = vb + 1
    kern = functools.partial(_gla_kernel, heads=GLA_HEADS, dk=dk, dv=dv, gate_rows=gate_rows)
    return pl.pallas_call(
        kern,
        grid=(batch,),
        in_specs=[
            pl.BlockSpec((seq, key_w), lambda b: (b, 0)),
            pl.BlockSpec((seq, key_w), lambda b: (b, 1)),
            pl.BlockSpec((seq, val_w), lambda b: (b, vb)),
            pl.BlockSpec((seq, val_w), lambda b: (b, rb)),
            pl.BlockSpec((seq, LANES), lambda b: (b, 0)),
            pl.BlockSpec((LANES, key_w), lambda b: (0, 0)),
            pl.BlockSpec((1, key_w), lambda b: (0, 0)),
            pl.BlockSpec((1, dv), lambda b: (0, 0)),
        ],
        out_specs=pl.BlockSpec((seq, val_w), lambda b: (b, 0)),
        out_shape=jax.ShapeDtypeStruct((n, val_w), BF16),
        scratch_shapes=[
            pltpu.VMEM((seq, key_w), BF16),
            pltpu.VMEM((seq, key_w), BF16),
            pltpu.VMEM((seq, key_w), BF16),
            pltpu.VMEM((seq, key_w), BF16),
            pltpu.VMEM((seq // GLA_CHUNK, key_w), F32),
            pltpu.VMEM((GLA_HEADS, dv, dk), F32),
        ],
        compiler_params=_params(("arbitrary",)),
        name="gla",
    )(proj, proj, proj, proj, lr, w2p, gate_b, norm_g)


def _moba_kernel(q_ref, k_ref, v_ref, o_ref, vt_ref, s_ref, pt_ref, *, heads):
    bs = MOBA_BLOCK
    hd = MOBA_HEAD_DIM
    seq = k_ref.shape[0]
    nb = seq // bs
    scale = hd ** -0.5
    hs = range(heads)
    cols = [slice(h * hd, (h + 1) * hd) for h in hs]

    krow = lax.broadcasted_iota(jnp.int32, (bs, bs), 0)
    qcol = lax.broadcasted_iota(jnp.int32, (bs, bs), 1)
    own_bias = jnp.where(krow <= qcol, 0.0, -jnp.inf)
    ones_row = (lax.broadcasted_iota(jnp.int32, (BF16_SUBLANES, seq), 0) == 0).astype(F32).astype(BF16)

    def select(h):
        km = jnp.mean(k_ref[:, cols[h]].astype(F32).reshape(nb, bs, hd), axis=1)
        km = jnp.concatenate([km, jnp.zeros((BF16_SUBLANES - nb, hd), F32)], axis=0).astype(BF16)
        sc = lax.dot_general(km, q_ref[:, cols[h]], _NT, preferred_element_type=F32)[0:nb, :]
        blk = lax.broadcasted_iota(jnp.int32, sc.shape, 0)
        qpos = lax.broadcasted_iota(jnp.int32, sc.shape, 1)
        past = (blk + 1) * bs <= qpos
        sc = jnp.where(past, sc, -jnp.inf)
        beaten = jnp.zeros(sc.shape, F32)
        for jp in range(nb):
            c = sc[jp:jp + 1, :]
            ahead = (c > sc) | ((c == sc) & (blk > jp))
            beaten = beaten + ahead.astype(F32)
        vt_ref[h, 0:hd, :] = v_ref[:, cols[h]].astype(F32).T.astype(BF16)
        vt_ref[h, hd:hd + BF16_SUBLANES, :] = ones_row
        return jnp.where(past & (beaten < MOBA_TOPK), 0.0, -jnp.inf)

    sel_bias = [select(h) for h in hs]

    def scores(h, i):
        nk = (i + 1) * bs
        s_ref[h, i % 2, 0:nk, :] = lax.dot_general(
            k_ref[0:nk, cols[h]], q_ref[i * bs:nk, cols[h]], _NT, preferred_element_type=F32)

    def softmax(h, i):
        slot = i % 2
        qs = slice(i * bs, (i + 1) * bs)
        nk = (i + 1) * bs
        biases = [sel_bias[h][j:j + 1, qs] for j in range(i)]
        m = (s_ref[h, slot, i * bs:nk, :] + own_bias).max(axis=0, keepdims=True)
        for j in range(i):
            m = jnp.maximum(m, s_ref[h, slot, j * bs:(j + 1) * bs, :].max(axis=0, keepdims=True) + biases[j])
        own = s_ref[h, slot, i * bs:nk, :] + own_bias
        pt_ref[h, slot, i * bs:nk, :] = jnp.exp2((own - m) * (scale * LOG2E)).astype(BF16)
        for j in range(i):
            p = jnp.exp2((s_ref[h, slot, j * bs:(j + 1) * bs, :] - (m - biases[j])) * (scale * LOG2E))
            pt_ref[h, slot, j * bs:(j + 1) * bs, :] = p.astype(BF16)

    def attend(h, i):
        nk = (i + 1) * bs
        ot = jnp.dot(vt_ref[h, :, 0:nk], pt_ref[h, i % 2, 0:nk, :], preferred_element_type=F32)
        o_ref[i * bs:nk, cols[h]] = (ot[0:hd, :] / ot[hd:hd + 1, :]).T.astype(o_ref.dtype)

    for h in hs:
        scores(h, 0)
    for i in range(nb):
        for h in hs:
            if i + 1 < nb:
                scores(h, i + 1)
        for h in hs:
            if i > 0:
                attend(h, i - 1)
        for h in hs:
            softmax(h, i)
    for h in hs:
        attend(h, nb - 1)


def _moba(proj, *, batch, seq, col0, width, heads_per_step=_Tiles.moba_heads):
    n = proj.shape[0]
    hd = MOBA_HEAD_DIM
    wb = heads_per_step * hd
    groups = width // wb
    qb = col0 // wb
    kb = qb + groups
    vb = kb + groups
    kern = functools.partial(_moba_kernel, heads=heads_per_step)
    return pl.pallas_call(
        kern,
        grid=(batch, groups),
        in_specs=[
            pl.BlockSpec((seq, wb), lambda b, h: (b, qb + h)),
            pl.BlockSpec((seq, wb), lambda b, h: (b, kb + h)),
            pl.BlockSpec((seq, wb), lambda b, h: (b, vb + h)),
        ],
        out_specs=pl.BlockSpec((seq, wb), lambda b, h: (b, h)),
        out_shape=jax.ShapeDtypeStruct((n, width), BF16),
        scratch_shapes=[
            pltpu.VMEM((heads_per_step, hd + BF16_SUBLANES, seq), BF16),
            pltpu.VMEM((heads_per_step, 2, seq, MOBA_BLOCK), F32),
            pltpu.VMEM((heads_per_step, 2, seq, MOBA_BLOCK), BF16),
        ],
        compiler_params=_params(("arbitrary", "arbitrary")),
        name="moba",
    )(proj, proj, proj)


def _merge_kernel(og_ref, om_ref, wg_ref, wm_ref, g1_ref, g2_ref, wo_ref, x_ref, gt_ref, o_ref):
    yg = jnp.dot(og_ref[...], wg_ref[...], preferred_element_type=F32)
    ym = jnp.dot(om_ref[...], wm_ref[...], preferred_element_type=F32)
    z = jax.nn.sigmoid(g1_ref[...].astype(F32)) * yg + jax.nn.sigmoid(g2_ref[...].astype(F32)) * ym
    y = jnp.dot(z.astype(BF16), wo_ref[...], preferred_element_type=F32)
    o_ref[...] = x_ref[...] + gt_ref[...] * y


def _merge(o_gla, o_moba, w_up_gla, w_up_moba, proj, gate_col0, w_out, x2, mod, layer, *, seq,
           tm=_Tiles.merge_rows):
    n, d = x2.shape
    per_b = seq // tm
    g1b = gate_col0 // d
    kg = o_gla.shape[1]
    km = o_moba.shape[1]
    resident = dict(pipeline_mode=pl.Buffered(1))
    return pl.pallas_call(
        _merge_kernel,
        grid=(n // tm,),
        in_specs=[
            pl.BlockSpec((tm, kg), lambda i: (i, 0)),
            pl.BlockSpec((tm, km), lambda i: (i, 0)),
            pl.BlockSpec((None, kg, d), lambda i: (layer, 0, 0), **resident),
            pl.BlockSpec((None, km, d), lambda i: (layer, 0, 0), **resident),
            pl.BlockSpec((tm, d), lambda i: (i, g1b)),
            pl.BlockSpec((tm, d), lambda i: (i, g1b + 1)),
            pl.BlockSpec((None, d, d), lambda i: (layer, 0, 0), **resident),
            pl.BlockSpec((tm, d), lambda i: (i, 0)),
            pl.BlockSpec((None, None, 1, d), lambda i: (i // per_b, 2, 0, 0)),
        ],
        out_specs=pl.BlockSpec((tm, d), lambda i: (i, 0)),
        out_shape=jax.ShapeDtypeStruct((n, d), F32),
        compiler_params=_params(("arbitrary",)),
        name="merge",
    )(o_gla, o_moba, w_up_gla, w_up_moba, proj, proj, w_out, x2, mod)


def _ffn_kernel(x_ref, g_ref, sc_ref, sh_ref, gt_ref, wa_ref, wu_ref, wo_ref, fg_ref, o_ref, h_ref, *, final_norm):
    j = pl.program_id(1)

    def part(h):
        a = jnp.dot(h, wa_ref[...], preferred_element_type=F32)
        u = jnp.dot(h, wu_ref[...], preferred_element_type=F32)
        return jnp.dot((jax.nn.silu(a) * u).astype(BF16), wo_ref[...], preferred_element_type=F32)

    @pl.when(j == 0)
    def _():
        h = _mod_norm(x_ref[...], g_ref[...], sc_ref[...], sh_ref[...]).astype(BF16)
        h_ref[...] = h
        o_ref[...] = part(h)

    last = pl.num_programs(1) - 1

    @pl.when((j > 0) & (j < last))
    def _():
        o_ref[...] += part(h_ref[...])

    @pl.when(j == last)
    def _():
        y = x_ref[...] + gt_ref[...] * (o_ref[...] + part(h_ref[...]))
        if final_norm:
            ms = jnp.mean(y * y, axis=-1, keepdims=True)
            y = y * lax.rsqrt(ms + EPS) * fg_ref[...]
        o_ref[...] = y


def _ffn(x2, g, mod, w_in, w_out, final_g, layer, *, seq, final_norm,
         tm=_Tiles.ffn_rows, tf=_Tiles.ffn_cols):
    n, d = x2.shape
    dff = w_out.shape[1]
    per_b = seq // tm
    nf = dff // tf
    kern = functools.partial(_ffn_kernel, final_norm=final_norm)
    return pl.pallas_call(
        kern,
        grid=(n // tm, nf),
        in_specs=[
            pl.BlockSpec((tm, d), lambda i, j: (i, 0)),
            pl.BlockSpec((1, d), lambda i, j: (0, 0)),
            pl.BlockSpec((None, None, 1, d), lambda i, j: (i // per_b, 4, 0, 0)),
            pl.BlockSpec((None, None, 1, d), lambda i, j: (i // per_b, 3, 0, 0)),
            pl.BlockSpec((None, None, 1, d), lambda i, j: (i // per_b, 5, 0, 0)),
            pl.BlockSpec((None, d, tf), lambda i, j: (layer, 0, j)),
            pl.BlockSpec((None, d, tf), lambda i, j: (layer, 0, nf + j)),
            pl.BlockSpec((None, tf, d), lambda i, j: (layer, j, 0)),
            pl.BlockSpec((1, d), lambda i, j: (0, 0)),
        ],
        out_specs=pl.BlockSpec((tm, d), lambda i, j: (i, 0)),
        out_shape=jax.ShapeDtypeStruct((n, d), F32),
        scratch_shapes=[pltpu.VMEM((tm, d), BF16)],
        compiler_params=_params(("arbitrary", "arbitrary"), BIG_VMEM_LIMIT_BYTES),
        name="ffn",
    )(x2, g, mod, mod, mod, w_in, w_in, w_out, final_g)


def kernel(x, c, ada_w, ada_b, norm1_g, w_in, gla_gate_w2, gla_gate_b, gla_norm_g, w_up_gla, w_up_moba,
           w_out, norm2_g, w_ffn_in, w_ffn_out, final_g):
    batch, seq, d = x.shape
    depth = ada_w.shape[0]
    rank, key_w = gla_gate_w2.shape[1:]
    val_w = w_up_gla.shape[1]
    moba_w = w_up_moba.shape[1]
    n = batch * seq
    lr0 = 2 * key_w + 2 * val_w
    moba0 = lr0
    gate0 = moba0 + 3 * moba_w

    mod = _adaln(c, ada_w, ada_b).reshape(depth, batch, 6, 1, d)
    w_main, w_lr = _repack_w_in(jnp.swapaxes(w_in, 1, 2), lr0, rank)
    x2 = x.reshape(n, d)
    for l in range(depth):
        w2p = jnp.pad(gla_gate_w2[l], ((0, LANES - rank), (0, 0))).astype(BF16)
        proj, lr, (w_up_gla_b, w_up_moba_b, w_out_b, w_ffn_in_b, w_ffn_out_b) = _in_proj(
            x2, norm1_g[l][None], mod[l], w_main, w_lr, l, [w_up_gla, w_up_moba, w_out, w_ffn_in, w_ffn_out],
            seq=seq)
        o_gla = _gla(proj, lr, w2p, gla_gate_b[l][None], gla_norm_g[l][None],
                     batch=batch, seq=seq, key_w=key_w, val_w=val_w)
        o_moba = _moba(proj, batch=batch, seq=seq, col0=moba0, width=moba_w)
        x2 = _merge(o_gla, o_moba, w_up_gla_b, w_up_moba_b, proj, gate0, w_out_b, x2, mod[l], 0, seq=seq)
        x2 = _ffn(x2, norm2_g[l][None], mod[l], w_ffn_in_b, w_ffn_out_b, final_g[None], 0,
                  seq=seq, final_norm=(l == depth - 1))
    return x2.reshape(batch, seq, d)
```
